```python
import math
import jax, jax.numpy as jnp
from jax import lax
import numpy as np

D_MODEL = 1024
BATCH = 8
SEQ = 2048
DEPTH = 2
DEC_BATCH = 32
DEC_SEQ = 4
PAST_LEN = 8192
PAGE_SIZE = 128

HG_HEADS = 4
HG_DIM = 128
HG_WIDTH = HG_HEADS * HG_DIM
SSM_HEADS = 8
SSM_HEAD_DIM = 64
SSM_WIDTH = SSM_HEADS * SSM_HEAD_DIM
SSM_GROUPS = 2
SSM_STATE = 128
SSM_CONV = 4
SSM_CONV_DIM = SSM_WIDTH + 2 * SSM_GROUPS * SSM_STATE
ATT_WINDOWS = (128, 512, 2048)
ATT_DILATIONS = (1, 4, 16)
ATT_N_GROUPS = 3
ATT_HEADS = 4
ATT_HEAD_DIM = 64
ATT_WIDTH = ATT_HEADS * ATT_HEAD_DIM
ATT_KEYS = 128
ATT_BLOCK = 128
ATT_SCALE = ATT_HEAD_DIM ** -0.5
SCAN_CHUNK = 64
OFF_HG = 0
OFF_SSM = OFF_HG + 4 * HG_WIDTH
OFF_ATT = OFF_SSM + SSM_WIDTH + SSM_CONV_DIM + SSM_HEADS
OFF_GATE = OFF_ATT + ATT_N_GROUPS * 3 * ATT_WIDTH
N_IN = OFF_GATE + 3 * D_MODEL
D_FF = 11 * D_MODEL // 4
N_EXPERTS = 8
TOP_K = 2
D_FF_EXPERT = 7 * D_MODEL // 2
N_DENSE = (DEPTH + 1) // 2
N_MOE = DEPTH // 2
EPS = 1e-6

kernel_name = 'hybrid_hgrn2_ssd_dilated_decoder_step'


def rms_norm(x, gain):
    xf = x.astype(jnp.float32)
    y = xf * lax.rsqrt(jnp.mean(xf * xf, axis=-1, keepdims=True) + EPS)
    return (y * gain.astype(jnp.float32)).astype(x.dtype)


def _pad_time(a, pad):
    return jnp.pad(a, [(0, 0), (0, pad)] + [(0, 0)] * (a.ndim - 2))


def _to_chunks(a, c):
    b, l = a.shape[:2]
    return jnp.moveaxis(a.reshape((b, l // c, c) + a.shape[2:]), 1, 0)


def _from_chunks(a):
    n, b, c = a.shape[:3]
    return jnp.moveaxis(a, 0, 1).reshape((b, n * c) + a.shape[3:])


def hgrn2_recurrence(q, k, logf, v, s0):
    L = q.shape[1]
    c = min(SCAN_CHUNK, L)
    pad = (-L) % c
    qs, ks, fs, vs = [_to_chunks(_pad_time(a.astype(jnp.float32), pad), c) for a in (q, k, logf, v)]
    causal = jnp.tril(jnp.ones((c, c), dtype=bool))

    def step(S, inp):
        qc, kc, fc, vc = inp
        G = jnp.cumsum(fc, axis=1)
        rel = jnp.where(causal[None, :, :, None, None], G[:, :, None] - G[:, None, :], -jnp.inf)
        attn = jnp.einsum('bthd,bshd,btshd->bhts', qc, kc, jnp.exp(rel))
        o = (jnp.einsum('bhts,bshv->bthv', attn, vc)
             + jnp.einsum('bthd,bhdv->bthv', qc * jnp.exp(G), S))
        g_last = G[:, -1]
        k_dec = kc * jnp.exp(g_last[:, None] - G)
        S = jnp.exp(g_last)[..., None] * S + jnp.einsum('bshd,bshv->bhdv', k_dec, vc)
        return S, o

    S, o = lax.scan(step, s0.astype(jnp.float32), (qs, ks, fs, vs))
    return _from_chunks(o)[:, :L], S


def ssd_recurrence(x, dt, a, bm, cm, h0):
    b, L, H, P = x.shape
    G, N = bm.shape[2], bm.shape[3]
    R = H // G
    c = min(SCAN_CHUNK, L)
    pad = (-L) % c
    x = x.reshape(b, L, G, R, P)
    dt = dt.reshape(b, L, G, R)
    xs, dts, bs, cs = [_to_chunks(_pad_time(t.astype(jnp.float32), pad), c) for t in (x, dt, bm, cm)]
    la = a.astype(jnp.float32).reshape(G, R)
    causal = jnp.tril(jnp.ones((c, c), dtype=bool))

    def step(h, inp):
        xc, dtc, bc, cc = inp
        cum = jnp.cumsum(dtc * la, axis=1)
        rel = jnp.where(causal[None, :, :, None, None], cum[:, :, None] - cum[:, None, :], -jnp.inf)
        cb = jnp.einsum('btgn,bsgn->btsg', cc, bc)
        w = cb[..., None] * jnp.exp(rel) * dtc[:, None]
        y = jnp.einsum('btsgr,bsgrp->btgrp', w, xc)
        y = y + jnp.einsum('btgn,bgrpn->btgrp', cc, h) * jnp.exp(cum)[..., None]
        c_last = cum[:, -1]
        w_in = jnp.exp(c_last[:, None] - cum) * dtc
        h = jnp.exp(c_last)[..., None, None] * h + jnp.einsum('bsgrp,bsgn,bsgr->bgrpn', xc, bc, w_in)
        return h, y

    h, y = lax.scan(step, h0.astype(jnp.float32).reshape(b, G, R, P, N), (xs, dts, bs, cs))
    y = _from_chunks(y)[:, :L].reshape(b, L, H, P)
    return y, h.reshape(b, H, P, N)


def causal_dwconv(prefix, u, w, bias):
    L = u.shape[1]
    full = jnp.concatenate([prefix.astype(u.dtype), u], axis=1)
    y = bias
    for j in range(SSM_CONV):
        y = y + full[:, j:j + L] * w[j]
    return y, full[:, -(SSM_CONV - 1):]


def dilated_attention_prompt(q, k, v, dil):
    b, L, H, D = q.shape
    M = L // dil
    P = min(ATT_BLOCK, M)
    Mp = -(-M // P) * P
    nb = Mp // P

    def classes(a):
        return jnp.pad(a.reshape(b, M, dil, H, D), ((0, 0), (0, Mp - M), (0, 0), (0, 0), (0, 0)))

    def band(a):
        ap = jnp.pad(a, ((0, 0), (P, 0), (0, 0), (0, 0), (0, 0)))
        prev = ap[:, :Mp].reshape(b, nb, P, dil, H, D)
        cur = ap[:, P:].reshape(b, nb, P, dil, H, D)
        return jnp.concatenate([prev, cur], axis=2)

    qb = classes(q).reshape(b, nb, P, dil, H, D)
    kb = band(classes(k))
    vb = band(classes(v)).astype(jnp.float32)
    s = jnp.einsum('bnqrhd,bnkrhd->bnrhqk', qb, kb, preferred_element_type=jnp.float32) * ATT_SCALE
    iq = jnp.arange(P)[:, None]
    ik = jnp.arange(2 * P)[None, :]
    dist = P + iq - ik
    blk = jnp.arange(nb)[:, None, None]
    valid = (dist >= 0) & (dist <= ATT_KEYS) & (blk * P - P + ik >= 0)
    s = jnp.where(valid[None, :, None, None], s, -jnp.inf)
    m = jnp.max(s, axis=-1, keepdims=True)
    p = jnp.exp(s - m)
    den = jnp.sum(p, axis=-1, keepdims=True)
    o = jnp.einsum('bnrhqk,bnkrhd->bnqrhd', p, vb) / jnp.transpose(den, (0, 1, 4, 2, 3, 5))
    lse = jnp.transpose((m + jnp.log(den))[..., 0], (0, 1, 4, 2, 3))
    o = o.reshape(b, Mp, dil, H, D)[:, :M].reshape(b, L, H, D)
    lse = lse.reshape(b, Mp, dil, H)[:, :M].reshape(b, L, H)
    return o, lse


def dilated_attention_sample(q, k, v, buf, dil):
    wb, S = buf.shape[1], q.shape[1]
    keys = jnp.concatenate([buf[:, :, 0].astype(k.dtype), k], axis=1)
    vals = jnp.concatenate([buf[:, :, 1].astype(v.dtype), v], axis=1)
    idx = wb + jnp.arange(S)[:, None] - dil * jnp.arange(ATT_KEYS + 1)[None, :]
    valid = idx >= 0
    idx = jnp.maximum(idx, 0)
    kg = keys[:, idx]
    vg = vals[:, idx].astype(jnp.float32)
    s = jnp.einsum('bshd,bsjhd->bshj', q, kg, preferred_element_type=jnp.float32) * ATT_SCALE
    s = jnp.where(valid[None, :, None, :], s, -jnp.inf)
    m = jnp.max(s, axis=-1, keepdims=True)
    p = jnp.exp(s - m)
    den = jnp.sum(p, axis=-1, keepdims=True)
    o = jnp.einsum('bshj,bsjhd->bshd', p, vg) / den
    return o, (m + jnp.log(den))[..., 0]


def token_mixers(h, lb, w_in, hg_norm, conv_w, conv_b, dt_bias, a_log, d_skip, ssm_norm,
                 q_norm, k_norm, w_out_hg, w_out_ssm, w_out_att, w_o,
                 hg_s0, ssm_h0, conv_prefix, kv_bufs):
    b, L, _ = h.shape
    proj = h @ w_in

    hq, hf, hi, hg = jnp.split(proj[..., OFF_HG:OFF_SSM], 4, axis=-1)
    heads = lambda t: t.reshape(b, L, HG_HEADS, HG_DIM)
    fgate = lb + (1.0 - lb) * jax.nn.sigmoid(hf.astype(jnp.float32))
    o_hg, s_hg = hgrn2_recurrence(heads(jax.nn.silu(hq)), heads(1.0 - fgate),
                                  heads(jnp.log(fgate)), heads(hi), hg_s0)
    y_hg = (rms_norm(o_hg, hg_norm) * jax.nn.silu(heads(hg).astype(jnp.float32)))
    y_hg = y_hg.reshape(b, L, HG_WIDTH).astype(h.dtype)

    ssm = proj[..., OFF_SSM:OFF_ATT]
    z = ssm[..., :SSM_WIDTH]
    xbc = ssm[..., SSM_WIDTH:SSM_WIDTH + SSM_CONV_DIM]
    dt_raw = ssm[..., SSM_WIDTH + SSM_CONV_DIM:]
    xbc_c, conv_state = causal_dwconv(conv_prefix, xbc, conv_w, conv_b)
    xbc_c = jax.nn.silu(xbc_c)
    gn = SSM_GROUPS * SSM_STATE
    xs = xbc_c[..., :SSM_WIDTH].reshape(b, L, SSM_HEADS, SSM_HEAD_DIM)
    bm = xbc_c[..., SSM_WIDTH:SSM_WIDTH + gn].reshape(b, L, SSM_GROUPS, SSM_STATE)
    cm = xbc_c[..., SSM_WIDTH + gn:].reshape(b, L, SSM_GROUPS, SSM_STATE)
    dt = jax.nn.softplus(dt_raw.astype(jnp.float32) + dt_bias.astype(jnp.float32))
    a = -jnp.exp(a_log.astype(jnp.float32))
    y, s_ssm = ssd_recurrence(xs, dt, a, bm, cm, ssm_h0)
    y = y + d_skip.astype(jnp.float32)[:, None] * xs.astype(jnp.float32)
    y_ssm = rms_norm(y.reshape(b, L, SSM_WIDTH) * jax.nn.silu(z.astype(jnp.float32)), ssm_norm)
    y_ssm = y_ssm.astype(h.dtype)

    att = proj[..., OFF_ATT:OFF_GATE].reshape(b, L, ATT_N_GROUPS, 3, ATT_HEADS, ATT_HEAD_DIM)
    outs, lses, new_kv = [], [], []
    for g in range(ATT_N_GROUPS):
        qg = rms_norm(att[:, :, g, 0], q_norm[g])
        kg = rms_norm(att[:, :, g, 1], k_norm[g])
        vg = att[:, :, g, 2]
        if kv_bufs is None:
            o, lse = dilated_attention_prompt(qg, kg, vg, ATT_DILATIONS[g])
            keep = min(ATT_WINDOWS[g], L)
            new_kv.append(jnp.stack([kg, vg], axis=2)[:, L - keep:])
        else:
            o, lse = dilated_attention_sample(qg, kg, vg, kv_bufs[g], ATT_DILATIONS[g])
            new_kv.append(jnp.stack([kg, vg], axis=2))
        outs.append(o)
        lses.append(lse)
    wts = jax.nn.softmax(jnp.stack(lses), axis=0)
    y_att = jnp.sum(wts[..., None] * jnp.stack(outs), axis=0)
    y_att = y_att.reshape(b, L, ATT_WIDTH).astype(h.dtype)

    gates = jax.nn.sigmoid(proj[..., OFF_GATE:].astype(jnp.float32)).reshape(b, L, 3, D_MODEL)
    mixed = (gates[:, :, 0] * (y_hg @ w_out_hg)
             + gates[:, :, 1] * (y_ssm @ w_out_ssm)
             + gates[:, :, 2] * (y_att @ w_out_att))
    out = mixed.astype(h.dtype) @ w_o
    return out, (s_hg, s_ssm, conv_state, new_kv[0], new_kv[1], new_kv[2])


def swiglu(h, w_up, w_down):
    g, u = jnp.split(h @ w_up, 2, axis=-1)
    return (jax.nn.silu(g) * u) @ w_down


def moe_swiglu(h, w_router, w_up, w_down):
    logits = jnp.einsum('bld,de->ble', h, w_router, preferred_element_type=jnp.float32)
    top_v, top_i = lax.top_k(logits, TOP_K)
    top_w = jax.nn.softmax(top_v, axis=-1)
    combine = jnp.sum(top_w[..., None] * jax.nn.one_hot(top_i, N_EXPERTS, dtype=jnp.float32), axis=-2)
    out = jnp.zeros(h.shape, jnp.float32)
    for e in range(N_EXPERTS):
        out = out + combine[..., e:e + 1] * swiglu(h, w_up[e], w_down[e])
    return out.astype(h.dtype)


def setup_inputs(seed: int = 0) -> dict:
    key = jax.random.key(seed)
    ks = iter(jax.random.split(key, 40))
    f32 = jnp.float32
    nrm = lambda shape, scale: scale * jax.random.normal(next(ks), shape, f32)
    gain = lambda shape: 1.0 + 0.02 * jax.random.normal(next(ks), shape, f32)
    kv_shape = lambda w: (DEPTH, DEC_BATCH, min(w, PAST_LEN), 2, ATT_HEADS, ATT_HEAD_DIM)
    dt0 = jnp.exp(jax.random.uniform(next(ks), (DEPTH, SSM_HEADS), f32, math.log(1e-3), math.log(0.1)))
    dt_bias = dt0 + jnp.log(-jnp.expm1(-dt0))
    a_log = jnp.log(jax.random.uniform(next(ks), (DEPTH, SSM_HEADS), f32, 1.0, 16.0))
    return {
        'x_prompt': nrm((BATCH, SEQ, D_MODEL), 1.0),
        'x_sample': nrm((DEC_BATCH, DEC_SEQ, D_MODEL), 1.0),
        'state_hgrn': nrm((DEPTH, DEC_BATCH, HG_HEADS, HG_DIM, HG_DIM), 0.5),
        'state_ssm': nrm((DEPTH, DEC_BATCH, SSM_HEADS, SSM_HEAD_DIM, SSM_STATE), 0.3),
        'state_conv': nrm((DEPTH, DEC_BATCH, SSM_CONV - 1, SSM_CONV_DIM), 1.0),
        'cache_kv_w128': nrm(kv_shape(ATT_WINDOWS[0]), 1.0),
        'cache_kv_w512': nrm(kv_shape(ATT_WINDOWS[1]), 1.0),
        'cache_kv_w2048': nrm(kv_shape(ATT_WINDOWS[2]), 1.0),
        'w_in': nrm((DEPTH, D_MODEL, N_IN), D_MODEL ** -0.5),
        'mix_norm': gain((DEPTH, D_MODEL)),
        'hgrn_lb_logits': nrm((DEPTH, HG_WIDTH), 1.0),
        'hgrn_norm': gain((DEPTH, HG_DIM)),
        'ssm_conv_w': nrm((DEPTH, SSM_CONV, SSM_CONV_DIM), SSM_CONV ** -0.5),
        'ssm_conv_b': nrm((DEPTH, SSM_CONV_DIM), 0.02),
        'ssm_dt_bias': dt_bias,
        'ssm_a_log': a_log,
        'ssm_d': gain((DEPTH, SSM_HEADS)),
        'ssm_norm': gain((DEPTH, SSM_WIDTH)),
        'att_q_norm': gain((DEPTH, ATT_N_GROUPS, ATT_HEAD_DIM)),
        'att_k_norm': gain((DEPTH, ATT_N_GROUPS, ATT_HEAD_DIM)),
        'w_out_hgrn': nrm((DEPTH, HG_WIDTH, D_MODEL), HG_WIDTH ** -0.5),
        'w_out_ssm': nrm((DEPTH, SSM_WIDTH, D_MODEL), SSM_WIDTH ** -0.5),
        'w_out_att': nrm((DEPTH, ATT_WIDTH, D_MODEL), ATT_WIDTH ** -0.5),
        'w_o': nrm((DEPTH, D_MODEL, D_MODEL), D_MODEL ** -0.5),
        'ffn_norm': gain((DEPTH, D_MODEL)),
        'w_ffn_up': nrm((N_DENSE, D_MODEL, 2 * D_FF), D_MODEL ** -0.5),
        'w_ffn_down': nrm((N_DENSE, D_FF, D_MODEL), D_FF ** -0.5),
        'w_router': nrm((N_MOE, D_MODEL, N_EXPERTS), D_MODEL ** -0.5),
        'w_moe_up': nrm((N_MOE, N_EXPERTS, D_MODEL, 2 * D_FF_EXPERT), D_MODEL ** -0.5),
        'w_moe_down': nrm((N_MOE, N_EXPERTS, D_FF_EXPERT, D_MODEL), D_FF_EXPERT ** -0.5),
    }


def reference(x_prompt, x_sample, state_hgrn, state_ssm, state_conv, cache_kv_w128, cache_kv_w512,
              cache_kv_w2048, w_in, mix_norm, hgrn_lb_logits, hgrn_norm, ssm_conv_w, ssm_conv_b,
              ssm_dt_bias, ssm_a_log, ssm_d, ssm_norm, att_q_norm, att_k_norm, w_out_hgrn, w_out_ssm,
              w_out_att, w_o, ffn_norm, w_ffn_up, w_ffn_down, w_router, w_moe_up, w_moe_down):
    lb_soft = jax.nn.softmax(hgrn_lb_logits.astype(jnp.float32), axis=0)
    lb_table = jnp.cumsum(lb_soft, axis=0) - lb_soft[0]

    def run(x, layer_state):
        new_states = []
        for l in range(DEPTH):
            hg_s0, ssm_h0, conv_prefix, kv_bufs = layer_state(l, x.shape[0], x.dtype)
            h = rms_norm(x, mix_norm[l])
            mix_out, st = token_mixers(h, lb_table[l], w_in[l], hgrn_norm[l], ssm_conv_w[l], ssm_conv_b[l],
                                       ssm_dt_bias[l], ssm_a_log[l], ssm_d[l], ssm_norm[l],
                                       att_q_norm[l], att_k_norm[l], w_out_hgrn[l], w_out_ssm[l],
                                       w_out_att[l], w_o[l], hg_s0, ssm_h0, conv_prefix, kv_bufs)
            x = x + mix_out
            h = rms_norm(x, ffn_norm[l])
            if l % 2 == 0:
                x = x + swiglu(h, w_ffn_up[l // 2], w_ffn_down[l // 2])
            else:
                x = x + moe_swiglu(h, w_router[l // 2], w_moe_up[l // 2], w_moe_down[l // 2])
            new_states.append(st)
        return x, [jnp.stack(f) for f in zip(*new_states)]

    def fresh_state(l, b, dtype):
        return (jnp.zeros((b, HG_HEADS, HG_DIM, HG_DIM), jnp.float32),
                jnp.zeros((b, SSM_HEADS, SSM_HEAD_DIM, SSM_STATE), jnp.float32),
                jnp.zeros((b, SSM_CONV - 1, SSM_CONV_DIM), dtype),
                None)

    def cached_state(l, b, dtype):
        return (state_hgrn[l], state_ssm[l], state_conv[l],
                (cache_kv_w128[l], cache_kv_w512[l], cache_kv_w2048[l]))

    y_prompt, (p_hgrn, p_ssm, p_conv, p_kv128, p_kv512, p_kv2048) = run(x_prompt, fresh_state)
    y_sample, (s_hgrn, s_ssm, s_conv, s_kv128, s_kv512, s_kv2048) = run(x_sample, cached_state)
    return (y_prompt, y_sample, p_hgrn, p_ssm, p_conv, p_kv128, p_kv512, p_kv2048,
            s_hgrn, s_ssm, s_conv, s_kv128, s_kv512, s_kv2048)
```

```python
import functools

import jax
import jax.numpy as jnp
from jax import lax
from jax.experimental import pallas as pl
from jax.experimental.pallas import tpu as pltpu

F32 = jnp.float32
BF16 = jnp.bfloat16

D_MODEL = 1024
HG_HEADS, HG_DIM = 4, 128
HG_WIDTH = HG_HEADS * HG_DIM
SSM_HEADS, SSM_HEAD_DIM, SSM_GROUPS, SSM_STATE, SSM_CONV = 8, 64, 2, 128, 4
SSM_WIDTH = SSM_HEADS * SSM_HEAD_DIM
SSM_CONV_DIM = SSM_WIDTH + 2 * SSM_GROUPS * SSM_STATE
ATT_WINDOWS = (128, 512, 2048)
ATT_DILATIONS = (1, 4, 16)
ATT_HEADS, ATT_HEAD_DIM = 4, 64
ATT_WIDTH = ATT_HEADS * ATT_HEAD_DIM
ATT_KEYS = 128
ATT_SCALE = ATT_HEAD_DIM ** -0.5
N_EXPERTS = 8
EPS = 1e-6
NEG = -1e30

_OFF_SSM = 4 * HG_WIDTH
_OFF_ATT = _OFF_SSM + SSM_WIDTH + SSM_CONV_DIM + SSM_HEADS
_OFF_GATE = _OFF_ATT + 3 * 3 * ATT_WIDTH

A16_GATE, A16_XBC, A16_HQ, A16_HI, A16_HG, A16_Z, A16_ATT = 0, 3072, 4096, 4608, 5120, 5632, 6144
W16 = 8448
A32_HF, A32_DT = 0, 512
W32 = 640

LANE = 128
SAMPLE_ROWS = 16
VMEM_LIMIT = 56 * 2 ** 20


def _cparams(sem):
    return pltpu.CompilerParams(dimension_semantics=sem, vmem_limit_bytes=VMEM_LIMIT)


def _dot(a, b):
    return jnp.dot(a, b, preferred_element_type=F32)


def _dot_nt(a, b):
    return lax.dot_general(a, b, (((1,), (1,)), ((), ())), preferred_element_type=F32)


def _dot_tn(a, b):
    return lax.dot_general(a, b, (((0,), (0,)), ((), ())), preferred_element_type=F32)


def _split3(x):
    hi = x.astype(BF16)
    r = x - hi.astype(F32)
    mid = r.astype(BF16)
    lo = (r - mid.astype(F32)).astype(BF16)
    return hi, mid, lo


def _cumsum_rows(x, tril_bf16):
    hi, mid, lo = _split3(x)
    return _dot(tril_bf16, hi) + _dot(tril_bf16, mid) + _dot(tril_bf16, lo)


def _silu(x):
    return x * jax.nn.sigmoid(x)


def _tril_bf16(n):
    r = lax.broadcasted_iota(jnp.int32, (n, n), 0)
    c = lax.broadcasted_iota(jnp.int32, (n, n), 1)
    return jnp.where(c <= r, 1.0, 0.0).astype(BF16)


def _norm_proj_kernel(x_ref, g_ref, w16_ref, w32_ref, o16_ref, o32_ref, *, chunk):
    x = x_ref[...]
    h = (x * lax.rsqrt(jnp.mean(x * x, axis=-1, keepdims=True) + EPS) * g_ref[...]).astype(BF16)
    for c0 in range(0, W16, chunk):
        o16_ref[:, c0:c0 + chunk] = _dot(h, w16_ref[:, c0:c0 + chunk]).astype(BF16)
    o32_ref[...] = _dot(h, w32_ref[...])


def _norm_proj(x, gain, w16, w32, tm):
    T = x.shape[0]
    const = lambda i: (0, 0)
    return pl.pallas_call(
        functools.partial(_norm_proj_kernel, chunk=768),
        grid=(pl.cdiv(T, tm),),
        in_specs=[
            pl.BlockSpec((tm, D_MODEL), lambda i: (i, 0)),
            pl.BlockSpec((1, D_MODEL), const),
            pl.BlockSpec((D_MODEL, W16), const, pipeline_mode=pl.Buffered(1)),
            pl.BlockSpec((D_MODEL, W32), const, pipeline_mode=pl.Buffered(1)),
        ],
        out_specs=[
            pl.BlockSpec((tm, W16), lambda i: (i, 0)),
            pl.BlockSpec((tm, W32), lambda i: (i, 0)),
        ],
        out_shape=[jax.ShapeDtypeStruct((T, W16), BF16), jax.ShapeDtypeStruct((T, W32), F32)],
        compiler_params=_cparams(("parallel",)),
        name="norm_proj",
    )(x, gain, w16, w32)


def _hgrn_kernel(q_ref, f_ref, i_ref, g_ref, lb_ref, gn_ref, s0_ref, y_ref, sout_ref, st_scr,
                 *, C, sub, valid, nchunks):
    ci = pl.program_id(1)

    @pl.when(ci == 0)
    def _():
        st_scr[...] = s0_ref[0]

    lb = lb_ref[...]
    fgate = lb + (1.0 - lb) * jax.nn.sigmoid(f_ref[...])
    logf = jnp.log(fgate)
    kk = 1.0 - fgate
    if valid < C:
        rows = lax.broadcasted_iota(jnp.int32, (C, HG_WIDTH), 0)
        logf = jnp.where(rows < valid, logf, 0.0)
        kk = jnp.where(rows < valid, kk, 0.0)
    qa = _silu(q_ref[...].astype(F32))
    va = i_ref[...].astype(F32)
    ga = g_ref[...].astype(F32)
    G = _cumsum_rows(logf, _tril_bf16(C))

    lane_c = lax.broadcasted_iota(jnp.int32, (sub, C), 1)
    row_c = lax.broadcasted_iota(jnp.int32, (sub, C), 0)
    krow = lax.broadcasted_iota(jnp.int32, (C, HG_DIM), 0)
    outs = []
    for h in range(HG_HEADS):
        sl = slice(h * HG_DIM, (h + 1) * HG_DIM)
        Gh, qh, kh, vh = G[:, sl], qa[:, sl], kk[:, sl], va[:, sl]
        vb = vh.astype(BF16)
        st = st_scr[sl, :]
        strips = []
        for i in range(C // sub):
            r0 = i * sub
            Gb, qb, kb = Gh[r0:r0 + sub], qh[r0:r0 + sub], kh[r0:r0 + sub]
            P = jnp.zeros((sub, C), F32)
            for s in range(sub):
                e = jnp.exp(jnp.minimum(Gb - Gb[s:s + 1], 0.0))
                col = jnp.sum(qb * (kb[s:s + 1] * e), axis=-1, keepdims=True)
                P = jnp.where(lane_c == r0 + s, col, P)
            strip = jnp.where(lane_c <= row_c + r0, P, 0.0)
            if i > 0:
                Gr = Gh[r0 - 1:r0]
                qi = qb * jnp.exp(Gb - Gr)
                kt = kh * jnp.exp(jnp.where(krow < r0, Gr - Gh, NEG))
                strip = strip + _dot_nt(qi.astype(BF16), kt.astype(BF16))
            strips.append(strip)
        A = jnp.concatenate(strips, axis=0) if len(strips) > 1 else strips[0]
        o = _dot(A.astype(BF16), vb) + _dot_nt((qh * jnp.exp(Gh)).astype(BF16), st.astype(BF16))
        g_last = Gh[C - 1:C]
        kdec = kh * jnp.exp(g_last - Gh)
        st_scr[sl, :] = jnp.exp(g_last) * st + _dot_tn(vb, kdec.astype(BF16))
        on = o * lax.rsqrt(jnp.mean(o * o, axis=-1, keepdims=True) + EPS) * gn_ref[...]
        outs.append(on * _silu(ga[:, sl]))
    y_ref[...] = jnp.concatenate(outs, axis=1).astype(BF16)

    @pl.when(ci == nchunks - 1)
    def _():
        sout_ref[0] = st_scr[...]


def _hgrn(a16, a32, lb, gn, s0t, B, L, C, sub, valid):
    nch = L // C
    row = lambda b, c: b * nch + c
    const = lambda b, c: (0, 0)
    blk = lambda col: pl.BlockSpec((C, HG_WIDTH), lambda b, c: (row(b, c), col))
    return pl.pallas_call(
        functools.partial(_hgrn_kernel, C=C, sub=sub, valid=valid, nchunks=nch),
        grid=(B, nch),
        in_specs=[
            blk(A16_HQ // HG_WIDTH), blk(A32_HF // HG_WIDTH), blk(A16_HI // HG_WIDTH), blk(A16_HG // HG_WIDTH),
            pl.BlockSpec((1, HG_WIDTH), const),
            pl.BlockSpec((1, HG_DIM), const),
            pl.BlockSpec((1, HG_WIDTH, HG_DIM), lambda b, c: (b, 0, 0)),
        ],
        out_specs=[
            pl.BlockSpec((C, HG_WIDTH), lambda b, c: (row(b, c), 0)),
            pl.BlockSpec((1, HG_WIDTH, HG_DIM), lambda b, c: (b, 0, 0)),
        ],
        out_shape=[jax.ShapeDtypeStruct((B * L, HG_WIDTH), BF16),
                   jax.ShapeDtypeStruct((B, HG_WIDTH, HG_DIM), F32)],
        scratch_shapes=[pltpu.VMEM((HG_WIDTH, HG_DIM), F32)],
        compiler_params=_cparams(("parallel", "arbitrary")),
        name="hgrn2",
    )(a16, a32, a16, a16, lb, gn, s0t)


def _ssd_kernel(xbc_ref, z_ref, dt_ref, pre_ref, cw_ref, cb_ref, dtb_ref, alog_ref, dsk_ref, sn_ref,
                h0_ref, y_ref, hout_ref, cout_ref, ubuf, hs, *, C, valid, nchunks):
    ci = pl.program_id(1)

    @pl.when(ci == 0)
    def _():
        ubuf[0:8, :] = pre_ref[0]
        hs[...] = h0_ref[0]

    ubuf[8:8 + C, :] = xbc_ref[...].astype(F32)
    acc = cb_ref[...] + ubuf[5:5 + C, :] * cw_ref[0:1, :]
    for j in range(1, SSM_CONV):
        acc = acc + ubuf[5 + j:5 + j + C, :] * cw_ref[j:j + 1, :]
    xc = _silu(acc)
    tail = ubuf[valid:valid + 8, :]
    cout_ref[0] = tail
    ubuf[0:8, :] = tail

    x = dt_ref[...] + dtb_ref[...]
    dt = jnp.maximum(x, 0.0) + jnp.log1p(jnp.exp(-jnp.abs(x)))
    if valid < C:
        rows = lax.broadcasted_iota(jnp.int32, (C, LANE), 0)
        dt = jnp.where(rows < valid, dt, 0.0)
    a = -jnp.exp(alog_ref[...])
    cum = _cumsum_rows(dt * a, _tril_bf16(C))
    cum_t = cum.T
    dt_t = dt.T

    ri = lax.broadcasted_iota(jnp.int32, (C, C), 0)
    cj = lax.broadcasted_iota(jnp.int32, (C, C), 1)
    tril = cj <= ri
    lane = lax.broadcasted_iota(jnp.int32, (C, LANE), 1)
    srow = lax.broadcasted_iota(jnp.int32, (LANE, LANE), 0)
    bm = [xc[:, SSM_WIDTH + g * SSM_STATE:SSM_WIDTH + (g + 1) * SSM_STATE].astype(BF16) for g in range(SSM_GROUPS)]
    off_c = SSM_WIDTH + SSM_GROUPS * SSM_STATE
    cm = [xc[:, off_c + g * SSM_STATE:off_c + (g + 1) * SSM_STATE].astype(BF16) for g in range(SSM_GROUPS)]
    cb = [_dot_nt(cm[g], bm[g]) for g in range(SSM_GROUPS)]

    ys = []
    for p in range(SSM_HEADS // 2):
        g = (2 * p) // (SSM_HEADS // SSM_GROUPS)
        xp = xc[:, p * LANE:(p + 1) * LANE]
        yp = None
        colw = []
        for k, hd in enumerate((2 * p, 2 * p + 1)):
            colb = jnp.broadcast_to(cum[:, hd:hd + 1], (C, LANE))
            rowb = jnp.broadcast_to(cum_t[hd:hd + 1, :], (C, C))
            dtr = jnp.broadcast_to(dt_t[hd:hd + 1, :], (C, C))
            w = cb[g] * jnp.exp(jnp.where(tril, colb[:, :C] - rowb, NEG)) * dtr
            half = (lane < SSM_HEAD_DIM) if k == 0 else (lane >= SSM_HEAD_DIM)
            t = _dot(w.astype(BF16), jnp.where(half, xp, 0.0).astype(BF16))
            yp = t if yp is None else yp + t
            colw.append((colb, jnp.broadcast_to(dt[:, hd:hd + 1], (C, LANE))))
        first = lane < SSM_HEAD_DIM
        cum_e = jnp.where(first, colw[0][0], colw[1][0])
        dt_e = jnp.where(first, colw[0][1], colw[1][1])
        hp = hs[p * LANE:(p + 1) * LANE, :]
        yp = yp + jnp.exp(cum_e) * _dot_nt(cm[g], hp.astype(BF16))
        c_last = cum_e[C - 1:C, :]
        w_in = jnp.exp(c_last - cum_e) * dt_e
        dec = jnp.where(srow < SSM_HEAD_DIM,
                        jnp.broadcast_to(cum[C - 1:C, 2 * p:2 * p + 1], (LANE, LANE)),
                        jnp.broadcast_to(cum[C - 1:C, 2 * p + 1:2 * p + 2], (LANE, LANE)))
        hs[p * LANE:(p + 1) * LANE, :] = jnp.exp(dec) * hp + _dot_tn((xp * w_in).astype(BF16), bm[g])
        ys.append(yp)
    y = jnp.concatenate(ys, axis=1) + dsk_ref[...] * xc[:, :SSM_WIDTH]
    yz = y * _silu(z_ref[...].astype(F32))
    y_ref[...] = (yz * lax.rsqrt(jnp.mean(yz * yz, axis=-1, keepdims=True) + EPS) * sn_ref[...]).astype(BF16)

    @pl.when(ci == nchunks - 1)
    def _():
        hout_ref[0] = hs[...]


def _ssd(a16, a32, prefix, cw, cb, dtb, alog, dsk, sn, h0, B, L, C, valid):
    nch = L // C
    row = lambda b, c: b * nch + c
    const = lambda b, c: (0, 0)
    bat = lambda b, c: (b, 0, 0)
    return pl.pallas_call(
        functools.partial(_ssd_kernel, C=C, valid=valid, nchunks=nch),
        grid=(B, nch),
        in_specs=[
            pl.BlockSpec((C, SSM_CONV_DIM), lambda b, c: (row(b, c), A16_XBC // SSM_CONV_DIM)),
            pl.BlockSpec((C, SSM_WIDTH), lambda b, c: (row(b, c), A16_Z // SSM_WIDTH)),
            pl.BlockSpec((C, LANE), lambda b, c: (row(b, c), A32_DT // LANE)),
            pl.BlockSpec((1, 8, SSM_CONV_DIM), bat),
            pl.BlockSpec((8, SSM_CONV_DIM), const),
            pl.BlockSpec((1, SSM_CONV_DIM), const),
            pl.BlockSpec((1, LANE), const),
            pl.BlockSpec((1, LANE), const),
            pl.BlockSpec((1, SSM_WIDTH), const),
            pl.BlockSpec((1, SSM_WIDTH), const),
            pl.BlockSpec((1, SSM_WIDTH, SSM_STATE), bat),
        ],
        out_specs=[
            pl.BlockSpec((C, SSM_WIDTH), lambda b, c: (row(b, c), 0)),
            pl.BlockSpec((1, SSM_WIDTH, SSM_STATE), bat),
            pl.BlockSpec((1, 8, SSM_CONV_DIM), bat),
        ],
        out_shape=[jax.ShapeDtypeStruct((B * L, SSM_WIDTH), BF16),
                   jax.ShapeDtypeStruct((B, SSM_WIDTH, SSM_STATE), F32),
                   jax.ShapeDtypeStruct((B, 8, SSM_CONV_DIM), F32)],
        scratch_shapes=[pltpu.VMEM((C + 8, SSM_CONV_DIM), F32), pltpu.VMEM((SSM_WIDTH, SSM_STATE), F32)],
        compiler_params=_cparams(("parallel", "arbitrary")),
        name="ssd",
    )(a16, a16, a32, prefix, cw, cb, dtb, alog, dsk, sn, h0)


def _head_norm(x, gain, bd):
    ss = x * x
    hi = ss.astype(BF16)
    lo = (ss - hi.astype(F32)).astype(BF16)
    ms = _dot(hi, bd) + _dot(lo, bd)
    return x * lax.rsqrt(ms + EPS) * gain


def _head_masks(rows):
    lane = lax.broadcasted_iota(jnp.int32, (rows, ATT_WIDTH), 1)
    return [(lane >= h * ATT_HEAD_DIM) & (lane < (h + 1) * ATT_HEAD_DIM) for h in range(ATT_HEADS)]


def _attend(qb, kb, vb, valid, masks):
    acc = mx = den = None
    for h in range(ATT_HEADS):
        s = _dot_nt(jnp.where(masks[h], qb, 0.0).astype(BF16), kb)
        s = jnp.where(valid, s, NEG)
        m = jnp.max(s, axis=-1, keepdims=True)
        p = jnp.exp(s - m)
        d = jnp.sum(p, axis=-1, keepdims=True)
        pv = _dot(p.astype(BF16), vb)
        if h == 0:
            acc, mx, den = pv, jnp.broadcast_to(m, pv.shape), jnp.broadcast_to(d, pv.shape)
        else:
            acc = jnp.where(masks[h], pv, acc)
            mx = jnp.where(masks[h], m, mx)
            den = jnp.where(masks[h], d, den)
    return acc, mx, den


def _ld2(pair, s):
    return jnp.concatenate([pair[0][s, :], pair[1][s, :]], axis=1)


def _st2(pair, s, val):
    pair[0][s, :] = val[:, :LANE]
    pair[1][s, :] = val[:, LANE:]


def _attn_prompt_kernel(att_ref, gq_ref, gk_ref, bd_ref, y_ref, kv0_ref, kv1_ref, kv2_ref,
                        *scr, L):
    qn, kn, vv, ya, ma, za = (scr[2 * j:2 * j + 2] for j in range(6))
    g = pl.program_id(1)
    RB = 256
    QB = ATT_KEYS

    def group(gi, dil, keep, kv_ref):
        M = L // dil
        nb = M // QB
        KW = min(2 * QB, M)
        bd = bd_ref[...]

        def norm_body(i, carry):
            r0 = pl.multiple_of(i * RB, RB)
            x = att_ref[pl.ds(r0, RB), :].astype(F32)
            _st2(qn, pl.ds(r0, RB), _head_norm(x[:, :ATT_WIDTH], gq_ref[gi:gi + 1, :], bd) * ATT_SCALE)
            _st2(kn, pl.ds(r0, RB), _head_norm(x[:, ATT_WIDTH:2 * ATT_WIDTH], gk_ref[gi:gi + 1, :], bd))
            _st2(vv, pl.ds(r0, RB), x[:, 2 * ATT_WIDTH:])
            return carry

        lax.fori_loop(0, L // RB, norm_body, 0)
        kv_ref[0, :, 0:ATT_WIDTH] = _ld2(kn, slice(L - keep, L))
        kv_ref[0, :, ATT_WIDTH:2 * ATT_WIDTH] = _ld2(vv, slice(L - keep, L))

        masks = _head_masks(QB)
        iq = lax.broadcasted_iota(jnp.int32, (QB, KW), 0)
        ik = lax.broadcasted_iota(jnp.int32, (QB, KW), 1)

        def blk_body(it, carry):
            r = it // nb
            n = it - r * nb
            k0 = jnp.maximum(n - 1, 0) * QB
            if dil == 1:
                qs = pl.ds(pl.multiple_of(n * QB, QB), QB)
                ks = pl.ds(pl.multiple_of(k0, QB), KW)
            else:
                qs = pl.ds(n * QB * dil + r, QB, stride=dil)
                ks = pl.ds(k0 * dil + r, KW, stride=dil)
            dist = (n * QB + iq) - (k0 + ik)
            valid = (dist >= 0) & (dist <= ATT_KEYS)
            acc, mx, den = _attend(_ld2(qn, qs), _ld2(kn, ks).astype(BF16), _ld2(vv, ks).astype(BF16),
                                   valid, masks)
            if gi == 0:
                _st2(ya, qs, acc)
                _st2(ma, qs, mx)
                _st2(za, qs, den)
            else:
                mo = _ld2(ma, qs)
                mn = jnp.maximum(mo, mx)
                eo = jnp.exp(mo - mn)
                en = jnp.exp(mx - mn)
                _st2(ya, qs, _ld2(ya, qs) * eo + acc * en)
                _st2(za, qs, _ld2(za, qs) * eo + den * en)
                _st2(ma, qs, mn)
            return carry

        lax.fori_loop(0, dil * nb, blk_body, 0)

    for gi in range(3):
        @pl.when(g == gi)
        def _(gi=gi):
            group(gi, ATT_DILATIONS[gi], min(ATT_WINDOWS[gi], L), (kv0_ref, kv1_ref, kv2_ref)[gi])

    @pl.when(g == 2)
    def _():
        def out_body(i, carry):
            r0 = pl.multiple_of(i * RB, RB)
            y_ref[pl.ds(r0, RB), :] = (_ld2(ya, pl.ds(r0, RB)) / _ld2(za, pl.ds(r0, RB))).astype(BF16)
            return carry
        lax.fori_loop(0, L // RB, out_body, 0)


def _attn_prompt(a16, gq, gk, bd, B, L):
    keeps = [min(w, L) for w in ATT_WINDOWS]
    const = lambda b, g: (0, 0)
    bat = lambda b, g: (b, 0, 0)
    return pl.pallas_call(
        functools.partial(_attn_prompt_kernel, L=L),
        grid=(B, 3),
        in_specs=[
            pl.BlockSpec((L, 3 * ATT_WIDTH), lambda b, g: (b, A16_ATT // (3 * ATT_WIDTH) + g)),
            pl.BlockSpec((3, ATT_WIDTH), const),
            pl.BlockSpec((3, ATT_WIDTH), const),
            pl.BlockSpec((ATT_WIDTH, ATT_WIDTH), const),
        ],
        out_specs=[pl.BlockSpec((L, ATT_WIDTH), lambda b, g: (b, 0))]
        + [pl.BlockSpec((1, k, 2 * ATT_WIDTH), bat) for k in keeps],
        out_shape=[jax.ShapeDtypeStruct((B * L, ATT_WIDTH), BF16)]
        + [jax.ShapeDtypeStruct((B, k, 2 * ATT_WIDTH), F32) for k in keeps],
        scratch_shapes=[pltpu.VMEM((L, LANE), F32) for _ in range(12)],
        compiler_params=_cparams(("parallel", "arbitrary")),
        name="attn_prompt",
    )(a16, gq, gk, bd)


def _attn_sample_kernel(a0_ref, a1_ref, a2_ref, b0_ref, b1_ref, b2_ref, gq_ref, gk_ref, bd_ref,
                        y_ref, k0_ref, k1_ref, k2_ref):
    R = SAMPLE_ROWS
    bd = bd_ref[...]
    masks = _head_masks(R)
    ya = ma = za = None
    for gi, (a_ref, b_ref, k_ref) in enumerate(((a0_ref, b0_ref, k0_ref), (a1_ref, b1_ref, k1_ref),
                                                (a2_ref, b2_ref, k2_ref))):
        dil = ATT_DILATIONS[gi]
        wb = b_ref.shape[1]
        x = a_ref[...].astype(F32)
        q = _head_norm(x[:, :ATT_WIDTH], gq_ref[gi:gi + 1, :], bd) * ATT_SCALE
        k = _head_norm(x[:, ATT_WIDTH:2 * ATT_WIDTH], gk_ref[gi:gi + 1, :], bd)
        v = x[:, 2 * ATT_WIDTH:]
        k_ref[0] = jnp.concatenate([k[:8], v[:8]], axis=1)
        keys = jnp.concatenate([b_ref[0, :, 0:ATT_WIDTH], k], axis=0).astype(BF16)
        vals = jnp.concatenate([b_ref[0, :, ATT_WIDTH:2 * ATT_WIDTH], v], axis=0).astype(BF16)
        iq = lax.broadcasted_iota(jnp.int32, (R, wb + R), 0)
        ik = lax.broadcasted_iota(jnp.int32, (R, wb + R), 1)
        dist = wb + iq - ik
        valid = (dist >= 0) & (dist <= ATT_KEYS * dil) & ((dist & (dil - 1)) == 0)
        acc, mx, den = _attend(q, keys, vals, valid, masks)
        if gi == 0:
            ya, ma, za = acc, mx, den
        else:
            mn = jnp.maximum(ma, mx)
            eo = jnp.exp(ma - mn)
            en = jnp.exp(mx - mn)
            ya, za, ma = ya * eo + acc * en, za * eo + den * en, mn
    y_ref[...] = (ya / za).astype(BF16)


def _attn_sample(a16, bufs, gq, gk, bd, B):
    R = SAMPLE_ROWS
    const = lambda b: (0, 0)
    bat = lambda b: (b, 0, 0)
    base = A16_ATT // (3 * ATT_WIDTH)
    return pl.pallas_call(
        _attn_sample_kernel,
        grid=(B,),
        in_specs=[pl.BlockSpec((R, 3 * ATT_WIDTH), functools.partial(lambda b, j: (b, j), j=base + g))
                  for g in range(3)]
        + [pl.BlockSpec((1, buf.shape[1], 2 * ATT_WIDTH), bat) for buf in bufs]
        + [pl.BlockSpec((3, ATT_WIDTH), const), pl.BlockSpec((3, ATT_WIDTH), const),
           pl.BlockSpec((ATT_WIDTH, ATT_WIDTH), const)],
        out_specs=[pl.BlockSpec((R, ATT_WIDTH), lambda b: (b, 0))]
        + [pl.BlockSpec((1, 8, 2 * ATT_WIDTH), bat) for _ in range(3)],
        out_shape=[jax.ShapeDtypeStruct((B * R, ATT_WIDTH), BF16)]
        + [jax.ShapeDtypeStruct((B, 8, 2 * ATT_WIDTH), F32) for _ in range(3)],
        compiler_params=_cparams(("parallel",)),
        name="attn_sample",
    )(a16, a16, a16, *bufs, gq, gk, bd)


def _merge_kernel(yh_ref, ys_ref, ya_ref, gt_ref, x_ref, wh_ref, ws_ref, wa_ref, wo_ref, o_ref):
    gt = jax.nn.sigmoid(gt_ref[...].astype(F32))
    mixed = (gt[:, 0:D_MODEL] * _dot(yh_ref[...], wh_ref[...])
             + gt[:, D_MODEL:2 * D_MODEL] * _dot(ys_ref[...], ws_ref[...])
             + gt[:, 2 * D_MODEL:] * _dot(ya_ref[...], wa_ref[...]))
    o_ref[...] = x_ref[...] + _dot(mixed.astype(BF16), wo_ref[...])


def _merge(y_hg, y_ssm, y_att, a16, x, wh, ws, wa, wo, tm):
    T = x.shape[0]
    const = lambda i: (0, 0)
    rowb = lambda w: pl.BlockSpec((tm, w), lambda i: (i, 0))
    wsp = lambda w: pl.BlockSpec(w.shape, const, pipeline_mode=pl.Buffered(1))
    return pl.pallas_call(
        _merge_kernel,
        grid=(pl.cdiv(T, tm),),
        in_specs=[rowb(HG_WIDTH), rowb(SSM_WIDTH), rowb(ATT_WIDTH),
                  pl.BlockSpec((tm, 3 * D_MODEL), lambda i: (i, A16_GATE // (3 * D_MODEL))),
                  rowb(D_MODEL), wsp(wh), wsp(ws), wsp(wa), wsp(wo)],
        out_specs=rowb(D_MODEL),
        out_shape=jax.ShapeDtypeStruct((T, D_MODEL), F32),
        compiler_params=_cparams(("parallel",)),
        name="merge_out_proj",
    )(y_hg, y_ssm, y_att, a16, x, wh, ws, wa, wo)


def _ffn_kernel(x_ref, g_ref, wg_ref, wu_ref, wd_ref, o_ref):
    x = x_ref[...]
    h = (x * lax.rsqrt(jnp.mean(x * x, axis=-1, keepdims=True) + EPS) * g_ref[...]).astype(BF16)
    a = _silu(_dot(h, wg_ref[...])) * _dot(h, wu_ref[...])
    o_ref[...] = x + _dot(a.astype(BF16), wd_ref[...])


def _ffn(x, gain, wg, wu, wd, tm):
    T = x.shape[0]
    const = lambda i: (0, 0)
    wsp = lambda w: pl.BlockSpec(w.shape, const, pipeline_mode=pl.Buffered(1))
    return pl.pallas_call(
        _ffn_kernel,
        grid=(pl.cdiv(T, tm),),
        in_specs=[pl.BlockSpec((tm, D_MODEL), lambda i: (i, 0)), pl.BlockSpec((1, D_MODEL), const),
                  wsp(wg), wsp(wu), wsp(wd)],
        out_specs=pl.BlockSpec((tm, D_MODEL), lambda i: (i, 0)),
        out_shape=jax.ShapeDtypeStruct((T, D_MODEL), F32),
        compiler_params=_cparams(("parallel",)),
        name="ffn_dense",
    )(x, gain, wg, wu, wd)


def _router_kernel(x_ref, g_ref, wr_ref, h_ref, comb_ref):
    x = x_ref[...]
    h = x * lax.rsqrt(jnp.mean(x * x, axis=-1, keepdims=True) + EPS) * g_ref[...]
    hb = h.astype(BF16)
    h_ref[...] = hb
    h_lo = (h - hb.astype(F32)).astype(BF16)
    w = wr_ref[...]
    w_hi = w.astype(BF16)
    w_lo = (w - w_hi.astype(F32)).astype(BF16)
    logits = _dot(hb, w_hi) + _dot(hb, w_lo) + _dot(h_lo, w_hi)
    lane = lax.broadcasted_iota(jnp.int32, logits.shape, 1)
    logits = jnp.where(lane < N_EXPERTS, logits, NEG)
    v1 = jnp.max(logits, axis=-1, keepdims=True)
    i1 = jnp.min(jnp.where(logits == v1, lane, LANE), axis=-1, keepdims=True)
    rest = jnp.where(lane == i1, NEG, logits)
    v2 = jnp.max(rest, axis=-1, keepdims=True)
    i2 = jnp.min(jnp.where(rest == v2, lane, LANE), axis=-1, keepdims=True)
    e = jnp.exp(v2 - v1)
    w1 = 1.0 / (1.0 + e)
    w2 = e / (1.0 + e)
    comb_ref[...] = jnp.where(lane == i1, w1, 0.0) + jnp.where(lane == i2, w2, 0.0)


def _router(x, gain, wr, tm):
    T = x.shape[0]
    const = lambda i: (0, 0)
    return pl.pallas_call(
        _router_kernel,
        grid=(pl.cdiv(T, tm),),
        in_specs=[pl.BlockSpec((tm, D_MODEL), lambda i: (i, 0)), pl.BlockSpec((1, D_MODEL), const),
                  pl.BlockSpec((D_MODEL, LANE), const)],
        out_specs=[pl.BlockSpec((tm, D_MODEL), lambda i: (i, 0)), pl.BlockSpec((tm, LANE), lambda i: (i, 0))],
        out_shape=[jax.ShapeDtypeStruct((T, D_MODEL), BF16), jax.ShapeDtypeStruct((T, LANE), F32)],
        compiler_params=_cparams(("parallel",)),
        name="moe_router",
    )(x, gain, wr)


def _moe_kernel(x_ref, h_ref, comb_ref, wg_ref, wu_ref, wd_ref, o_ref, acc_ref, *, nf):
    e = pl.program_id(1)
    f = pl.program_id(2)

    @pl.when((e == 0) & (f == 0))
    def _():
        acc_ref[...] = jnp.zeros_like(acc_ref)

    h = h_ref[...]
    a = _silu(_dot(h, wg_ref[0])) * _dot(h, wu_ref[0])
    lane = lax.broadcasted_iota(jnp.int32, comb_ref.shape, 1)
    ce = jnp.sum(jnp.where(lane == e, comb_ref[...], 0.0), axis=-1, keepdims=True)
    acc_ref[...] += ce * _dot(a.astype(BF16), wd_ref[0])

    @pl.when((e == N_EXPERTS - 1) & (f == nf - 1))
    def _():
        o_ref[...] = x_ref[...] + acc_ref[...]


def _moe(x, h, comb, wg, wu, wd, tm, tf):
    T = x.shape[0]
    dff = wg.shape[2]
    nf = dff // tf
    rowb = lambda w: pl.BlockSpec((tm, w), lambda i, e, f: (i, 0))
    return pl.pallas_call(
        functools.partial(_moe_kernel, nf=nf),
        grid=(pl.cdiv(T, tm), N_EXPERTS, nf),
        in_specs=[rowb(D_MODEL), rowb(D_MODEL), rowb(LANE),
                  pl.BlockSpec((1, D_MODEL, tf), lambda i, e, f: (e, 0, f)),
                  pl.BlockSpec((1, D_MODEL, tf), lambda i, e, f: (e, 0, f)),
                  pl.BlockSpec((1, tf, D_MODEL), lambda i, e, f: (e, f, 0))],
        out_specs=rowb(D_MODEL),
        out_shape=jax.ShapeDtypeStruct((T, D_MODEL), F32),
        scratch_shapes=[pltpu.VMEM((tm, D_MODEL), F32)],
        compiler_params=_cparams(("parallel", "arbitrary", "arbitrary")),
        name="moe_dense",
    )(x, h, comb, wg, wu, wd)


def _prep_w_in(w):
    hq, hf, hi, hg = (w[:, j * HG_WIDTH:(j + 1) * HG_WIDTH] for j in range(4))
    z = w[:, _OFF_SSM:_OFF_SSM + SSM_WIDTH]
    xbc = w[:, _OFF_SSM + SSM_WIDTH:_OFF_SSM + SSM_WIDTH + SSM_CONV_DIM]
    dt = w[:, _OFF_ATT - SSM_HEADS:_OFF_ATT]
    att = w[:, _OFF_ATT:_OFF_GATE]
    gates = w[:, _OFF_GATE:]
    w16 = jnp.concatenate([gates, xbc, hq, hi, hg, z, att], axis=1).astype(BF16)
    w32 = jnp.concatenate([hf, dt, jnp.zeros((D_MODEL, LANE - SSM_HEADS), w.dtype)], axis=1).astype(BF16)
    return w16, w32


def _pad_lanes(v, n, value=0.0):
    v = v.reshape(1, -1).astype(F32)
    return jnp.pad(v, ((0, 0), (0, n - v.shape[1])), constant_values=value)


def _run_group(x, B, L, C_h, C_s, valid, tm, layer_params, states, kv_bufs):
    outs = []
    for l, P in enumerate(layer_params):
        st = states[l]
        a16, a32 = _norm_proj(x, P["mix_norm"], P["w16"], P["w32"], tm)
        y_hg, s_hg = _hgrn(a16, a32, P["lb"], P["hg_norm"], st["hgrn_t"], B, L, C_h, min(16, C_h), valid)
        y_ssm, s_ssm, s_conv = _ssd(a16, a32, st["conv"], P["conv_w"], P["conv_b"], P["dt_bias"], P["a_log"],
                                    P["d_skip"], P["ssm_norm"], st["ssm"], B, L, C_s, valid)
        if kv_bufs is None:
            y_att, kv0, kv1, kv2 = _attn_prompt(a16, P["gq"], P["gk"], P["bd"], B, L)
        else:
            y_att, kv0, kv1, kv2 = _attn_sample(a16, kv_bufs[l], P["gq"], P["gk"], P["bd"], B)
        x = _merge(y_hg, y_ssm, y_att, a16, x, P["w_out_hg"], P["w_out_ssm"], P["w_out_att"], P["w_o"], tm)
        if l % 2 == 0:
            F = P["ffn"]
            x = _ffn(x, P["ffn_norm"], F["wg"], F["wu"], F["wd"], min(tm, 256))
        else:
            F = P["moe"]
            h, comb = _router(x, P["ffn_norm"], F["wr"], tm)
            x = _moe(x, h, comb, F["wg"], F["wu"], F["wd"], tm, F["wg"].shape[2] // 2)
        outs.append((s_hg, s_ssm, s_conv, kv0, kv1, kv2))
    return x, outs


def kernel(x_prompt, x_sample, state_hgrn, state_ssm, state_conv, cache_kv_w128, cache_kv_w512, cache_kv_w2048, w_in, mix_norm, hgrn_lb_logits, hgrn_norm, ssm_conv_w, ssm_conv_b, ssm_dt_bias, ssm_a_log, ssm_d, ssm_norm, att_q_norm, att_k_norm, w_out_hgrn, w_out_ssm, w_out_att, w_o, ffn_norm, w_ffn_up, w_ffn_down, w_router, w_moe_up, w_moe_down):
    depth = w_in.shape[0]
    Bp, Lp, _ = x_prompt.shape
    Bs, Ls, _ = x_sample.shape
    R = SAMPLE_ROWS
    for buf, w in zip((cache_kv_w128, cache_kv_w512, cache_kv_w2048), ATT_WINDOWS):
        assert buf.shape[2] == w, "sample attention assumes full window buffers"

    lb_soft = jax.nn.softmax(hgrn_lb_logits.astype(F32), axis=0)
    lb_table = jnp.cumsum(lb_soft, axis=0) - lb_soft[0]
    bd = jnp.kron(jnp.eye(ATT_HEADS, dtype=F32), jnp.full((ATT_HEAD_DIM, ATT_HEAD_DIM), 1.0 / ATT_HEAD_DIM, F32)).astype(BF16)

    layer_params = []
    for l in range(depth):
        w16, w32 = _prep_w_in(w_in[l])
        P = dict(
            w16=w16, w32=w32,
            mix_norm=mix_norm[l].reshape(1, -1), ffn_norm=ffn_norm[l].reshape(1, -1),
            lb=lb_table[l].reshape(1, -1), hg_norm=hgrn_norm[l].reshape(1, -1),
            conv_w=jnp.pad(ssm_conv_w[l], ((0, 8 - SSM_CONV), (0, 0))), conv_b=ssm_conv_b[l].reshape(1, -1),
            dt_bias=_pad_lanes(ssm_dt_bias[l], LANE), a_log=_pad_lanes(ssm_a_log[l], LANE),
            d_skip=jnp.repeat(ssm_d[l].astype(F32), SSM_HEAD_DIM).reshape(1, -1),
            ssm_norm=ssm_norm[l].reshape(1, -1),
            gq=jnp.tile(att_q_norm[l], (1, ATT_HEADS)), gk=jnp.tile(att_k_norm[l], (1, ATT_HEADS)), bd=bd,
            w_out_hg=w_out_hgrn[l].astype(BF16), w_out_ssm=w_out_ssm[l].astype(BF16),
            w_out_att=w_out_att[l].astype(BF16), w_o=w_o[l].astype(BF16),
        )
        if l % 2 == 0:
            up = w_ffn_up[l // 2]
            dff = up.shape[1] // 2
            P["ffn"] = dict(wg=up[:, :dff].astype(BF16), wu=up[:, dff:].astype(BF16),
                            wd=w_ffn_down[l // 2].astype(BF16))
        else:
            up = w_moe_up[l // 2]
            dff = up.shape[2] // 2
            P["moe"] = dict(wr=jnp.pad(w_router[l // 2].astype(F32), ((0, 0), (0, LANE - N_EXPERTS))),
                            wg=up[:, :, :dff].astype(BF16), wu=up[:, :, dff:].astype(BF16),
                            wd=w_moe_down[l // 2].astype(BF16))
        layer_params.append(P)

    zero_p = dict(hgrn_t=jnp.zeros((Bp, HG_WIDTH, HG_DIM), F32), ssm=jnp.zeros((Bp, SSM_WIDTH, SSM_STATE), F32),
                  conv=jnp.zeros((Bp, 8, SSM_CONV_DIM), F32))
    xp, outs_p = _run_group(x_prompt.reshape(Bp * Lp, D_MODEL), Bp, Lp, 64, 64, 64, 512,
                            layer_params, [zero_p] * depth, None)

    xs = jnp.pad(x_sample, ((0, 0), (0, R - Ls), (0, 0))).reshape(Bs * R, D_MODEL)
    st_s, kv_s = [], []
    for l in range(depth):
        st_s.append(dict(
            hgrn_t=jnp.swapaxes(state_hgrn[l], -1, -2).reshape(Bs, HG_WIDTH, HG_DIM),
            ssm=state_ssm[l].reshape(Bs, SSM_WIDTH, SSM_STATE),
            conv=jnp.pad(state_conv[l], ((0, 0), (8 - (SSM_CONV - 1), 0), (0, 0)))))
        kv_s.append([c[l].reshape(Bs, c.shape[2], 2 * ATT_WIDTH)
                     for c in (cache_kv_w128, cache_kv_w512, cache_kv_w2048)])
    xs, outs_s = _run_group(xs, Bs, R, R, R, Ls, Bs * R, layer_params, st_s, kv_s)

    def pack(outs, B, nkv):
        s_hg = jnp.stack([jnp.swapaxes(o[0].reshape(B, HG_HEADS, HG_DIM, HG_DIM), -1, -2) for o in outs])
        s_ssm = jnp.stack([o[1].reshape(B, SSM_HEADS, SSM_HEAD_DIM, SSM_STATE) for o in outs])
        s_conv = jnp.stack([o[2][:, 8 - (SSM_CONV - 1):] for o in outs])
        kvs = []
        for j in range(3):
            n = nkv[j]
            kvs.append(jnp.stack([o[3 + j][:, :n].reshape(B, n, 2, ATT_HEADS, ATT_HEAD_DIM) for o in outs]))
        return (s_hg, s_ssm, s_conv, *kvs)

    keeps = [min(w, Lp) for w in ATT_WINDOWS]
    y_prompt = xp.reshape(Bp, Lp, D_MODEL)
    y_sample = xs.reshape(Bs, R, D_MODEL)[:, :Ls]
    return (y_prompt, y_sample, *pack(outs_p, Bp, keeps), *pack(outs_s, Bs, [Ls] * 3))
```

```python
import functools

import jax
import jax.numpy as jnp
from jax import lax
from jax.experimental import pallas as pl
from jax.experimental.pallas import tpu as pltpu

F32 = jnp.float32
BF16 = jnp.bfloat16

D_MODEL = 1024
HG_HEADS, HG_DIM = 4, 128
HG_WIDTH = HG_HEADS * HG_DIM
SSM_HEADS, SSM_HEAD_DIM, SSM_GROUPS, SSM_STATE, SSM_CONV = 8, 64, 2, 128, 4
SSM_WIDTH = SSM_HEADS * SSM_HEAD_DIM
SSM_CONV_DIM = SSM_WIDTH + 2 * SSM_GROUPS * SSM_STATE
ATT_WINDOWS = (128, 512, 2048)
ATT_DILATIONS = (1, 4, 16)
ATT_HEADS, ATT_HEAD_DIM = 4, 64
ATT_WIDTH = ATT_HEADS * ATT_HEAD_DIM
ATT_KEYS = 128
ATT_SCALE = ATT_HEAD_DIM ** -0.5
N_EXPERTS = 8
EPS = 1e-6
NEG = -1e30

_OFF_SSM = 4 * HG_WIDTH
_OFF_ATT = _OFF_SSM + SSM_WIDTH + SSM_CONV_DIM + SSM_HEADS
_OFF_GATE = _OFF_ATT + 3 * 3 * ATT_WIDTH

A16_GATE, A16_XBC, A16_HQ, A16_HI, A16_HG, A16_Z, A16_ATT = 0, 3072, 4096, 4608, 5120, 5632, 6144
W16 = 8448
A32_HF, A32_DT = 0, 512
W32 = 640

LANE = 128
SAMPLE_ROWS = 16
MOE_BLOCK = 1024
MOE_SLOTS = 320
VMEM_LIMIT = 56 * 2 ** 20


def _cparams(sem):
    return pltpu.CompilerParams(dimension_semantics=sem, vmem_limit_bytes=VMEM_LIMIT)


def _dot(a, b):
    return jnp.dot(a, b, preferred_element_type=F32)


def _dot_nt(a, b):
    return lax.dot_general(a, b, (((1,), (1,)), ((), ())), preferred_element_type=F32)


def _dot_tn(a, b):
    return lax.dot_general(a, b, (((0,), (0,)), ((), ())), preferred_element_type=F32)


def _split3(x):
    hi = x.astype(BF16)
    r = x - hi.astype(F32)
    mid = r.astype(BF16)
    lo = (r - mid.astype(F32)).astype(BF16)
    return hi, mid, lo


def _cumsum_rows(x, tril_bf16):
    hi, mid, lo = _split3(x)
    return _dot(tril_bf16, hi) + _dot(tril_bf16, mid) + _dot(tril_bf16, lo)


def _silu(x):
    return x * jax.nn.sigmoid(x)


def _tril_bf16(n):
    r = lax.broadcasted_iota(jnp.int32, (n, n), 0)
    c = lax.broadcasted_iota(jnp.int32, (n, n), 1)
    return jnp.where(c <= r, 1.0, 0.0).astype(BF16)


def _norm_proj_kernel(x_ref, g_ref, w16_ref, w32_ref, o16_ref, o32_ref, *, chunk):
    x = x_ref[...]
    h = (x * lax.rsqrt(jnp.mean(x * x, axis=-1, keepdims=True) + EPS) * g_ref[...]).astype(BF16)
    for c0 in range(0, W16, chunk):
        o16_ref[:, c0:c0 + chunk] = _dot(h, w16_ref[:, c0:c0 + chunk]).astype(BF16)
    o32_ref[...] = _dot(h, w32_ref[...])


def _norm_proj(x, gain, w16, w32, tm):
    T = x.shape[0]
    const = lambda i: (0, 0)
    return pl.pallas_call(
        functools.partial(_norm_proj_kernel, chunk=768),
        grid=(pl.cdiv(T, tm),),
        in_specs=[
            pl.BlockSpec((tm, D_MODEL), lambda i: (i, 0)),
            pl.BlockSpec((1, D_MODEL), const),
            pl.BlockSpec((D_MODEL, W16), const, pipeline_mode=pl.Buffered(1)),
            pl.BlockSpec((D_MODEL, W32), const, pipeline_mode=pl.Buffered(1)),
        ],
        out_specs=[
            pl.BlockSpec((tm, W16), lambda i: (i, 0)),
            pl.BlockSpec((tm, W32), lambda i: (i, 0)),
        ],
        out_shape=[jax.ShapeDtypeStruct((T, W16), BF16), jax.ShapeDtypeStruct((T, W32), F32)],
        compiler_params=_cparams(("parallel",)),
        name="norm_proj",
    )(x, gain, w16, w32)


def _hgrn_kernel(q_ref, f_ref, i_ref, g_ref, lb_ref, gn_ref, s0_ref, y_ref, sout_ref, st_scr,
                 *, C, sub, valid, nchunks):
    ci = pl.program_id(1)

    @pl.when(ci == 0)
    def _():
        st_scr[...] = s0_ref[0]

    lb = lb_ref[...]
    fgate = lb + (1.0 - lb) * jax.nn.sigmoid(f_ref[...])
    logf = jnp.log(fgate)
    kk = 1.0 - fgate
    if valid < C:
        rows = lax.broadcasted_iota(jnp.int32, (C, HG_WIDTH), 0)
        logf = jnp.where(rows < valid, logf, 0.0)
        kk = jnp.where(rows < valid, kk, 0.0)
    qa = _silu(q_ref[...].astype(F32))
    va = i_ref[...].astype(F32)
    ga = g_ref[...].astype(F32)
    G = _cumsum_rows(logf, _tril_bf16(C))

    lane_c = lax.broadcasted_iota(jnp.int32, (sub, C), 1)
    row_c = lax.broadcasted_iota(jnp.int32, (sub, C), 0)
    krow = lax.broadcasted_iota(jnp.int32, (C, HG_DIM), 0)
    outs = []
    for h in range(HG_HEADS):
        sl = slice(h * HG_DIM, (h + 1) * HG_DIM)
        Gh, qh, kh, vh = G[:, sl], qa[:, sl], kk[:, sl], va[:, sl]
        vb = vh.astype(BF16)
        st = st_scr[sl, :]
        strips = []
        for i in range(C // sub):
            r0 = i * sub
            Gb, qb, kb = Gh[r0:r0 + sub], qh[r0:r0 + sub], kh[r0:r0 + sub]
            P = jnp.zeros((sub, C), F32)
            for s in range(sub):
                e = jnp.exp(jnp.minimum(Gb - Gb[s:s + 1], 0.0))
                col = jnp.sum(qb * (kb[s:s + 1] * e), axis=-1, keepdims=True)
                P = jnp.where(lane_c == r0 + s, col, P)
            strip = jnp.where(lane_c <= row_c + r0, P, 0.0)
            if i > 0:
                Gr = Gh[r0 - 1:r0]
                qi = qb * jnp.exp(Gb - Gr)
                kt = kh * jnp.exp(jnp.where(krow < r0, Gr - Gh, NEG))
                strip = strip + _dot_nt(qi.astype(BF16), kt.astype(BF16))
            strips.append(strip)
        A = jnp.concatenate(strips, axis=0) if len(strips) > 1 else strips[0]
        o = _dot(A.astype(BF16), vb) + _dot_nt((qh * jnp.exp(Gh)).astype(BF16), st.astype(BF16))
        g_last = Gh[C - 1:C]
        kdec = kh * jnp.exp(g_last - Gh)
        st_scr[sl, :] = jnp.exp(g_last) * st + _dot_tn(vb, kdec.astype(BF16))
        on = o * lax.rsqrt(jnp.mean(o * o, axis=-1, keepdims=True) + EPS) * gn_ref[...]
        outs.append(on * _silu(ga[:, sl]))
    y_ref[...] = jnp.concatenate(outs, axis=1).astype(BF16)

    @pl.when(ci == nchunks - 1)
    def _():
        sout_ref[0] = st_scr[...]


def _hgrn(a16, a32, lb, gn, s0t, B, L, C, sub, valid):
    nch = L // C
    row = lambda b, c: b * nch + c
    const = lambda b, c: (0, 0)
    blk = lambda col: pl.BlockSpec((C, HG_WIDTH), lambda b, c: (row(b, c), col))
    return pl.pallas_call(
        functools.partial(_hgrn_kernel, C=C, sub=sub, valid=valid, nchunks=nch),
        grid=(B, nch),
        in_specs=[
            blk(A16_HQ // HG_WIDTH), blk(A32_HF // HG_WIDTH), blk(A16_HI // HG_WIDTH), blk(A16_HG // HG_WIDTH),
            pl.BlockSpec((1, HG_WIDTH), const),
            pl.BlockSpec((1, HG_DIM), const),
            pl.BlockSpec((1, HG_WIDTH, HG_DIM), lambda b, c: (b, 0, 0)),
        ],
        out_specs=[
            pl.BlockSpec((C, HG_WIDTH), lambda b, c: (row(b, c), 0)),
            pl.BlockSpec((1, HG_WIDTH, HG_DIM), lambda b, c: (b, 0, 0)),
        ],
        out_shape=[jax.ShapeDtypeStruct((B * L, HG_WIDTH), BF16),
                   jax.ShapeDtypeStruct((B, HG_WIDTH, HG_DIM), F32)],
        scratch_shapes=[pltpu.VMEM((HG_WIDTH, HG_DIM), F32)],
        compiler_params=_cparams(("parallel", "arbitrary")),
        name="hgrn2",
    )(a16, a32, a16, a16, lb, gn, s0t)


def _ssd_kernel(xbc_ref, z_ref, dt_ref, pre_ref, cw_ref, cb_ref, dtb_ref, alog_ref, dsk_ref, sn_ref,
                h0_ref, y_ref, hout_ref, cout_ref, ubuf, hs, *, C, valid, nchunks):
    ci = pl.program_id(1)

    @pl.when(ci == 0)
    def _():
        ubuf[0:8, :] = pre_ref[0]
        hs[...] = h0_ref[0]

    ubuf[8:8 + C, :] = xbc_ref[...].astype(F32)
    acc = cb_ref[...] + ubuf[5:5 + C, :] * cw_ref[0:1, :]
    for j in range(1, SSM_CONV):
        acc = acc + ubuf[5 + j:5 + j + C, :] * cw_ref[j:j + 1, :]
    xc = _silu(acc)
    tail = ubuf[valid:valid + 8, :]
    cout_ref[0] = tail
    ubuf[0:8, :] = tail

    x = dt_ref[...] + dtb_ref[...]
    dt = jnp.maximum(x, 0.0) + jnp.log1p(jnp.exp(-jnp.abs(x)))
    if valid < C:
        rows = lax.broadcasted_iota(jnp.int32, (C, LANE), 0)
        dt = jnp.where(rows < valid, dt, 0.0)
    a = -jnp.exp(alog_ref[...])
    cum = _cumsum_rows(dt * a, _tril_bf16(C))
    cum_t = cum.T
    dt_t = dt.T

    ri = lax.broadcasted_iota(jnp.int32, (C, C), 0)
    cj = lax.broadcasted_iota(jnp.int32, (C, C), 1)
    tril = cj <= ri
    lane = lax.broadcasted_iota(jnp.int32, (C, LANE), 1)
    srow = lax.broadcasted_iota(jnp.int32, (LANE, LANE), 0)
    bm = [xc[:, SSM_WIDTH + g * SSM_STATE:SSM_WIDTH + (g + 1) * SSM_STATE].astype(BF16) for g in range(SSM_GROUPS)]
    off_c = SSM_WIDTH + SSM_GROUPS * SSM_STATE
    cm = [xc[:, off_c + g * SSM_STATE:off_c + (g + 1) * SSM_STATE].astype(BF16) for g in range(SSM_GROUPS)]
    cb = [_dot_nt(cm[g], bm[g]) for g in range(SSM_GROUPS)]

    ys = []
    for p in range(SSM_HEADS // 2):
        g = (2 * p) // (SSM_HEADS // SSM_GROUPS)
        xp = xc[:, p * LANE:(p + 1) * LANE]
        yp = None
        colw = []
        for k, hd in enumerate((2 * p, 2 * p + 1)):
            colb = jnp.broadcast_to(cum[:, hd:hd + 1], (C, LANE))
            rowb = jnp.broadcast_to(cum_t[hd:hd + 1, :], (C, C))
            dtr = jnp.broadcast_to(dt_t[hd:hd + 1, :], (C, C))
            w = cb[g] * jnp.exp(jnp.where(tril, colb[:, :C] - rowb, NEG)) * dtr
            half = (lane < SSM_HEAD_DIM) if k == 0 else (lane >= SSM_HEAD_DIM)
            t = _dot(w.astype(BF16), jnp.where(half, xp, 0.0).astype(BF16))
            yp = t if yp is None else yp + t
            colw.append((colb, jnp.broadcast_to(dt[:, hd:hd + 1], (C, LANE))))
        first = lane < SSM_HEAD_DIM
        cum_e = jnp.where(first, colw[0][0], colw[1][0])
        dt_e = jnp.where(first, colw[0][1], colw[1][1])
        hp = hs[p * LANE:(p + 1) * LANE, :]
        yp = yp + jnp.exp(cum_e) * _dot_nt(cm[g], hp.astype(BF16))
        c_last = cum_e[C - 1:C, :]
        w_in = jnp.exp(c_last - cum_e) * dt_e
        dec = jnp.where(srow < SSM_HEAD_DIM,
                        jnp.broadcast_to(cum[C - 1:C, 2 * p:2 * p + 1], (LANE, LANE)),
                        jnp.broadcast_to(cum[C - 1:C, 2 * p + 1:2 * p + 2], (LANE, LANE)))
        hs[p * LANE:(p + 1) * LANE, :] = jnp.exp(dec) * hp + _dot_tn((xp * w_in).astype(BF16), bm[g])
        ys.append(yp)
    y = jnp.concatenate(ys, axis=1) + dsk_ref[...] * xc[:, :SSM_WIDTH]
    yz = y * _silu(z_ref[...].astype(F32))
    y_ref[...] = (yz * lax.rsqrt(jnp.mean(yz * yz, axis=-1, keepdims=True) + EPS) * sn_ref[...]).astype(BF16)

    @pl.when(ci == nchunks - 1)
    def _():
        hout_ref[0] = hs[...]


def _ssd(a16, a32, prefix, cw, cb, dtb, alog, dsk, sn, h0, B, L, C, valid):
    nch = L // C
    row = lambda b, c: b * nch + c
    const = lambda b, c: (0, 0)
    bat = lambda b, c: (b, 0, 0)
    return pl.pallas_call(
        functools.partial(_ssd_kernel, C=C, valid=valid, nchunks=nch),
        grid=(B, nch),
        in_specs=[
            pl.BlockSpec((C, SSM_CONV_DIM), lambda b, c: (row(b, c), A16_XBC // SSM_CONV_DIM)),
            pl.BlockSpec((C, SSM_WIDTH), lambda b, c: (row(b, c), A16_Z // SSM_WIDTH)),
            pl.BlockSpec((C, LANE), lambda b, c: (row(b, c), A32_DT // LANE)),
            pl.BlockSpec((1, 8, SSM_CONV_DIM), bat),
            pl.BlockSpec((8, SSM_CONV_DIM), const),
            pl.BlockSpec((1, SSM_CONV_DIM), const),
            pl.BlockSpec((1, LANE), const),
            pl.BlockSpec((1, LANE), const),
            pl.BlockSpec((1, SSM_WIDTH), const),
            pl.BlockSpec((1, SSM_WIDTH), const),
            pl.BlockSpec((1, SSM_WIDTH, SSM_STATE), bat),
        ],
        out_specs=[
            pl.BlockSpec((C, SSM_WIDTH), lambda b, c: (row(b, c), 0)),
            pl.BlockSpec((1, SSM_WIDTH, SSM_STATE), bat),
            pl.BlockSpec((1, 8, SSM_CONV_DIM), bat),
        ],
        out_shape=[jax.ShapeDtypeStruct((B * L, SSM_WIDTH), BF16),
                   jax.ShapeDtypeStruct((B, SSM_WIDTH, SSM_STATE), F32),
                   jax.ShapeDtypeStruct((B, 8, SSM_CONV_DIM), F32)],
        scratch_shapes=[pltpu.VMEM((C + 8, SSM_CONV_DIM), F32), pltpu.VMEM((SSM_WIDTH, SSM_STATE), F32)],
        compiler_params=_cparams(("parallel", "arbitrary")),
        name="ssd",
    )(a16, a16, a32, prefix, cw, cb, dtb, alog, dsk, sn, h0)


def _head_norm(x, gain, bd):
    ss = x * x
    hi = ss.astype(BF16)
    lo = (ss - hi.astype(F32)).astype(BF16)
    ms = _dot(hi, bd) + _dot(lo, bd)
    return x * lax.rsqrt(ms + EPS) * gain


def _head_masks(rows):
    lane = lax.broadcasted_iota(jnp.int32, (rows, ATT_WIDTH), 1)
    return [(lane >= h * ATT_HEAD_DIM) & (lane < (h + 1) * ATT_HEAD_DIM) for h in range(ATT_HEADS)]


def _stack_heads(q, masks):
    return jnp.concatenate([jnp.where(m, q, 0.0) for m in masks], axis=0).astype(BF16)


def _unstack_heads(pv, m, d, masks):
    R = pv.shape[0] // ATT_HEADS
    acc, mx, den = pv[0:R], jnp.broadcast_to(m[0:R], (R, ATT_WIDTH)), jnp.broadcast_to(d[0:R], (R, ATT_WIDTH))
    for h in range(1, ATT_HEADS):
        sl = slice(h * R, (h + 1) * R)
        acc = jnp.where(masks[h], pv[sl], acc)
        mx = jnp.where(masks[h], m[sl], mx)
        den = jnp.where(masks[h], d[sl], den)
    return acc, mx, den


def _attend(qb, kb, vb, valid4, masks):
    s = jnp.where(valid4, _dot_nt(_stack_heads(qb, masks), kb), NEG)
    m = jnp.max(s, axis=-1, keepdims=True)
    p = jnp.exp(s - m)
    d = jnp.sum(p, axis=-1, keepdims=True)
    return _unstack_heads(_dot(p.astype(BF16), vb), m, d, masks)


def _ld2(pair, s):
    return jnp.concatenate([pair[0][s, :], pair[1][s, :]], axis=1)


def _st2(pair, s, val):
    pair[0][s, :] = val[:, :LANE]
    pair[1][s, :] = val[:, LANE:]


def _attn_prompt_kernel(att_ref, gq_ref, gk_ref, bd_ref, y_ref, kv0_ref, kv1_ref, kv2_ref,
                        *scr, L):
    qn, kn, vv, ya, ma, za = (scr[2 * j:2 * j + 2] for j in range(6))
    g = pl.program_id(1)
    RB = 256
    QB = ATT_KEYS

    def group(gi, dil, keep, kv_ref):
        M = L // dil
        nb = M // QB
        KW = min(2 * QB, M)
        bd = bd_ref[...]

        def norm_body(i, carry):
            r0 = pl.multiple_of(i * RB, RB)
            x = att_ref[pl.ds(r0, RB), :].astype(F32)
            _st2(qn, pl.ds(r0, RB), _head_norm(x[:, :ATT_WIDTH], gq_ref[gi:gi + 1, :], bd) * ATT_SCALE)
            _st2(kn, pl.ds(r0, RB), _head_norm(x[:, ATT_WIDTH:2 * ATT_WIDTH], gk_ref[gi:gi + 1, :], bd))
            _st2(vv, pl.ds(r0, RB), x[:, 2 * ATT_WIDTH:])
            return carry

        lax.fori_loop(0, L // RB, norm_body, 0)
        kv_ref[0, :, 0:ATT_WIDTH] = _ld2(kn, slice(L - keep, L))
        kv_ref[0, :, ATT_WIDTH:2 * ATT_WIDTH] = _ld2(vv, slice(L - keep, L))

        masks = _head_masks(QB)
        iq = lax.broadcasted_iota(jnp.int32, (ATT_HEADS * QB, KW), 0) & (QB - 1)
        ik = lax.broadcasted_iota(jnp.int32, (ATT_HEADS * QB, KW), 1)
        UNROLL = 2

        def blk_body(j, carry):
            loaded = []
            for u in range(UNROLL):
                it = j * UNROLL + u
                r = it // nb
                n = it - r * nb
                k0 = jnp.maximum(n - 1, 0) * QB
                if dil == 1:
                    qs = pl.ds(pl.multiple_of(n * QB, QB), QB)
                    ks = pl.ds(pl.multiple_of(k0, QB), KW)
                else:
                    qs = pl.ds(n * QB * dil + r, QB, stride=dil)
                    ks = pl.ds(k0 * dil + r, KW, stride=dil)
                dist = (n * QB + iq) - (k0 + ik)
                old = None if gi == 0 else (_ld2(ya, qs), _ld2(ma, qs), _ld2(za, qs))
                loaded.append((qs, _ld2(qn, qs), _ld2(kn, ks).astype(BF16), _ld2(vv, ks).astype(BF16),
                               (dist >= 0) & (dist <= ATT_KEYS), old))
            for qs, qb, kb, vb, valid4, old in loaded:
                acc, mx, den = _attend(qb, kb, vb, valid4, masks)
                if gi > 0:
                    yo, mo, zo = old
                    mn = jnp.maximum(mo, mx)
                    eo = jnp.exp(mo - mn)
                    en = jnp.exp(mx - mn)
                    acc, mx, den = yo * eo + acc * en, mn, zo * eo + den * en
                _st2(ya, qs, acc)
                _st2(ma, qs, mx)
                _st2(za, qs, den)
            return carry

        lax.fori_loop(0, dil * nb // UNROLL, blk_body, 0)

    for gi in range(3):
        @pl.when(g == gi)
        def _(gi=gi):
            group(gi, ATT_DILATIONS[gi], min(ATT_WINDOWS[gi], L), (kv0_ref, kv1_ref, kv2_ref)[gi])

    @pl.when(g == 2)
    def _():
        def out_body(i, carry):
            r0 = pl.multiple_of(i * RB, RB)
            y_ref[pl.ds(r0, RB), :] = (_ld2(ya, pl.ds(r0, RB)) / _ld2(za, pl.ds(r0, RB))).astype(BF16)
            return carry
        lax.fori_loop(0, L // RB, out_body, 0)


def _attn_prompt(a16, gq, gk, bd, B, L):
    keeps = [min(w, L) for w in ATT_WINDOWS]
    const = lambda b, g: (0, 0)
    bat = lambda b, g: (b, 0, 0)
    return pl.pallas_call(
        functools.partial(_attn_prompt_kernel, L=L),
        grid=(B, 3),
        in_specs=[
            pl.BlockSpec((L, 3 * ATT_WIDTH), lambda b, g: (b, A16_ATT // (3 * ATT_WIDTH) + g)),
            pl.BlockSpec((3, ATT_WIDTH), const),
            pl.BlockSpec((3, ATT_WIDTH), const),
            pl.BlockSpec((ATT_WIDTH, ATT_WIDTH), const),
        ],
        out_specs=[pl.BlockSpec((L, ATT_WIDTH), lambda b, g: (b, 0))]
        + [pl.BlockSpec((1, k, 2 * ATT_WIDTH), bat) for k in keeps],
        out_shape=[jax.ShapeDtypeStruct((B * L, ATT_WIDTH), BF16)]
        + [jax.ShapeDtypeStruct((B, k, 2 * ATT_WIDTH), F32) for k in keeps],
        scratch_shapes=[pltpu.VMEM((L, LANE), F32) for _ in range(12)],
        compiler_params=_cparams(("parallel", "arbitrary")),
        name="attn_prompt",
    )(a16, gq, gk, bd)


def _attn_sample_kernel(a0_ref, a1_ref, a2_ref, b0_ref, b1_ref, b2_ref, gq_ref, gk_ref, bd_ref,
                        y_ref, k0_ref, k1_ref, k2_ref):
    R = SAMPLE_ROWS
    bd = bd_ref[...]
    masks = _head_masks(R)
    ya = ma = za = None
    for gi, (a_ref, b_ref, k_ref) in enumerate(((a0_ref, b0_ref, k0_ref), (a1_ref, b1_ref, k1_ref),
                                                (a2_ref, b2_ref, k2_ref))):
        dil = ATT_DILATIONS[gi]
        wb = b_ref.shape[3]
        x = a_ref[...].astype(F32)
        q = _head_norm(x[:, :ATT_WIDTH], gq_ref[gi:gi + 1, :], bd) * ATT_SCALE
        k = _head_norm(x[:, ATT_WIDTH:2 * ATT_WIDTH], gk_ref[gi:gi + 1, :], bd)
        v = x[:, 2 * ATT_WIDTH:]
        k_ref[0] = jnp.concatenate([k[:8], v[:8]], axis=1)
        k_t = b_ref[0, 0, 0:ATT_WIDTH, :].astype(BF16)
        v_t = b_ref[0, 0, ATT_WIDTH:2 * ATT_WIDTH, :].astype(BF16)
        kb, vb = k.astype(BF16), v.astype(BF16)

        def ok(dist):
            return (dist >= 0) & (dist <= ATT_KEYS * dil) & ((dist & (dil - 1)) == 0)

        R4 = ATT_HEADS * R
        row_b = lax.broadcasted_iota(jnp.int32, (R4, wb), 0) & (R - 1)
        row_n = lax.broadcasted_iota(jnp.int32, (R4, R), 0) & (R - 1)
        ok_b = ok(wb + row_b - lax.broadcasted_iota(jnp.int32, (R4, wb), 1))
        ok_n = ok(row_n - lax.broadcasted_iota(jnp.int32, (R4, R), 1))
        qs = _stack_heads(q, masks)
        s_b = jnp.where(ok_b, _dot(qs, k_t), NEG)
        s_n = jnp.where(ok_n, _dot_nt(qs, kb), NEG)
        m = jnp.maximum(jnp.max(s_b, axis=-1, keepdims=True), jnp.max(s_n, axis=-1, keepdims=True))
        p_b = jnp.exp(s_b - m)
        p_n = jnp.exp(s_n - m)
        d = jnp.sum(p_b, axis=-1, keepdims=True) + jnp.sum(p_n, axis=-1, keepdims=True)
        pv = _dot_nt(p_b.astype(BF16), v_t) + _dot(p_n.astype(BF16), vb)
        acc, mx, den = _unstack_heads(pv, m, d, masks)
        if gi == 0:
            ya, ma, za = acc, mx, den
        else:
            mn = jnp.maximum(ma, mx)
            eo = jnp.exp(ma - mn)
            en = jnp.exp(mx - mn)
            ya, za, ma = ya * eo + acc * en, za * eo + den * en, mn
    y_ref[...] = (ya / za).astype(BF16)


def _attn_sample(a16, bufs, layer, gq, gk, bd, B):
    R = SAMPLE_ROWS
    const = lambda b: (0, 0)
    bat = lambda b: (b, 0, 0)
    base = A16_ATT // (3 * ATT_WIDTH)
    return pl.pallas_call(
        _attn_sample_kernel,
        grid=(B,),
        in_specs=[pl.BlockSpec((R, 3 * ATT_WIDTH), functools.partial(lambda b, j: (b, j), j=base + g))
                  for g in range(3)]
        + [pl.BlockSpec((1, 1, 2 * ATT_WIDTH, buf.shape[3]), lambda b: (layer, b, 0, 0)) for buf in bufs]
        + [pl.BlockSpec((3, ATT_WIDTH), const), pl.BlockSpec((3, ATT_WIDTH), const),
           pl.BlockSpec((ATT_WIDTH, ATT_WIDTH), const)],
        out_specs=[pl.BlockSpec((R, ATT_WIDTH), lambda b: (b, 0))]
        + [pl.BlockSpec((1, 8, 2 * ATT_WIDTH), bat) for _ in range(3)],
        out_shape=[jax.ShapeDtypeStruct((B * R, ATT_WIDTH), BF16)]
        + [jax.ShapeDtypeStruct((B, 8, 2 * ATT_WIDTH), F32) for _ in range(3)],
        compiler_params=_cparams(("parallel",)),
        name="attn_sample",
    )(a16, a16, a16, *bufs, gq, gk, bd)


def _merge_kernel(yh_ref, ys_ref, ya_ref, gt_ref, x_ref, wh_ref, ws_ref, wa_ref, wo_ref, o_ref):
    gt = jax.nn.sigmoid(gt_ref[...].astype(F32))
    mixed = (gt[:, 0:D_MODEL] * _dot(yh_ref[...], wh_ref[...])
             + gt[:, D_MODEL:2 * D_MODEL] * _dot(ys_ref[...], ws_ref[...])
             + gt[:, 2 * D_MODEL:] * _dot(ya_ref[...], wa_ref[...]))
    o_ref[...] = x_ref[...] + _dot(mixed.astype(BF16), wo_ref[...])


def _merge(y_hg, y_ssm, y_att, a16, x, wh, ws, wa, wo, tm):
    T = x.shape[0]
    const = lambda i: (0, 0)
    rowb = lambda w: pl.BlockSpec((tm, w), lambda i: (i, 0))
    wsp = lambda w: pl.BlockSpec(w.shape, const, pipeline_mode=pl.Buffered(1))
    return pl.pallas_call(
        _merge_kernel,
        grid=(pl.cdiv(T, tm),),
        in_specs=[rowb(HG_WIDTH), rowb(SSM_WIDTH), rowb(ATT_WIDTH),
                  pl.BlockSpec((tm, 3 * D_MODEL), lambda i: (i, A16_GATE // (3 * D_MODEL))),
                  rowb(D_MODEL), wsp(wh), wsp(ws), wsp(wa), wsp(wo)],
        out_specs=rowb(D_MODEL),
        out_shape=jax.ShapeDtypeStruct((T, D_MODEL), F32),
        compiler_params=_cparams(("parallel",)),
        name="merge_out_proj",
    )(y_hg, y_ssm, y_att, a16, x, wh, ws, wa, wo)


def _ffn_kernel(x_ref, g_ref, wg_ref, wu_ref, wd_ref, o_ref):
    x = x_ref[...]
    h = (x * lax.rsqrt(jnp.mean(x * x, axis=-1, keepdims=True) + EPS) * g_ref[...]).astype(BF16)
    a = _silu(_dot(h, wg_ref[...])) * _dot(h, wu_ref[...])
    o_ref[...] = x + _dot(a.astype(BF16), wd_ref[...])


def _ffn(x, gain, wg, wu, wd, tm):
    T = x.shape[0]
    const = lambda i: (0, 0)
    wsp = lambda w: pl.BlockSpec(w.shape, const, pipeline_mode=pl.Buffered(1))
    return pl.pallas_call(
        _ffn_kernel,
        grid=(pl.cdiv(T, tm),),
        in_specs=[pl.BlockSpec((tm, D_MODEL), lambda i: (i, 0)), pl.BlockSpec((1, D_MODEL), const),
                  wsp(wg), wsp(wu), wsp(wd)],
        out_specs=pl.BlockSpec((tm, D_MODEL), lambda i: (i, 0)),
        out_shape=jax.ShapeDtypeStruct((T, D_MODEL), F32),
        compiler_params=_cparams(("parallel",)),
        name="ffn_dense",
    )(x, gain, wg, wu, wd)


def _router_kernel(x_ref, g_ref, wr_ref, h_ref, comb_ref, combt_ref):
    x = x_ref[...]
    h = x * lax.rsqrt(jnp.mean(x * x, axis=-1, keepdims=True) + EPS) * g_ref[...]
    hb = h.astype(BF16)
    h_ref[...] = hb
    h_lo = (h - hb.astype(F32)).astype(BF16)
    w = wr_ref[...]
    w_hi = w.astype(BF16)
    w_lo = (w - w_hi.astype(F32)).astype(BF16)
    logits = _dot(hb, w_hi) + _dot(hb, w_lo) + _dot(h_lo, w_hi)
    lane = lax.broadcasted_iota(jnp.int32, logits.shape, 1)
    logits = jnp.where(lane < N_EXPERTS, logits, NEG)
    v1 = jnp.max(logits, axis=-1, keepdims=True)
    i1 = jnp.min(jnp.where(logits == v1, lane, LANE), axis=-1, keepdims=True)
    rest = jnp.where(lane == i1, NEG, logits)
    v2 = jnp.max(rest, axis=-1, keepdims=True)
    i2 = jnp.min(jnp.where(rest == v2, lane, LANE), axis=-1, keepdims=True)
    e = jnp.exp(v2 - v1)
    w1 = 1.0 / (1.0 + e)
    w2 = e / (1.0 + e)
    comb = jnp.where(lane == i1, w1, 0.0) + jnp.where(lane == i2, w2, 0.0)
    comb_ref[...] = comb
    combt_ref[...] = comb.T[0:N_EXPERTS, :]


def _router(x, gain, wr, tm):
    T = x.shape[0]
    const = lambda i: (0, 0)
    return pl.pallas_call(
        _router_kernel,
        grid=(pl.cdiv(T, tm),),
        in_specs=[pl.BlockSpec((tm, D_MODEL), lambda i: (i, 0)), pl.BlockSpec((1, D_MODEL), const),
                  pl.BlockSpec((D_MODEL, LANE), const)],
        out_specs=[pl.BlockSpec((tm, D_MODEL), lambda i: (i, 0)), pl.BlockSpec((tm, LANE), lambda i: (i, 0)),
                   pl.BlockSpec((N_EXPERTS, tm), lambda i: (0, i))],
        out_shape=[jax.ShapeDtypeStruct((T, D_MODEL), BF16), jax.ShapeDtypeStruct((T, LANE), F32),
                   jax.ShapeDtypeStruct((N_EXPERTS, T), F32)],
        compiler_params=_cparams(("parallel",)),
        name="moe_router",
    )(x, gain, wr)


def _moe_kernel(h_ref, x_ref, comb_ref, combt_ref, tri_ref, wg_ref, wu_ref, wd_ref, o_ref,
                rank_scr, hs_scr, ys_scr, *, TB, CH, nf):
    e = pl.program_id(1)
    f = pl.program_id(2)

    @pl.when((e == 0) & (f == 0))
    def _():
        o_ref[...] = x_ref[...]
        routed_all = jnp.where(combt_ref[...] > 0.0, 1.0, 0.0).astype(BF16)
        rank_scr[...] = _dot(routed_all, tri_ref[...])

    routed = combt_ref[pl.ds(e, 1), :] > 0.0
    rank = rank_scr[pl.ds(e, 1), :]
    count = jnp.sum(jnp.where(routed, 1.0, 0.0)).astype(jnp.int32)
    slot = lax.broadcasted_iota(jnp.int32, (CH, TB), 0).astype(F32)
    lane = lax.broadcasted_iota(jnp.int32, (TB, LANE), 1)
    gate = jnp.sum(jnp.where(lane == e, comb_ref[...], 0.0), axis=-1, keepdims=True)

    def body(c, carry):
        r0 = pl.multiple_of(c * CH, 16)
        rows = pl.ds(r0, CH)
        onehot = jnp.where((rank - (c * CH).astype(F32) == slot) & routed, 1.0, 0.0).astype(BF16)

        @pl.when(f == 0)
        def _():
            hs_scr[rows, :] = _dot(onehot, h_ref[...]).astype(BF16)

        hs = hs_scr[rows, :]
        a = _silu(_dot(hs, wg_ref[0])) * _dot(hs, wu_ref[0])
        y = _dot(a.astype(BF16), wd_ref[0])

        @pl.when(f == 0)
        def _():
            ys_scr[rows, :] = y

        @pl.when(f > 0)
        def _():
            ys_scr[rows, :] += y

        @pl.when(f == nf - 1)
        def _():
            o_ref[...] += gate * _dot_tn(onehot, ys_scr[rows, :].astype(BF16))

        return carry

    lax.fori_loop(0, (count + CH - 1) // CH, body, 0)


def _moe(x, h, comb, combt, w_up, wd, TB, CH, nf):
    T = x.shape[0]
    dff = wd.shape[1]
    tf = dff // nf
    maxc = pl.cdiv(TB, CH)
    tri = jnp.triu(jnp.ones((TB, TB), BF16), k=1)
    once = pl.Buffered(1)
    return pl.pallas_call(
        functools.partial(_moe_kernel, TB=TB, CH=CH, nf=nf),
        grid=(T // TB, N_EXPERTS, nf),
        in_specs=[pl.BlockSpec((TB, D_MODEL), lambda i, e, f: (i, 0), pipeline_mode=once),
                  pl.BlockSpec((TB, D_MODEL), lambda i, e, f: (i, 0), pipeline_mode=once),
                  pl.BlockSpec((TB, LANE), lambda i, e, f: (i, 0)),
                  pl.BlockSpec((N_EXPERTS, TB), lambda i, e, f: (0, i)),
                  pl.BlockSpec((TB, TB), lambda i, e, f: (0, 0), pipeline_mode=once),
                  pl.BlockSpec((1, D_MODEL, tf), lambda i, e, f: (e, 0, f)),
                  pl.BlockSpec((1, D_MODEL, tf), lambda i, e, f: (e, 0, nf + f)),
                  pl.BlockSpec((1, tf, D_MODEL), lambda i, e, f: (e, f, 0))],
        out_specs=pl.BlockSpec((TB, D_MODEL), lambda i, e, f: (i, 0)),
        out_shape=jax.ShapeDtypeStruct((T, D_MODEL), F32),
        scratch_shapes=[pltpu.VMEM((N_EXPERTS, TB), F32), pltpu.VMEM((maxc * CH, D_MODEL), BF16),
                        pltpu.VMEM((maxc * CH, D_MODEL), F32)],
        compiler_params=_cparams(("parallel", "arbitrary", "arbitrary")),
        name="moe_top2",
    )(h, x, comb, combt, tri, w_up, w_up, wd)


def _prep_w_in(w):
    hq, hf, hi, hg = (w[:, j * HG_WIDTH:(j + 1) * HG_WIDTH] for j in range(4))
    z = w[:, _OFF_SSM:_OFF_SSM + SSM_WIDTH]
    xbc = w[:, _OFF_SSM + SSM_WIDTH:_OFF_SSM + SSM_WIDTH + SSM_CONV_DIM]
    dt = w[:, _OFF_ATT - SSM_HEADS:_OFF_ATT]
    att = w[:, _OFF_ATT:_OFF_GATE]
    gates = w[:, _OFF_GATE:]
    w16 = jnp.concatenate([gates, xbc, hq, hi, hg, z, att], axis=1).astype(BF16)
    w32 = jnp.concatenate([hf, dt, jnp.zeros((D_MODEL, LANE - SSM_HEADS), w.dtype)], axis=1).astype(BF16)
    return w16, w32


def _pad_lanes(v, n, value=0.0):
    v = v.reshape(1, -1).astype(F32)
    return jnp.pad(v, ((0, 0), (0, n - v.shape[1])), constant_values=value)


def _run_group(x, B, L, C_h, C_s, valid, tm, layer_params, states, kv_bufs):
    outs = []
    for l, P in enumerate(layer_params):
        st = states[l]
        a16, a32 = _norm_proj(x, P["mix_norm"], P["w16"], P["w32"], tm)
        y_hg, s_hg = _hgrn(a16, a32, P["lb"], P["hg_norm"], st["hgrn_t"], B, L, C_h, min(16, C_h), valid)
        y_ssm, s_ssm, s_conv = _ssd(a16, a32, st["conv"], P["conv_w"], P["conv_b"], P["dt_bias"], P["a_log"],
                                    P["d_skip"], P["ssm_norm"], st["ssm"], B, L, C_s, valid)
        if kv_bufs is None:
            y_att, kv0, kv1, kv2 = _attn_prompt(a16, P["gq"], P["gk"], P["bd"], B, L)
        else:
            y_att, kv0, kv1, kv2 = _attn_sample(a16, kv_bufs, l, P["gq"], P["gk"], P["bd"], B)
        x = _merge(y_hg, y_ssm, y_att, a16, x, P["w_out_hg"], P["w_out_ssm"], P["w_out_att"], P["w_o"], tm)
        if l % 2 == 0:
            F = P["ffn"]
            x = _ffn(x, P["ffn_norm"], F["wg"], F["wu"], F["wd"], min(tm, 256))
        else:
            F = P["moe"]
            h, comb, combt = _router(x, P["ffn_norm"], F["wr"], tm)
            x = _moe(x, h, comb, combt, F["w_up"], F["wd"], min(MOE_BLOCK, x.shape[0]), MOE_SLOTS, 2)
        outs.append((s_hg, s_ssm, s_conv, kv0, kv1, kv2))
    return x, outs


def kernel(x_prompt, x_sample, state_hgrn, state_ssm, state_conv, cache_kv_w128, cache_kv_w512, cache_kv_w2048, w_in, mix_norm, hgrn_lb_logits, hgrn_norm, ssm_conv_w, ssm_conv_b, ssm_dt_bias, ssm_a_log, ssm_d, ssm_norm, att_q_norm, att_k_norm, w_out_hgrn, w_out_ssm, w_out_att, w_o, ffn_norm, w_ffn_up, w_ffn_down, w_router, w_moe_up, w_moe_down):
    depth = w_in.shape[0]
    Bp, Lp, _ = x_prompt.shape
    Bs, Ls, _ = x_sample.shape
    R = SAMPLE_ROWS
    for buf, w in zip((cache_kv_w128, cache_kv_w512, cache_kv_w2048), ATT_WINDOWS):
        assert buf.shape[2] == w, "sample attention assumes full window buffers"

    lb_soft = jax.nn.softmax(hgrn_lb_logits.astype(F32), axis=0)
    lb_table = jnp.cumsum(lb_soft, axis=0) - lb_soft[0]
    bd = jnp.kron(jnp.eye(ATT_HEADS, dtype=F32), jnp.full((ATT_HEAD_DIM, ATT_HEAD_DIM), 1.0 / ATT_HEAD_DIM, F32)).astype(BF16)

    layer_params = []
    for l in range(depth):
        w16, w32 = _prep_w_in(w_in[l])
        P = dict(
            w16=w16, w32=w32,
            mix_norm=mix_norm[l].reshape(1, -1), ffn_norm=ffn_norm[l].reshape(1, -1),
            lb=lb_table[l].reshape(1, -1), hg_norm=hgrn_norm[l].reshape(1, -1),
            conv_w=jnp.pad(ssm_conv_w[l], ((0, 8 - SSM_CONV), (0, 0))), conv_b=ssm_conv_b[l].reshape(1, -1),
            dt_bias=_pad_lanes(ssm_dt_bias[l], LANE), a_log=_pad_lanes(ssm_a_log[l], LANE),
            d_skip=jnp.repeat(ssm_d[l].astype(F32), SSM_HEAD_DIM).reshape(1, -1),
            ssm_norm=ssm_norm[l].reshape(1, -1),
            gq=jnp.tile(att_q_norm[l], (1, ATT_HEADS)), gk=jnp.tile(att_k_norm[l], (1, ATT_HEADS)), bd=bd,
            w_out_hg=w_out_hgrn[l].astype(BF16), w_out_ssm=w_out_ssm[l].astype(BF16),
            w_out_att=w_out_att[l].astype(BF16), w_o=w_o[l].astype(BF16),
        )
        if l % 2 == 0:
            up = w_ffn_up[l // 2]
            dff = up.shape[1] // 2
            P["ffn"] = dict(wg=up[:, :dff].astype(BF16), wu=up[:, dff:].astype(BF16),
                            wd=w_ffn_down[l // 2].astype(BF16))
        else:
            P["moe"] = dict(wr=jnp.pad(w_router[l // 2].astype(F32), ((0, 0), (0, LANE - N_EXPERTS))),
                            w_up=w_moe_up[l // 2].astype(BF16), wd=w_moe_down[l // 2].astype(BF16))
        layer_params.append(P)

    zero_p = dict(hgrn_t=jnp.zeros((Bp, HG_WIDTH, HG_DIM), F32), ssm=jnp.zeros((Bp, SSM_WIDTH, SSM_STATE), F32),
                  conv=jnp.zeros((Bp, 8, SSM_CONV_DIM), F32))
    xp, outs_p = _run_group(x_prompt.reshape(Bp * Lp, D_MODEL), Bp, Lp, 64, 64, 64, 512,
                            layer_params, [zero_p] * depth, None)

    xs = jnp.pad(x_sample, ((0, 0), (0, R - Ls), (0, 0))).reshape(Bs * R, D_MODEL)
    st_s = []
    for l in range(depth):
        st_s.append(dict(
            hgrn_t=jnp.swapaxes(state_hgrn[l], -1, -2).reshape(Bs, HG_WIDTH, HG_DIM),
            ssm=state_ssm[l].reshape(Bs, SSM_WIDTH, SSM_STATE),
            conv=jnp.pad(state_conv[l], ((0, 0), (8 - (SSM_CONV - 1), 0), (0, 0)))))
    kv_s = [jnp.transpose(c, (0, 1, 3, 4, 5, 2)).reshape(depth, Bs, 2 * ATT_WIDTH, c.shape[2])
            for c in (cache_kv_w128, cache_kv_w512, cache_kv_w2048)]
    xs, outs_s = _run_group(xs, Bs, R, R, R, Ls, Bs * R, layer_params, st_s, kv_s)

    def pack(outs, B, nkv):
        s_hg = jnp.stack([jnp.swapaxes(o[0].reshape(B, HG_HEADS, HG_DIM, HG_DIM), -1, -2) for o in outs])
        s_ssm = jnp.stack([o[1].reshape(B, SSM_HEADS, SSM_HEAD_DIM, SSM_STATE) for o in outs])
        s_conv = jnp.stack([o[2][:, 8 - (SSM_CONV - 1):] for o in outs])
        kvs = []
        for j in range(3):
            n = nkv[j]
            kvs.append(jnp.stack([o[3 + j][:, :n].reshape(B, n, 2, ATT_HEADS, ATT_HEAD_DIM) for o in outs]))
        return (s_hg, s_ssm, s_conv, *kvs)

    keeps = [min(w, Lp) for w in ATT_WINDOWS]
    y_prompt = xp.reshape(Bp, Lp, D_MODEL)
    y_sample = xs.reshape(Bs, R, D_MODEL)[:, :Ls]
    return (y_prompt, y_sample, *pack(outs_p, Bp, keeps), *pack(outs_s, Bs, [Ls] * 3))
```

```python
import functools

import numpy as np
import jax
import jax.numpy as jnp
from jax import lax
from jax.experimental import pallas as pl
from jax.experimental.pallas import tpu as pltpu

F32 = jnp.float32
BF16 = jnp.bfloat16

D_MODEL = 1024
HG_HEADS, HG_DIM = 4, 128
HG_WIDTH = HG_HEADS * HG_DIM
SSM_HEADS, SSM_HEAD_DIM, SSM_GROUPS, SSM_STATE, SSM_CONV = 8, 64, 2, 128, 4
SSM_WIDTH = SSM_HEADS * SSM_HEAD_DIM
SSM_CONV_DIM = SSM_WIDTH + 2 * SSM_GROUPS * SSM_STATE
ATT_WINDOWS = (128, 512, 2048)
ATT_DILATIONS = (1, 4, 16)
ATT_HEADS, ATT_HEAD_DIM = 4, 64
ATT_WIDTH = ATT_HEADS * ATT_HEAD_DIM
ATT_KEYS = 128
ATT_SCALE = ATT_HEAD_DIM ** -0.5
N_EXPERTS = 8
EPS = 1e-6
NEG = -1e30

_OFF_SSM = 4 * HG_WIDTH
_OFF_ATT = _OFF_SSM + SSM_WIDTH + SSM_CONV_DIM + SSM_HEADS
_OFF_GATE = _OFF_ATT + 3 * 3 * ATT_WIDTH

A16_GATE, A16_XBC, A16_HQ, A16_HI, A16_HG, A16_Z, A16_ATT = 0, 3072, 4096, 4608, 5120, 5632, 6144
W16 = 8448
A32_HF, A32_DT = 0, 512
W32 = 640

LANE = 128
SAMPLE_ROWS = 16
HG_SUB = 8
MOE_BLOCK = 1024
MOE_SUB = 512
MOE_SLOTS = 160
VMEM_LIMIT = 56 * 2 ** 20


def _cparams(sem):
    return pltpu.CompilerParams(dimension_semantics=sem, vmem_limit_bytes=VMEM_LIMIT)


def _dot(a, b):
    return jnp.dot(a, b, preferred_element_type=F32)


def _dot_nt(a, b):
    return lax.dot_general(a, b, (((1,), (1,)), ((), ())), preferred_element_type=F32)


def _dot_tn(a, b):
    return lax.dot_general(a, b, (((0,), (0,)), ((), ())), preferred_element_type=F32)


def _split3(x):
    hi = x.astype(BF16)
    r = x - hi.astype(F32)
    mid = r.astype(BF16)
    lo = (r - mid.astype(F32)).astype(BF16)
    return hi, mid, lo


def _cumsum_rows(x, tril_bf16):
    hi, mid, lo = _split3(x)
    return _dot(tril_bf16, hi) + _dot(tril_bf16, mid) + _dot(tril_bf16, lo)


def _silu(x):
    return x * jax.nn.sigmoid(x)


def _tril_bf16(n):
    r = lax.broadcasted_iota(jnp.int32, (n, n), 0)
    c = lax.broadcasted_iota(jnp.int32, (n, n), 1)
    return jnp.where(c <= r, 1.0, 0.0).astype(BF16)


def _norm_proj_kernel(x_ref, g_ref, w16_ref, w32_ref, o16_ref, o32_ref, *, chunk):
    x = x_ref[...]
    h = (x * lax.rsqrt(jnp.mean(x * x, axis=-1, keepdims=True) + EPS) * g_ref[...]).astype(BF16)
    for c0 in range(0, W16, chunk):
        o16_ref[:, c0:c0 + chunk] = _dot(h, w16_ref[:, c0:c0 + chunk]).astype(BF16)
    o32_ref[...] = _dot(h, w32_ref[...])


def _norm_proj(x, gain, w16, w32, tm):
    T = x.shape[0]
    const = lambda i: (0, 0)
    return pl.pallas_call(
        functools.partial(_norm_proj_kernel, chunk=768),
        grid=(pl.cdiv(T, tm),),
        in_specs=[
            pl.BlockSpec((tm, D_MODEL), lambda i: (i, 0)),
            pl.BlockSpec((1, D_MODEL), const),
            pl.BlockSpec((D_MODEL, W16), const, pipeline_mode=pl.Buffered(1)),
            pl.BlockSpec((D_MODEL, W32), const, pipeline_mode=pl.Buffered(1)),
        ],
        out_specs=[
            pl.BlockSpec((tm, W16), lambda i: (i, 0)),
            pl.BlockSpec((tm, W32), lambda i: (i, 0)),
        ],
        out_shape=[jax.ShapeDtypeStruct((T, W16), BF16), jax.ShapeDtypeStruct((T, W32), F32)],
        compiler_params=_cparams(("parallel",)),
        name="norm_proj",
    )(x, gain, w16, w32)


def _hgrn_tables(C, sub):
    t = np.arange(C)[:, None]
    u = np.arange(C)[None, :]
    blocks = [u <= t]
    levels = []
    l = sub
    while l < C:
        upper = (t % (2 * l)) >= l
        b = (t // (2 * l)) * (2 * l) + l - 1
        blocks.append(np.where(upper, (u > b) & (u <= t), (u > t) & (u <= b)))
        levels.append(l)
        l *= 2
    for s in range(sub):
        r = (t // sub) * sub + s
        blocks.append((u > r) & (u <= t))
    sel = [u == (t // sub) * sub + s for s in range(sub)]
    d = np.concatenate(blocks, 0).astype(np.float32)
    sl = np.concatenate(sel, 0).astype(np.float32)
    table = np.block([[d, d, d, np.zeros_like(d)], [np.zeros((sl.shape[0], 3 * C), np.float32), sl]])
    return jnp.asarray(table, BF16), tuple(levels)


def _hgrn_kernel(q_ref, f_ref, i_ref, g_ref, lb_ref, gn_ref, s0_ref, tab_ref, y_ref, sout_ref, st_scr,
                 *, NB, nchunks, **kw):
    ci = pl.program_id(1)

    @pl.when(ci == 0)
    def _():
        st_scr[...] = s0_ref[...]

    for j in range(NB):
        _hgrn_chunk(q_ref.at[j], f_ref.at[j], i_ref.at[j], g_ref.at[j], lb_ref, gn_ref, tab_ref,
                    y_ref.at[j], st_scr.at[j], **kw)

    @pl.when(ci == nchunks - 1)
    def _():
        sout_ref[...] = st_scr[...]


def _hgrn_chunk(q_ref, f_ref, i_ref, g_ref, lb_ref, gn_ref, tab_ref, y_ref, st_scr, *, C, sub, levels, valid):
    lb = lb_ref[...]
    fgate = lb + (1.0 - lb) * jax.nn.sigmoid(f_ref[...])
    logf = jnp.log(fgate)
    kk = 1.0 - fgate
    rows = lax.broadcasted_iota(jnp.int32, (C, HG_WIDTH), 0)
    if valid < C:
        logf = jnp.where(rows < valid, logf, 0.0)
        kk = jnp.where(rows < valid, kk, 0.0)
    qa = _silu(q_ref[...].astype(F32))
    va = i_ref[...].astype(F32)
    ga = g_ref[...].astype(F32)

    ed = _dot(tab_ref[...], jnp.concatenate([*_split3(logf), kk.astype(BF16)], axis=0))
    nblk = 1 + len(levels) + sub
    blk = lambda j: ed[j * C:(j + 1) * C]
    G = blk(0)
    ksel = ed[nblk * C:]

    row_c = lax.broadcasted_iota(jnp.int32, (C, C), 0)
    lane_c = lax.broadcasted_iota(jnp.int32, (C, C), 1)
    off_c = lane_c - (row_c // sub) * sub

    A = [jnp.zeros((C, C), F32) for _ in range(HG_HEADS)]
    for s in range(sub):
        prod = qa * ksel[s * C:(s + 1) * C] * jnp.exp(blk(1 + len(levels) + s))
        for h in range(HG_HEADS):
            col = jnp.sum(prod[:, h * HG_DIM:(h + 1) * HG_DIM], axis=-1, keepdims=True)
            A[h] = jnp.where(off_c == s, col, A[h])
    A = [jnp.where(lane_c <= row_c, a, 0.0) for a in A]

    for li, l in enumerate(levels):
        upper = (rows % (2 * l)) >= l
        scaled = jnp.where(upper, qa, kk) * jnp.exp(blk(1 + li))
        qt = jnp.where(upper, scaled, 0.0).astype(BF16)
        kt = jnp.where(upper, 0.0, scaled).astype(BF16)
        same = (row_c // (2 * l)) == (lane_c // (2 * l))
        for h in range(HG_HEADS):
            sl = slice(h * HG_DIM, (h + 1) * HG_DIM)
            a_l = _dot_nt(qt[:, sl], kt[:, sl])
            A[h] = A[h] + (jnp.where(same, a_l, 0.0) if 2 * l < C else a_l)

    outs = []
    for h in range(HG_HEADS):
        sl = slice(h * HG_DIM, (h + 1) * HG_DIM)
        Gh, qh, kh = G[:, sl], qa[:, sl], kk[:, sl]
        vb = va[:, sl].astype(BF16)
        st = st_scr[sl, :]
        o = _dot(A[h].astype(BF16), vb) + _dot_nt((qh * jnp.exp(Gh)).astype(BF16), st.astype(BF16))
        g_last = Gh[C - 1:C]
        kdec = kh * jnp.exp(g_last - Gh)
        st_scr[sl, :] = jnp.exp(g_last) * st + _dot_tn(vb, kdec.astype(BF16))
        on = o * lax.rsqrt(jnp.mean(o * o, axis=-1, keepdims=True) + EPS) * gn_ref[...]
        outs.append(on * _silu(ga[:, sl]))
    y_ref[...] = jnp.concatenate(outs, axis=1).astype(BF16)


def _hgrn(a16, a32, lb, gn, s0t, B, L, C, sub, valid, NB):
    nch = L // C
    table, levels = _hgrn_tables(C, sub)
    const = lambda b, c: (0, 0)
    blk = lambda col: pl.BlockSpec((NB, C, HG_WIDTH), lambda b, c: (b, c, col))
    return pl.pallas_call(
        functools.partial(_hgrn_kernel, NB=NB, C=C, sub=sub, levels=levels, valid=valid, nchunks=nch),
        grid=(B // NB, nch),
        in_specs=[
            blk(A16_HQ // HG_WIDTH), blk(A32_HF // HG_WIDTH), blk(A16_HI // HG_WIDTH), blk(A16_HG // HG_WIDTH),
            pl.BlockSpec((1, HG_WIDTH), const),
            pl.BlockSpec((1, HG_DIM), const),
            pl.BlockSpec((NB, HG_WIDTH, HG_DIM), lambda b, c: (b, 0, 0)),
            pl.BlockSpec(table.shape, const),
        ],
        out_specs=[
            pl.BlockSpec((NB, C, HG_WIDTH), lambda b, c: (b, c, 0)),
            pl.BlockSpec((NB, HG_WIDTH, HG_DIM), lambda b, c: (b, 0, 0)),
        ],
        out_shape=[jax.ShapeDtypeStruct((B, L, HG_WIDTH), BF16),
                   jax.ShapeDtypeStruct((B, HG_WIDTH, HG_DIM), F32)],
        scratch_shapes=[pltpu.VMEM((NB, HG_WIDTH, HG_DIM), F32)],
        compiler_params=_cparams(("parallel", "arbitrary")),
        name="hgrn2",
    )(a16, a32, a16, a16, lb, gn, s0t, table)


def _ssd_kernel(xbc_ref, z_ref, dt_ref, pre_ref, cw_ref, cb_ref, dtb_ref, alog_ref, dsk_ref, sn_ref,
                h0_ref, y_ref, hout_ref, cout_ref, ubuf, hs, *, NB, nchunks, **kw):
    ci = pl.program_id(1)

    @pl.when(ci == 0)
    def _():
        ubuf[:, 0:8, :] = pre_ref[...]
        hs[...] = h0_ref[...]

    for j in range(NB):
        _ssd_chunk(xbc_ref.at[j], z_ref.at[j], dt_ref.at[j], cw_ref, cb_ref, dtb_ref, alog_ref, dsk_ref, sn_ref,
                   y_ref.at[j], cout_ref.at[j], ubuf.at[j], hs.at[j], **kw)

    @pl.when(ci == nchunks - 1)
    def _():
        hout_ref[...] = hs[...]


def _ssd_chunk(xbc_ref, z_ref, dt_ref, cw_ref, cb_ref, dtb_ref, alog_ref, dsk_ref, sn_ref,
               y_ref, cout_ref, ubuf, hs, *, C, valid):
    ubuf[8:8 + C, :] = xbc_ref[...].astype(F32)
    acc = cb_ref[...] + ubuf[5:5 + C, :] * cw_ref[0:1, :]
    for j in range(1, SSM_CONV):
        acc = acc + ubuf[5 + j:5 + j + C, :] * cw_ref[j:j + 1, :]
    xc = _silu(acc)
    tail = ubuf[valid:valid + 8, :]
    cout_ref[...] = tail
    ubuf[0:8, :] = tail

    x = dt_ref[...] + dtb_ref[...]
    dt = jnp.maximum(x, 0.0) + jnp.log1p(jnp.exp(-jnp.abs(x)))
    if valid < C:
        rows = lax.broadcasted_iota(jnp.int32, (C, LANE), 0)
        dt = jnp.where(rows < valid, dt, 0.0)
    a = -jnp.exp(alog_ref[...])
    cum = _cumsum_rows(dt * a, _tril_bf16(C))
    cum_t = cum.T
    dt_t = dt.T

    ri = lax.broadcasted_iota(jnp.int32, (C, C), 0)
    cj = lax.broadcasted_iota(jnp.int32, (C, C), 1)
    tril = cj <= ri
    lane = lax.broadcasted_iota(jnp.int32, (C, LANE), 1)
    srow = lax.broadcasted_iota(jnp.int32, (LANE, LANE), 0)
    bm = [xc[:, SSM_WIDTH + g * SSM_STATE:SSM_WIDTH + (g + 1) * SSM_STATE].astype(BF16) for g in range(SSM_GROUPS)]
    off_c = SSM_WIDTH + SSM_GROUPS * SSM_STATE
    cm = [xc[:, off_c + g * SSM_STATE:off_c + (g + 1) * SSM_STATE].astype(BF16) for g in range(SSM_GROUPS)]
    cb = [_dot_nt(cm[g], bm[g]) for g in range(SSM_GROUPS)]

    ys = []
    for p in range(SSM_HEADS // 2):
        g = (2 * p) // (SSM_HEADS // SSM_GROUPS)
        xp = xc[:, p * LANE:(p + 1) * LANE]
        yp = None
        colw = []
        for k, hd in enumerate((2 * p, 2 * p + 1)):
            colb = jnp.broadcast_to(cum[:, hd:hd + 1], (C, LANE))
            rowb = jnp.broadcast_to(cum_t[hd:hd + 1, :], (C, C))
            dtr = jnp.broadcast_to(dt_t[hd:hd + 1, :], (C, C))
            w = cb[g] * jnp.exp(jnp.where(tril, colb[:, :C] - rowb, NEG)) * dtr
            half = (lane < SSM_HEAD_DIM) if k == 0 else (lane >= SSM_HEAD_DIM)
            t = _dot(w.astype(BF16), jnp.where(half, xp, 0.0).astype(BF16))
            yp = t if yp is None else yp + t
            colw.append((colb, jnp.broadcast_to(dt[:, hd:hd + 1], (C, LANE))))
        first = lane < SSM_HEAD_DIM
        cum_e = jnp.where(first, colw[0][0], colw[1][0])
        dt_e = jnp.where(first, colw[0][1], colw[1][1])
        hp = hs[p * LANE:(p + 1) * LANE, :]
        yp = yp + jnp.exp(cum_e) * _dot_nt(cm[g], hp.astype(BF16))
        c_last = cum_e[C - 1:C, :]
        w_in = jnp.exp(c_last - cum_e) * dt_e
        dec = jnp.where(srow < SSM_HEAD_DIM,
                        jnp.broadcast_to(cum[C - 1:C, 2 * p:2 * p + 1], (LANE, LANE)),
                        jnp.broadcast_to(cum[C - 1:C, 2 * p + 1:2 * p + 2], (LANE, LANE)))
        hs[p * LANE:(p + 1) * LANE, :] = jnp.exp(dec) * hp + _dot_tn((xp * w_in).astype(BF16), bm[g])
        ys.append(yp)
    y = jnp.concatenate(ys, axis=1) + dsk_ref[...] * xc[:, :SSM_WIDTH]
    yz = y * _silu(z_ref[...].astype(F32))
    y_ref[...] = (yz * lax.rsqrt(jnp.mean(yz * yz, axis=-1, keepdims=True) + EPS) * sn_ref[...]).astype(BF16)


def _ssd(a16, a32, prefix, cw, cb, dtb, alog, dsk, sn, h0, B, L, C, valid, NB):
    nch = L // C
    const = lambda b, c: (0, 0)
    bat = lambda b, c: (b, 0, 0)
    return pl.pallas_call(
        functools.partial(_ssd_kernel, NB=NB, C=C, valid=valid, nchunks=nch),
        grid=(B // NB, nch),
        in_specs=[
            pl.BlockSpec((NB, C, SSM_CONV_DIM), lambda b, c: (b, c, A16_XBC // SSM_CONV_DIM)),
            pl.BlockSpec((NB, C, SSM_WIDTH), lambda b, c: (b, c, A16_Z // SSM_WIDTH)),
            pl.BlockSpec((NB, C, LANE), lambda b, c: (b, c, A32_DT // LANE)),
            pl.BlockSpec((NB, 8, SSM_CONV_DIM), bat),
            pl.BlockSpec((8, SSM_CONV_DIM), const),
            pl.BlockSpec((1, SSM_CONV_DIM), const),
            pl.BlockSpec((1, LANE), const),
            pl.BlockSpec((1, LANE), const),
            pl.BlockSpec((1, SSM_WIDTH), const),
            pl.BlockSpec((1, SSM_WIDTH), const),
            pl.BlockSpec((NB, SSM_WIDTH, SSM_STATE), bat),
        ],
        out_specs=[
            pl.BlockSpec((NB, C, SSM_WIDTH), lambda b, c: (b, c, 0)),
            pl.BlockSpec((NB, SSM_WIDTH, SSM_STATE), bat),
            pl.BlockSpec((NB, 8, SSM_CONV_DIM), bat),
        ],
        out_shape=[jax.ShapeDtypeStruct((B, L, SSM_WIDTH), BF16),
                   jax.ShapeDtypeStruct((B, SSM_WIDTH, SSM_STATE), F32),
                   jax.ShapeDtypeStruct((B, 8, SSM_CONV_DIM), F32)],
        scratch_shapes=[pltpu.VMEM((NB, C + 8, SSM_CONV_DIM), F32), pltpu.VMEM((NB, SSM_WIDTH, SSM_STATE), F32)],
        compiler_params=_cparams(("parallel", "arbitrary")),
        name="ssd",
    )(a16, a16, a32, prefix, cw, cb, dtb, alog, dsk, sn, h0)


def _head_norm(x, gain, bd):
    ss = x * x
    hi = ss.astype(BF16)
    lo = (ss - hi.astype(F32)).astype(BF16)
    ms = _dot(hi, bd) + _dot(lo, bd)
    return x * lax.rsqrt(ms + EPS) * gain


def _head_masks(rows):
    lane = lax.broadcasted_iota(jnp.int32, (rows, ATT_WIDTH), 1)
    return [(lane >= h * ATT_HEAD_DIM) & (lane < (h + 1) * ATT_HEAD_DIM) for h in range(ATT_HEADS)]


def _stack_heads(q, masks):
    return jnp.concatenate([jnp.where(m, q, 0.0) for m in masks], axis=0).astype(BF16)


def _unstack_heads(pv, m, d, masks):
    R = pv.shape[0] // ATT_HEADS
    acc, mx, den = pv[0:R], jnp.broadcast_to(m[0:R], (R, ATT_WIDTH)), jnp.broadcast_to(d[0:R], (R, ATT_WIDTH))
    for h in range(1, ATT_HEADS):
        sl = slice(h * R, (h + 1) * R)
        acc = jnp.where(masks[h], pv[sl], acc)
        mx = jnp.where(masks[h], m[sl], mx)
        den = jnp.where(masks[h], d[sl], den)
    return acc, mx, den


def _attend(qb, kb, vb, valid4, masks):
    s = jnp.where(valid4, _dot_nt(_stack_heads(qb, masks), kb), NEG)
    m = jnp.max(s, axis=-1, keepdims=True)
    p = jnp.exp(s - m)
    d = jnp.sum(p, axis=-1, keepdims=True)
    return _unstack_heads(_dot(p.astype(BF16), vb), m, d, masks)


def _ld2(pair, s):
    return jnp.concatenate([pair[0][s, :], pair[1][s, :]], axis=1)


def _st2(pair, s, val):
    pair[0][s, :] = val[:, :LANE]
    pair[1][s, :] = val[:, LANE:]


def _attn_prompt_kernel(att_ref, gq_ref, gk_ref, bd_ref, y_ref, kv0_ref, kv1_ref, kv2_ref,
                        *scr, L):
    qn, kn, vv, ya, ma, za = (scr[2 * j:2 * j + 2] for j in range(6))
    g = pl.program_id(1)
    RB = 256
    QB = ATT_KEYS

    def group(gi, dil, keep, kv_ref):
        M = L // dil
        nb = M // QB
        KW = min(2 * QB, M)
        bd = bd_ref[...]

        def norm_body(i, carry):
            r0 = pl.multiple_of(i * RB, RB)
            x = att_ref[pl.ds(r0, RB), :].astype(F32)
            _st2(qn, pl.ds(r0, RB), _head_norm(x[:, :ATT_WIDTH], gq_ref[gi:gi + 1, :], bd) * ATT_SCALE)
            _st2(kn, pl.ds(r0, RB), _head_norm(x[:, ATT_WIDTH:2 * ATT_WIDTH], gk_ref[gi:gi + 1, :], bd))
            _st2(vv, pl.ds(r0, RB), x[:, 2 * ATT_WIDTH:])
            return carry

        lax.fori_loop(0, L // RB, norm_body, 0)
        kv_ref[0, :, 0:ATT_WIDTH] = _ld2(kn, slice(L - keep, L))
        kv_ref[0, :, ATT_WIDTH:2 * ATT_WIDTH] = _ld2(vv, slice(L - keep, L))

        masks = _head_masks(QB)
        iq = lax.broadcasted_iota(jnp.int32, (ATT_HEADS * QB, KW), 0) & (QB - 1)
        ik = lax.broadcasted_iota(jnp.int32, (ATT_HEADS * QB, KW), 1)
        UNROLL = 2

        def blk_body(j, carry):
            loaded = []
            for u in range(UNROLL):
                it = j * UNROLL + u
                r = it // nb
                n = it - r * nb
                k0 = jnp.maximum(n - 1, 0) * QB
                if dil == 1:
                    qs = pl.ds(pl.multiple_of(n * QB, QB), QB)
                    ks = pl.ds(pl.multiple_of(k0, QB), KW)
                else:
                    qs = pl.ds(n * QB * dil + r, QB, stride=dil)
                    ks = pl.ds(k0 * dil + r, KW, stride=dil)
                dist = (n * QB + iq) - (k0 + ik)
                old = None if gi == 0 else (_ld2(ya, qs), _ld2(ma, qs), _ld2(za, qs))
                loaded.append((qs, _ld2(qn, qs), _ld2(kn, ks).astype(BF16), _ld2(vv, ks).astype(BF16),
                               (dist >= 0) & (dist <= ATT_KEYS), old))
            for qs, qb, kb, vb, valid4, old in loaded:
                acc, mx, den = _attend(qb, kb, vb, valid4, masks)
                if gi > 0:
                    yo, mo, zo = old
                    mn = jnp.maximum(mo, mx)
                    eo = jnp.exp(mo - mn)
                    en = jnp.exp(mx - mn)
                    acc, mx, den = yo * eo + acc * en, mn, zo * eo + den * en
                _st2(ya, qs, acc)
                _st2(ma, qs, mx)
                _st2(za, qs, den)
            return carry

        lax.fori_loop(0, dil * nb // UNROLL, blk_body, 0)

    for gi in range(3):
        @pl.when(g == gi)
        def _(gi=gi):
            group(gi, ATT_DILATIONS[gi], min(ATT_WINDOWS[gi], L), (kv0_ref, kv1_ref, kv2_ref)[gi])

    @pl.when(g == 2)
    def _():
        def out_body(i, carry):
            r0 = pl.multiple_of(i * RB, RB)
            y_ref[pl.ds(r0, RB), :] = (_ld2(ya, pl.ds(r0, RB)) / _ld2(za, pl.ds(r0, RB))).astype(BF16)
            return carry
        lax.fori_loop(0, L // RB, out_body, 0)


def _attn_prompt(a16, gq, gk, bd, B, L):
    keeps = [min(w, L) for w in ATT_WINDOWS]
    const = lambda b, g: (0, 0)
    bat = lambda b, g: (b, 0, 0)
    return pl.pallas_call(
        functools.partial(_attn_prompt_kernel, L=L),
        grid=(B, 3),
        in_specs=[
            pl.BlockSpec((L, 3 * ATT_WIDTH), lambda b, g: (b, A16_ATT // (3 * ATT_WIDTH) + g)),
            pl.BlockSpec((3, ATT_WIDTH), const),
            pl.BlockSpec((3, ATT_WIDTH), const),
            pl.BlockSpec((ATT_WIDTH, ATT_WIDTH), const),
        ],
        out_specs=[pl.BlockSpec((L, ATT_WIDTH), lambda b, g: (b, 0))]
        + [pl.BlockSpec((1, k, 2 * ATT_WIDTH), bat) for k in keeps],
        out_shape=[jax.ShapeDtypeStruct((B * L, ATT_WIDTH), BF16)]
        + [jax.ShapeDtypeStruct((B, k, 2 * ATT_WIDTH), F32) for k in keeps],
        scratch_shapes=[pltpu.VMEM((L, LANE), F32) for _ in range(12)],
        compiler_params=_cparams(("parallel", "arbitrary")),
        name="attn_prompt",
    )(a16, gq, gk, bd)


def _attn_sample_kernel(a0_ref, a1_ref, a2_ref, b0_ref, b1_ref, b2_ref, gq_ref, gk_ref, bd_ref,
                        y_ref, k0_ref, k1_ref, k2_ref):
    R = SAMPLE_ROWS
    bd = bd_ref[...]
    masks = _head_masks(R)
    ya = ma = za = None
    for gi, (a_ref, b_ref, k_ref) in enumerate(((a0_ref, b0_ref, k0_ref), (a1_ref, b1_ref, k1_ref),
                                                (a2_ref, b2_ref, k2_ref))):
        dil = ATT_DILATIONS[gi]
        wb = b_ref.shape[3]
        x = a_ref[...].astype(F32)
        q = _head_norm(x[:, :ATT_WIDTH], gq_ref[gi:gi + 1, :], bd) * ATT_SCALE
        k = _head_norm(x[:, ATT_WIDTH:2 * ATT_WIDTH], gk_ref[gi:gi + 1, :], bd)
        v = x[:, 2 * ATT_WIDTH:]
        k_ref[0] = jnp.concatenate([k[:8], v[:8]], axis=1)
        k_t = b_ref[0, 0, 0:ATT_WIDTH, :].astype(BF16)
        v_t = b_ref[0, 0, ATT_WIDTH:2 * ATT_WIDTH, :].astype(BF16)
        kb, vb = k.astype(BF16), v.astype(BF16)

        def ok(dist):
            return (dist >= 0) & (dist <= ATT_KEYS * dil) & ((dist & (dil - 1)) == 0)

        R4 = ATT_HEADS * R
        row_b = lax.broadcasted_iota(jnp.int32, (R4, wb), 0) & (R - 1)
        row_n = lax.broadcasted_iota(jnp.int32, (R4, R), 0) & (R - 1)
        ok_b = ok(wb + row_b - lax.broadcasted_iota(jnp.int32, (R4, wb), 1))
        ok_n = ok(row_n - lax.broadcasted_iota(jnp.int32, (R4, R), 1))
        qs = _stack_heads(q, masks)
        s_b = jnp.where(ok_b, _dot(qs, k_t), NEG)
        s_n = jnp.where(ok_n, _dot_nt(qs, kb), NEG)
        m = jnp.maximum(jnp.max(s_b, axis=-1, keepdims=True), jnp.max(s_n, axis=-1, keepdims=True))
        p_b = jnp.exp(s_b - m)
        p_n = jnp.exp(s_n - m)
        d = jnp.sum(p_b, axis=-1, keepdims=True) + jnp.sum(p_n, axis=-1, keepdims=True)
        pv = _dot_nt(p_b.astype(BF16), v_t) + _dot(p_n.astype(BF16), vb)
        acc, mx, den = _unstack_heads(pv, m, d, masks)
        if gi == 0:
            ya, ma, za = acc, mx, den
        else:
            mn = jnp.maximum(ma, mx)
            eo = jnp.exp(ma - mn)
            en = jnp.exp(mx - mn)
            ya, za, ma = ya * eo + acc * en, za * eo + den * en, mn
    y_ref[...] = (ya / za).astype(BF16)


def _attn_sample(a16, bufs, layer, gq, gk, bd, B):
    R = SAMPLE_ROWS
    const = lambda b: (0, 0)
    bat = lambda b: (b, 0, 0)
    base = A16_ATT // (3 * ATT_WIDTH)
    return pl.pallas_call(
        _attn_sample_kernel,
        grid=(B,),
        in_specs=[pl.BlockSpec((R, 3 * ATT_WIDTH), functools.partial(lambda b, j: (b, j), j=base + g))
                  for g in range(3)]
        + [pl.BlockSpec((1, 1, 2 * ATT_WIDTH, buf.shape[3]), lambda b: (layer, b, 0, 0)) for buf in bufs]
        + [pl.BlockSpec((3, ATT_WIDTH), const), pl.BlockSpec((3, ATT_WIDTH), const),
           pl.BlockSpec((ATT_WIDTH, ATT_WIDTH), const)],
        out_specs=[pl.BlockSpec((R, ATT_WIDTH), lambda b: (b, 0))]
        + [pl.BlockSpec((1, 8, 2 * ATT_WIDTH), bat) for _ in range(3)],
        out_shape=[jax.ShapeDtypeStruct((B * R, ATT_WIDTH), BF16)]
        + [jax.ShapeDtypeStruct((B, 8, 2 * ATT_WIDTH), F32) for _ in range(3)],
        compiler_params=_cparams(("parallel",)),
        name="attn_sample",
    )(a16, a16, a16, *bufs, gq, gk, bd)


def _merge_kernel(yh_ref, ys_ref, ya_ref, gt_ref, x_ref, wh_ref, ws_ref, wa_ref, wo_ref, o_ref):
    gt = jax.nn.sigmoid(gt_ref[...].astype(F32))
    mixed = (gt[:, 0:D_MODEL] * _dot(yh_ref[...], wh_ref[...])
             + gt[:, D_MODEL:2 * D_MODEL] * _dot(ys_ref[...], ws_ref[...])
             + gt[:, 2 * D_MODEL:] * _dot(ya_ref[...], wa_ref[...]))
    o_ref[...] = x_ref[...] + _dot(mixed.astype(BF16), wo_ref[...])


def _merge(y_hg, y_ssm, y_att, a16, x, wh, ws, wa, wo, tm):
    T = x.shape[0]
    const = lambda i: (0, 0)
    rowb = lambda w: pl.BlockSpec((tm, w), lambda i: (i, 0))
    wsp = lambda w: pl.BlockSpec(w.shape, const, pipeline_mode=pl.Buffered(1))
    return pl.pallas_call(
        _merge_kernel,
        grid=(pl.cdiv(T, tm),),
        in_specs=[rowb(HG_WIDTH), rowb(SSM_WIDTH), rowb(ATT_WIDTH),
                  pl.BlockSpec((tm, 3 * D_MODEL), lambda i: (i, A16_GATE // (3 * D_MODEL))),
                  rowb(D_MODEL), wsp(wh), wsp(ws), wsp(wa), wsp(wo)],
        out_specs=rowb(D_MODEL),
        out_shape=jax.ShapeDtypeStruct((T, D_MODEL), F32),
        compiler_params=_cparams(("parallel",)),
        name="merge_out_proj",
    )(y_hg, y_ssm, y_att, a16, x, wh, ws, wa, wo)


def _ffn_kernel(x_ref, g_ref, wg_ref, wu_ref, wd_ref, *rest):
    n = (len(rest) - 1) // 2
    cast_in, o_ref, cast_out = rest[:n], rest[n], rest[n + 1:]
    x = x_ref[...]
    h = (x * lax.rsqrt(jnp.mean(x * x, axis=-1, keepdims=True) + EPS) * g_ref[...]).astype(BF16)
    a = _silu(_dot(h, wg_ref[...])) * _dot(h, wu_ref[...])
    o_ref[...] = x + _dot(a.astype(BF16), wd_ref[...])
    for src, dst in zip(cast_in, cast_out):
        dst[...] = src[...].astype(BF16)


def _ffn(x, gain, wg, wu, wd, tm, cast=()):
    T = x.shape[0]
    steps = pl.cdiv(T, tm)
    const = lambda i: (0, 0)
    wsp = lambda w: pl.BlockSpec(w.shape, const, pipeline_mode=pl.Buffered(1))
    slab = [pl.BlockSpec((w.shape[0] // steps, w.shape[1]), lambda i: (i, 0)) for w in cast]
    for w in cast:
        assert w.shape[0] % (16 * steps) == 0
    outs = pl.pallas_call(
        _ffn_kernel,
        grid=(steps,),
        in_specs=[pl.BlockSpec((tm, D_MODEL), lambda i: (i, 0)), pl.BlockSpec((1, D_MODEL), const),
                  wsp(wg), wsp(wu), wsp(wd)] + slab,
        out_specs=[pl.BlockSpec((tm, D_MODEL), lambda i: (i, 0))] + slab,
        out_shape=[jax.ShapeDtypeStruct((T, D_MODEL), F32)] + [jax.ShapeDtypeStruct(w.shape, BF16) for w in cast],
        compiler_params=_cparams(("parallel",)),
        name="ffn_dense",
    )(x, gain, wg, wu, wd, *cast)
    return outs[0], outs[1:]


def _router_kernel(x_ref, g_ref, wr_ref, h_ref, comb_ref, combt_ref):
    x = x_ref[...]
    h = x * lax.rsqrt(jnp.mean(x * x, axis=-1, keepdims=True) + EPS) * g_ref[...]
    hb = h.astype(BF16)
    h_ref[...] = hb
    h_lo = (h - hb.astype(F32)).astype(BF16)
    w = wr_ref[...]
    w_hi = w.astype(BF16)
    w_lo = (w - w_hi.astype(F32)).astype(BF16)
    logits = _dot(hb, w_hi) + _dot(hb, w_lo) + _dot(h_lo, w_hi)
    lane = lax.broadcasted_iota(jnp.int32, logits.shape, 1)
    logits = jnp.where(lane < N_EXPERTS, logits, NEG)
    v1 = jnp.max(logits, axis=-1, keepdims=True)
    i1 = jnp.min(jnp.where(logits == v1, lane, LANE), axis=-1, keepdims=True)
    rest = jnp.where(lane == i1, NEG, logits)
    v2 = jnp.max(rest, axis=-1, keepdims=True)
    i2 = jnp.min(jnp.where(rest == v2, lane, LANE), axis=-1, keepdims=True)
    e = jnp.exp(v2 - v1)
    w1 = 1.0 / (1.0 + e)
    w2 = e / (1.0 + e)
    comb = jnp.where(lane == i1, w1, 0.0) + jnp.where(lane == i2, w2, 0.0)
    comb_ref[...] = comb
    combt_ref[...] = comb.T[0:N_EXPERTS, :]


def _router(x, gain, wr, tm):
    T = x.shape[0]
    const = lambda i: (0, 0)
    return pl.pallas_call(
        _router_kernel,
        grid=(pl.cdiv(T, tm),),
        in_specs=[pl.BlockSpec((tm, D_MODEL), lambda i: (i, 0)), pl.BlockSpec((1, D_MODEL), const),
                  pl.BlockSpec((D_MODEL, LANE), const)],
        out_specs=[pl.BlockSpec((tm, D_MODEL), lambda i: (i, 0)), pl.BlockSpec((tm, LANE), lambda i: (i, 0)),
                   pl.BlockSpec((N_EXPERTS, tm), lambda i: (0, i))],
        out_shape=[jax.ShapeDtypeStruct((T, D_MODEL), BF16), jax.ShapeDtypeStruct((T, LANE), F32),
                   jax.ShapeDtypeStruct((N_EXPERTS, T), F32)],
        compiler_params=_cparams(("parallel",)),
        name="moe_router",
    )(x, gain, wr)


def _moe_kernel(h_ref, x_ref, comb_ref, combt_ref, tri_ref, wg_ref, wu_ref, wd_ref, o_ref,
                rank_scr, hs_scr, ys_scr, *, TB, SB, CH, nf):
    e = pl.program_id(1)
    f = pl.program_id(2)

    nsub = TB // SB
    subs = [slice(j * SB, (j + 1) * SB) for j in range(nsub)]

    @pl.when((e == 0) & (f == 0))
    def _():
        o_ref[...] = x_ref[...]
        routed_all = jnp.where(combt_ref[...] > 0.0, 1.0, 0.0).astype(BF16)
        for sb in subs:
            rank_scr[:, sb] = _dot(routed_all[:, sb], tri_ref[...])

    routed = combt_ref[pl.ds(e, 1), :] > 0.0
    rank = rank_scr[pl.ds(e, 1), :]
    count = None
    for sb in subs:
        cj = jnp.sum(jnp.where(routed[:, sb], 1.0, 0.0)).astype(jnp.int32)
        count = cj if count is None else jnp.maximum(count, cj)
    slot = lax.broadcasted_iota(jnp.int32, (CH, SB), 0).astype(F32)
    lane = lax.broadcasted_iota(jnp.int32, (TB, LANE), 1)
    gate = jnp.sum(jnp.where(lane == e, comb_ref[...], 0.0), axis=-1, keepdims=True)

    def body(c, carry):
        rows = pl.ds(pl.multiple_of(c * (nsub * CH), 16), nsub * CH)
        base = (c * CH).astype(F32)
        onehots = [jnp.where((rank[:, sb] - base == slot) & routed[:, sb], 1.0, 0.0).astype(BF16) for sb in subs]

        @pl.when(f == 0)
        def _():
            hs_scr[rows, :] = jnp.concatenate(
                [_dot(p, h_ref[sb, :]) for p, sb in zip(onehots, subs)], axis=0).astype(BF16)

        hs = hs_scr[rows, :]
        a = _silu(_dot(hs, wg_ref[0])) * _dot(hs, wu_ref[0])
        y = _dot(a.astype(BF16), wd_ref[0])

        @pl.when(f == 0)
        def _():
            ys_scr[rows, :] = y

        @pl.when(f > 0)
        def _():
            ys_scr[rows, :] += y

        @pl.when(f == nf - 1)
        def _():
            ys = ys_scr[rows, :].astype(BF16)
            for j, (p, sb) in enumerate(zip(onehots, subs)):
                o_ref[sb, :] += gate[sb] * _dot_tn(p, ys[j * CH:(j + 1) * CH])

        return carry

    lax.fori_loop(0, (count + CH - 1) // CH, body, 0)


def _moe(x, h, comb, combt, w_up, wd, TB, SB, CH, nf):
    T = x.shape[0]
    dff = wd.shape[1]
    tf = dff // nf
    maxc = pl.cdiv(SB, CH) * (TB // SB)
    tri = jnp.triu(jnp.ones((SB, SB), BF16), k=1)
    once = pl.Buffered(1)
    return pl.pallas_call(
        functools.partial(_moe_kernel, TB=TB, SB=SB, CH=CH, nf=nf),
        grid=(T // TB, N_EXPERTS, nf),
        in_specs=[pl.BlockSpec((TB, D_MODEL), lambda i, e, f: (i, 0), pipeline_mode=once),
                  pl.BlockSpec((TB, D_MODEL), lambda i, e, f: (i, 0), pipeline_mode=once),
                  pl.BlockSpec((TB, LANE), lambda i, e, f: (i, 0)),
                  pl.BlockSpec((N_EXPERTS, TB), lambda i, e, f: (0, i)),
                  pl.BlockSpec((SB, SB), lambda i, e, f: (0, 0), pipeline_mode=once),
                  pl.BlockSpec((1, D_MODEL, tf), lambda i, e, f: (e, 0, f)),
                  pl.BlockSpec((1, D_MODEL, tf), lambda i, e, f: (e, 0, nf + f)),
                  pl.BlockSpec((1, tf, D_MODEL), lambda i, e, f: (e, f, 0))],
        out_specs=pl.BlockSpec((TB, D_MODEL), lambda i, e, f: (i, 0)),
        out_shape=jax.ShapeDtypeStruct((T, D_MODEL), F32),
        scratch_shapes=[pltpu.VMEM((N_EXPERTS, TB), F32), pltpu.VMEM((maxc * CH, D_MODEL), BF16),
                        pltpu.VMEM((maxc * CH, D_MODEL), F32)],
        compiler_params=_cparams(("parallel", "arbitrary", "arbitrary")),
        name="moe_top2",
    )(h, x, comb, combt, tri, w_up, w_up, wd)


def _prep_w_in(w):
    hq, hf, hi, hg = (w[:, j * HG_WIDTH:(j + 1) * HG_WIDTH] for j in range(4))
    z = w[:, _OFF_SSM:_OFF_SSM + SSM_WIDTH]
    xbc = w[:, _OFF_SSM + SSM_WIDTH:_OFF_SSM + SSM_WIDTH + SSM_CONV_DIM]
    dt = w[:, _OFF_ATT - SSM_HEADS:_OFF_ATT]
    att = w[:, _OFF_ATT:_OFF_GATE]
    gates = w[:, _OFF_GATE:]
    w16 = jnp.concatenate([gates, xbc, hq, hi, hg, z, att], axis=1).astype(BF16)
    w32 = jnp.concatenate([hf, dt, jnp.zeros((D_MODEL, LANE - SSM_HEADS), w.dtype)], axis=1).astype(BF16)
    return w16, w32


def _pad_lanes(v, n, value=0.0):
    v = v.reshape(1, -1).astype(F32)
    return jnp.pad(v, ((0, 0), (0, n - v.shape[1])), constant_values=value)


def _run_group(x, B, L, C_h, C_s, valid, NB, tm, layer_params, states, kv_bufs):
    outs = []
    for l, P in enumerate(layer_params):
        st = states[l]
        a16, a32 = _norm_proj(x, P["mix_norm"], P["w16"], P["w32"], tm)
        a16b, a32b = a16.reshape(B, L, W16), a32.reshape(B, L, W32)
        y_hg, s_hg = _hgrn(a16b, a32b, P["lb"], P["hg_norm"], st["hgrn_t"], B, L, C_h, HG_SUB, valid, NB)
        y_ssm, s_ssm, s_conv = _ssd(a16b, a32b, st["conv"], P["conv_w"], P["conv_b"], P["dt_bias"], P["a_log"],
                                    P["d_skip"], P["ssm_norm"], st["ssm"], B, L, C_s, valid, NB)
        y_hg, y_ssm = y_hg.reshape(B * L, HG_WIDTH), y_ssm.reshape(B * L, SSM_WIDTH)
        if kv_bufs is None:
            y_att, kv0, kv1, kv2 = _attn_prompt(a16, P["gq"], P["gk"], P["bd"], B, L)
        else:
            y_att, kv0, kv1, kv2 = _attn_sample(a16, kv_bufs, l, P["gq"], P["gk"], P["bd"], B)
        x = _merge(y_hg, y_ssm, y_att, a16, x, P["w_out_hg"], P["w_out_ssm"], P["w_out_att"], P["w_o"], tm)
        if l % 2 == 0:
            F = P["ffn"]
            nxt = layer_params[l + 1]["moe"] if l + 1 < len(layer_params) else {}
            pending = [k for k in ("w_up", "wd") if k + "_f32" in nxt and k not in nxt]
            cast = [nxt[k + "_f32"].reshape(-1, nxt[k + "_f32"].shape[-1]) for k in pending]
            x, done = _ffn(x, P["ffn_norm"], F["wg"], F["wu"], F["wd"], min(tm, 256), cast)
            for k, w in zip(pending, done):
                nxt[k] = w.reshape(nxt[k + "_f32"].shape)
        else:
            F = P["moe"]
            h, comb, combt = _router(x, P["ffn_norm"], F["wr"], tm)
            x = _moe(x, h, comb, combt, F["w_up"], F["wd"], min(MOE_BLOCK, x.shape[0]), min(MOE_SUB, x.shape[0]), MOE_SLOTS, 2)
        outs.append((s_hg, s_ssm, s_conv, kv0, kv1, kv2))
    return x, outs


def kernel(x_prompt, x_sample, state_hgrn, state_ssm, state_conv, cache_kv_w128, cache_kv_w512, cache_kv_w2048, w_in, mix_norm, hgrn_lb_logits, hgrn_norm, ssm_conv_w, ssm_conv_b, ssm_dt_bias, ssm_a_log, ssm_d, ssm_norm, att_q_norm, att_k_norm, w_out_hgrn, w_out_ssm, w_out_att, w_o, ffn_norm, w_ffn_up, w_ffn_down, w_router, w_moe_up, w_moe_down):
    depth = w_in.shape[0]
    Bp, Lp, _ = x_prompt.shape
    Bs, Ls, _ = x_sample.shape
    R = SAMPLE_ROWS
    for buf, w in zip((cache_kv_w128, cache_kv_w512, cache_kv_w2048), ATT_WINDOWS):
        assert buf.shape[2] == w, "sample attention assumes full window buffers"

    lb_soft = jax.nn.softmax(hgrn_lb_logits.astype(F32), axis=0)
    lb_table = jnp.cumsum(lb_soft, axis=0) - lb_soft[0]
    bd = jnp.kron(jnp.eye(ATT_HEADS, dtype=F32), jnp.full((ATT_HEAD_DIM, ATT_HEAD_DIM), 1.0 / ATT_HEAD_DIM, F32)).astype(BF16)

    layer_params = []
    for l in range(depth):
        w16, w32 = _prep_w_in(w_in[l])
        P = dict(
            w16=w16, w32=w32,
            mix_norm=mix_norm[l].reshape(1, -1), ffn_norm=ffn_norm[l].reshape(1, -1),
            lb=lb_table[l].reshape(1, -1), hg_norm=hgrn_norm[l].reshape(1, -1),
            conv_w=jnp.pad(ssm_conv_w[l], ((0, 8 - SSM_CONV), (0, 0))), conv_b=ssm_conv_b[l].reshape(1, -1),
            dt_bias=_pad_lanes(ssm_dt_bias[l], LANE), a_log=_pad_lanes(ssm_a_log[l], LANE),
            d_skip=jnp.repeat(ssm_d[l].astype(F32), SSM_HEAD_DIM).reshape(1, -1),
            ssm_norm=ssm_norm[l].reshape(1, -1),
            gq=jnp.tile(att_q_norm[l], (1, ATT_HEADS)), gk=jnp.tile(att_k_norm[l], (1, ATT_HEADS)), bd=bd,
            w_out_hg=w_out_hgrn[l].astype(BF16), w_out_ssm=w_out_ssm[l].astype(BF16),
            w_out_att=w_out_att[l].astype(BF16), w_o=w_o[l].astype(BF16),
        )
        if l % 2 == 0:
            up = w_ffn_up[l // 2]
            dff = up.shape[1] // 2
            P["ffn"] = dict(wg=up[:, :dff].astype(BF16), wu=up[:, dff:].astype(BF16),
                            wd=w_ffn_down[l // 2].astype(BF16))
        else:
            P["moe"] = dict(wr=jnp.pad(w_router[l // 2].astype(F32), ((0, 0), (0, LANE - N_EXPERTS))),
                            w_up_f32=w_moe_up[l // 2], wd_f32=w_moe_down[l // 2])
        layer_params.append(P)

    zero_p = dict(hgrn_t=jnp.zeros((Bp, HG_WIDTH, HG_DIM), F32), ssm=jnp.zeros((Bp, SSM_WIDTH, SSM_STATE), F32),
                  conv=jnp.zeros((Bp, 8, SSM_CONV_DIM), F32))
    xp, outs_p = _run_group(x_prompt.reshape(Bp * Lp, D_MODEL), Bp, Lp, 64, 64, 64, 2, 512,
                            layer_params, [zero_p] * depth, None)

    xs = jnp.pad(x_sample, ((0, 0), (0, R - Ls), (0, 0))).reshape(Bs * R, D_MODEL)
    st_s = []
    for l in range(depth):
        st_s.append(dict(
            hgrn_t=jnp.swapaxes(state_hgrn[l], -1, -2).reshape(Bs, HG_WIDTH, HG_DIM),
            ssm=state_ssm[l].reshape(Bs, SSM_WIDTH, SSM_STATE),
            conv=jnp.pad(state_conv[l], ((0, 0), (8 - (SSM_CONV - 1), 0), (0, 0)))))
    kv_s = [jnp.transpose(c, (0, 1, 3, 4, 5, 2)).reshape(depth, Bs, 2 * ATT_WIDTH, c.shape[2])
            for c in (cache_kv_w128, cache_kv_w512, cache_kv_w2048)]
    xs, outs_s = _run_group(xs, Bs, R, R, R, Ls, 4, Bs * R, layer_params, st_s, kv_s)

    def pack(outs, B, nkv):
        s_hg = jnp.stack([jnp.swapaxes(o[0].reshape(B, HG_HEADS, HG_DIM, HG_DIM), -1, -2) for o in outs])
        s_ssm = jnp.stack([o[1].reshape(B, SSM_HEADS, SSM_HEAD_DIM, SSM_STATE) for o in outs])
        s_conv = jnp.stack([o[2][:, 8 - (SSM_CONV - 1):] for o in outs])
        kvs = []
        for j in range(3):
            n = nkv[j]
            kvs.append(jnp.stack([o[3 + j][:, :n].reshape(B, n, 2, ATT_HEADS, ATT_HEAD_DIM) for o in outs]))
        return (s_hg, s_ssm, s_conv, *kvs)

    keeps = [min(w, Lp) for w in ATT_WINDOWS]
    y_prompt = xp.reshape(Bp, Lp, D_MODEL)
    y_sample = xs.reshape(Bs, R, D_MODEL)[:, :Ls]
    return (y_prompt, y_sample, *pack(outs_p, Bp, keeps), *pack(outs_s, Bs, [Ls] * 3))
```

```python
import functools

import numpy as np
import jax
import jax.numpy as jnp
from jax import lax
from jax.experimental import pallas as pl
from jax.experimental.pallas import tpu as pltpu

F32 = jnp.float32
BF16 = jnp.bfloat16

D_MODEL = 1024
HG_HEADS, HG_DIM = 4, 128
HG_WIDTH = HG_HEADS * HG_DIM
SSM_HEADS, SSM_HEAD_DIM, SSM_GROUPS, SSM_STATE, SSM_CONV = 8, 64, 2, 128, 4
SSM_WIDTH = SSM_HEADS * SSM_HEAD_DIM
SSM_CONV_DIM = SSM_WIDTH + 2 * SSM_GROUPS * SSM_STATE
ATT_WINDOWS = (128, 512, 2048)
ATT_DILATIONS = (1, 4, 16)
ATT_HEADS, ATT_HEAD_DIM = 4, 64
ATT_WIDTH = ATT_HEADS * ATT_HEAD_DIM
ATT_KEYS = 128
ATT_SCALE = ATT_HEAD_DIM ** -0.5
N_EXPERTS = 8
EPS = 1e-6
NEG = -1e30

_OFF_SSM = 4 * HG_WIDTH
_OFF_ATT = _OFF_SSM + SSM_WIDTH + SSM_CONV_DIM + SSM_HEADS
_OFF_GATE = _OFF_ATT + 3 * 3 * ATT_WIDTH

A16_GATE, A16_XBC, A16_HQ, A16_HI, A16_HG, A16_Z, A16_ATT = 0, 3072, 4096, 4608, 5120, 5632, 6144
W16 = 8448
A32_HF, A32_DT = 0, 512
W32 = 640

LANE = 128
SAMPLE_ROWS = 16
HG_SUB = 8
MOE_BLOCK = 1024
MOE_SUB = 512
MOE_SLOTS = (128, 144, 160)
VMEM_LIMIT = 56 * 2 ** 20


def _cparams(sem):
    return pltpu.CompilerParams(dimension_semantics=sem, vmem_limit_bytes=VMEM_LIMIT)


def _dot(a, b):
    return jnp.dot(a, b, preferred_element_type=F32)


def _dot_nt(a, b):
    return lax.dot_general(a, b, (((1,), (1,)), ((), ())), preferred_element_type=F32)


def _dot_tn(a, b):
    return lax.dot_general(a, b, (((0,), (0,)), ((), ())), preferred_element_type=F32)


def _split3(x):
    hi = x.astype(BF16)
    r = x - hi.astype(F32)
    mid = r.astype(BF16)
    lo = (r - mid.astype(F32)).astype(BF16)
    return hi, mid, lo


def _cumsum_rows(x, tril_bf16):
    hi, mid, lo = _split3(x)
    return _dot(tril_bf16, hi) + _dot(tril_bf16, mid) + _dot(tril_bf16, lo)


def _silu(x):
    return x * jax.nn.sigmoid(x)


def _tril_bf16(n):
    r = lax.broadcasted_iota(jnp.int32, (n, n), 0)
    c = lax.broadcasted_iota(jnp.int32, (n, n), 1)
    return jnp.where(c <= r, 1.0, 0.0).astype(BF16)


def _norm_proj_kernel(x_ref, g_ref, w16_ref, w32_ref, o16_ref, o32_ref, *, chunk):
    x = x_ref[...]
    h = (x * lax.rsqrt(jnp.mean(x * x, axis=-1, keepdims=True) + EPS) * g_ref[...]).astype(BF16)
    for c0 in range(0, W16, chunk):
        o16_ref[:, c0:c0 + chunk] = _dot_nt(h, w16_ref[c0:c0 + chunk, :]).astype(BF16)
    o32_ref[...] = _dot_nt(h, w32_ref[...])


def _norm_proj(x, gain, w16, w32, tm):
    T = x.shape[0]
    const = lambda i: (0, 0)
    return pl.pallas_call(
        functools.partial(_norm_proj_kernel, chunk=768),
        grid=(pl.cdiv(T, tm),),
        in_specs=[
            pl.BlockSpec((tm, D_MODEL), lambda i: (i, 0)),
            pl.BlockSpec((1, D_MODEL), const),
            pl.BlockSpec((W16, D_MODEL), const, pipeline_mode=pl.Buffered(1)),
            pl.BlockSpec((W32, D_MODEL), const, pipeline_mode=pl.Buffered(1)),
        ],
        out_specs=[
            pl.BlockSpec((tm, W16), lambda i: (i, 0)),
            pl.BlockSpec((tm, W32), lambda i: (i, 0)),
        ],
        out_shape=[jax.ShapeDtypeStruct((T, W16), BF16), jax.ShapeDtypeStruct((T, W32), F32)],
        compiler_params=_cparams(("parallel",)),
        name="norm_proj",
    )(x, gain, w16, w32)


def _hgrn_tables(C, sub):
    t = np.arange(C)[:, None]
    u = np.arange(C)[None, :]
    blocks = [u <= t]
    levels = []
    l = sub
    while l < C:
        upper = (t % (2 * l)) >= l
        b = (t // (2 * l)) * (2 * l) + l - 1
        blocks.append(np.where(upper, (u > b) & (u <= t), (u > t) & (u <= b)))
        levels.append(l)
        l *= 2
    for s in range(sub):
        r = (t // sub) * sub + s
        blocks.append((u > r) & (u <= t))
    sel = [u == (t // sub) * sub + s for s in range(sub)]
    d = np.concatenate(blocks, 0).astype(np.float32)
    sl = np.concatenate(sel, 0).astype(np.float32)
    table = np.block([[d, d, d, np.zeros_like(d)], [np.zeros((sl.shape[0], 3 * C), np.float32), sl]])
    return jnp.asarray(table, BF16), tuple(levels)


def _hgrn_kernel(q_ref, f_ref, i_ref, g_ref, lb_ref, gn_ref, s0_ref, tab_ref, y_ref, sout_ref, st_scr,
                 *, NB, nchunks, **kw):
    ci = pl.program_id(1)

    @pl.when(ci == 0)
    def _():
        st_scr[...] = s0_ref[...]

    for j in range(NB):
        _hgrn_chunk(q_ref.at[j], f_ref.at[j], i_ref.at[j], g_ref.at[j], lb_ref, gn_ref, tab_ref,
                    y_ref.at[j], st_scr.at[j], **kw)

    @pl.when(ci == nchunks - 1)
    def _():
        sout_ref[...] = st_scr[...]


def _hgrn_chunk(q_ref, f_ref, i_ref, g_ref, lb_ref, gn_ref, tab_ref, y_ref, st_scr, *, C, sub, levels, valid):
    lb = lb_ref[...]
    fgate = lb + (1.0 - lb) * jax.nn.sigmoid(f_ref[...])
    logf = jnp.log(fgate)
    kk = 1.0 - fgate
    rows = lax.broadcasted_iota(jnp.int32, (C, HG_WIDTH), 0)
    if valid < C:
        logf = jnp.where(rows < valid, logf, 0.0)
        kk = jnp.where(rows < valid, kk, 0.0)
    qa = _silu(q_ref[...].astype(F32))
    va = i_ref[...].astype(F32)
    ga = g_ref[...].astype(F32)

    ed = _dot(tab_ref[...], jnp.concatenate([*_split3(logf), kk.astype(BF16)], axis=0))
    nblk = 1 + len(levels) + sub
    blk = lambda j: ed[j * C:(j + 1) * C]
    G = blk(0)
    ksel = ed[nblk * C:]

    row_c = lax.broadcasted_iota(jnp.int32, (C, C), 0)
    lane_c = lax.broadcasted_iota(jnp.int32, (C, C), 1)
    off_c = lane_c - (row_c // sub) * sub

    A = [jnp.zeros((C, C), F32) for _ in range(HG_HEADS)]
    for s in range(sub):
        prod = qa * ksel[s * C:(s + 1) * C] * jnp.exp(blk(1 + len(levels) + s))
        for h in range(HG_HEADS):
            col = jnp.sum(prod[:, h * HG_DIM:(h + 1) * HG_DIM], axis=-1, keepdims=True)
            A[h] = jnp.where(off_c == s, col, A[h])
    A = [jnp.where(lane_c <= row_c, a, 0.0) for a in A]

    for li, l in enumerate(levels):
        upper = (rows % (2 * l)) >= l
        scaled = jnp.where(upper, qa, kk) * jnp.exp(blk(1 + li))
        qt = jnp.where(upper, scaled, 0.0).astype(BF16)
        kt = jnp.where(upper, 0.0, scaled).astype(BF16)
        same = (row_c // (2 * l)) == (lane_c // (2 * l))
        for h in range(HG_HEADS):
            sl = slice(h * HG_DIM, (h + 1) * HG_DIM)
            a_l = _dot_nt(qt[:, sl], kt[:, sl])
            A[h] = A[h] + (jnp.where(same, a_l, 0.0) if 2 * l < C else a_l)

    outs = []
    for h in range(HG_HEADS):
        sl = slice(h * HG_DIM, (h + 1) * HG_DIM)
        Gh, qh, kh = G[:, sl], qa[:, sl], kk[:, sl]
        vb = va[:, sl].astype(BF16)
        st = st_scr[sl, :]
        o = _dot(A[h].astype(BF16), vb) + _dot_nt((qh * jnp.exp(Gh)).astype(BF16), st.astype(BF16))
        g_last = Gh[C - 1:C]
        kdec = kh * jnp.exp(g_last - Gh)
        st_scr[sl, :] = jnp.exp(g_last) * st + _dot_tn(vb, kdec.astype(BF16))
        on = o * lax.rsqrt(jnp.mean(o * o, axis=-1, keepdims=True) + EPS) * gn_ref[...]
        outs.append(on * _silu(ga[:, sl]))
    y_ref[...] = jnp.concatenate(outs, axis=1).astype(BF16)


def _hgrn(a16, a32, lb, gn, s0t, B, L, C, sub, valid, NB):
    nch = L // C
    table, levels = _hgrn_tables(C, sub)
    const = lambda b, c: (0, 0)
    blk = lambda col: pl.BlockSpec((NB, C, HG_WIDTH), lambda b, c: (b, c, col))
    return pl.pallas_call(
        functools.partial(_hgrn_kernel, NB=NB, C=C, sub=sub, levels=levels, valid=valid, nchunks=nch),
        grid=(B // NB, nch),
        in_specs=[
            blk(A16_HQ // HG_WIDTH), blk(A32_HF // HG_WIDTH), blk(A16_HI // HG_WIDTH), blk(A16_HG // HG_WIDTH),
            pl.BlockSpec((1, HG_WIDTH), const),
            pl.BlockSpec((1, HG_DIM), const),
            pl.BlockSpec((NB, HG_WIDTH, HG_DIM), lambda b, c: (b, 0, 0)),
            pl.BlockSpec(table.shape, const),
        ],
        out_specs=[
            pl.BlockSpec((NB, C, HG_WIDTH), lambda b, c: (b, c, 0)),
            pl.BlockSpec((NB, HG_WIDTH, HG_DIM), lambda b, c: (b, 0, 0)),
        ],
        out_shape=[jax.ShapeDtypeStruct((B, L, HG_WIDTH), BF16),
                   jax.ShapeDtypeStruct((B, HG_WIDTH, HG_DIM), F32)],
        scratch_shapes=[pltpu.VMEM((NB, HG_WIDTH, HG_DIM), F32)],
        compiler_params=_cparams(("parallel", "arbitrary")),
        name="hgrn2",
    )(a16, a32, a16, a16, lb, gn, s0t, table)


def _ssd_kernel(xbc_ref, z_ref, dt_ref, pre_ref, cw_ref, cb_ref, dtb_ref, alog_ref, dsk_ref, sn_ref,
                h0_ref, y_ref, hout_ref, cout_ref, ubuf, hs, *, NB, nchunks, **kw):
    ci = pl.program_id(1)

    @pl.when(ci == 0)
    def _():
        ubuf[:, 0:8, :] = pre_ref[...]
        hs[...] = h0_ref[...]

    for j in range(NB):
        _ssd_chunk(xbc_ref.at[j], z_ref.at[j], dt_ref.at[j], cw_ref, cb_ref, dtb_ref, alog_ref, dsk_ref, sn_ref,
                   y_ref.at[j], cout_ref.at[j], ubuf.at[j], hs.at[j], **kw)

    @pl.when(ci == nchunks - 1)
    def _():
        hout_ref[...] = hs[...]


def _ssd_chunk(xbc_ref, z_ref, dt_ref, cw_ref, cb_ref, dtb_ref, alog_ref, dsk_ref, sn_ref,
               y_ref, cout_ref, ubuf, hs, *, C, valid):
    ubuf[8:8 + C, :] = xbc_ref[...].astype(F32)
    acc = cb_ref[...] + ubuf[5:5 + C, :] * cw_ref[0:1, :]
    for j in range(1, SSM_CONV):
        acc = acc + ubuf[5 + j:5 + j + C, :] * cw_ref[j:j + 1, :]
    xc = _silu(acc)
    tail = ubuf[valid:valid + 8, :]
    cout_ref[...] = tail
    ubuf[0:8, :] = tail

    x = dt_ref[...] + dtb_ref[...]
    dt = jnp.maximum(x, 0.0) + jnp.log1p(jnp.exp(-jnp.abs(x)))
    if valid < C:
        rows = lax.broadcasted_iota(jnp.int32, (C, LANE), 0)
        dt = jnp.where(rows < valid, dt, 0.0)
    a = -jnp.exp(alog_ref[...])
    cum = _cumsum_rows(dt * a, _tril_bf16(C))
    cum_t = cum.T
    dt_t = dt.T

    ri = lax.broadcasted_iota(jnp.int32, (C, C), 0)
    cj = lax.broadcasted_iota(jnp.int32, (C, C), 1)
    tril = cj <= ri
    lane = lax.broadcasted_iota(jnp.int32, (C, LANE), 1)
    srow = lax.broadcasted_iota(jnp.int32, (LANE, LANE), 0)
    bm = [xc[:, SSM_WIDTH + g * SSM_STATE:SSM_WIDTH + (g + 1) * SSM_STATE].astype(BF16) for g in range(SSM_GROUPS)]
    off_c = SSM_WIDTH + SSM_GROUPS * SSM_STATE
    cm = [xc[:, off_c + g * SSM_STATE:off_c + (g + 1) * SSM_STATE].astype(BF16) for g in range(SSM_GROUPS)]
    cb = [_dot_nt(cm[g], bm[g]) for g in range(SSM_GROUPS)]

    ys = []
    for p in range(SSM_HEADS // 2):
        g = (2 * p) // (SSM_HEADS // SSM_GROUPS)
        xp = xc[:, p * LANE:(p + 1) * LANE]
        yp = None
        colw = []
        for k, hd in enumerate((2 * p, 2 * p + 1)):
            colb = jnp.broadcast_to(cum[:, hd:hd + 1], (C, LANE))
            rowb = jnp.broadcast_to(cum_t[hd:hd + 1, :], (C, C))
            dtr = jnp.broadcast_to(dt_t[hd:hd + 1, :], (C, C))
            w = cb[g] * jnp.exp(jnp.where(tril, colb[:, :C] - rowb, NEG)) * dtr
            half = (lane < SSM_HEAD_DIM) if k == 0 else (lane >= SSM_HEAD_DIM)
            t = _dot(w.astype(BF16), jnp.where(half, xp, 0.0).astype(BF16))
            yp = t if yp is None else yp + t
            colw.append((colb, jnp.broadcast_to(dt[:, hd:hd + 1], (C, LANE))))
        first = lane < SSM_HEAD_DIM
        cum_e = jnp.where(first, colw[0][0], colw[1][0])
        dt_e = jnp.where(first, colw[0][1], colw[1][1])
        hp = hs[p * LANE:(p + 1) * LANE, :]
        yp = yp + jnp.exp(cum_e) * _dot_nt(cm[g], hp.astype(BF16))
        c_last = cum_e[C - 1:C, :]
        w_in = jnp.exp(c_last - cum_e) * dt_e
        dec = jnp.where(srow < SSM_HEAD_DIM,
                        jnp.broadcast_to(cum[C - 1:C, 2 * p:2 * p + 1], (LANE, LANE)),
                        jnp.broadcast_to(cum[C - 1:C, 2 * p + 1:2 * p + 2], (LANE, LANE)))
        hs[p * LANE:(p + 1) * LANE, :] = jnp.exp(dec) * hp + _dot_tn((xp * w_in).astype(BF16), bm[g])
        ys.append(yp)
    y = jnp.concatenate(ys, axis=1) + dsk_ref[...] * xc[:, :SSM_WIDTH]
    yz = y * _silu(z_ref[...].astype(F32))
    y_ref[...] = (yz * lax.rsqrt(jnp.mean(yz * yz, axis=-1, keepdims=True) + EPS) * sn_ref[...]).astype(BF16)


def _ssd(a16, a32, prefix, cw, cb, dtb, alog, dsk, sn, h0, B, L, C, valid, NB):
    nch = L // C
    const = lambda b, c: (0, 0)
    bat = lambda b, c: (b, 0, 0)
    return pl.pallas_call(
        functools.partial(_ssd_kernel, NB=NB, C=C, valid=valid, nchunks=nch),
        grid=(B // NB, nch),
        in_specs=[
            pl.BlockSpec((NB, C, SSM_CONV_DIM), lambda b, c: (b, c, A16_XBC // SSM_CONV_DIM)),
            pl.BlockSpec((NB, C, SSM_WIDTH), lambda b, c: (b, c, A16_Z // SSM_WIDTH)),
            pl.BlockSpec((NB, C, LANE), lambda b, c: (b, c, A32_DT // LANE)),
            pl.BlockSpec((NB, 8, SSM_CONV_DIM), bat),
            pl.BlockSpec((8, SSM_CONV_DIM), const),
            pl.BlockSpec((1, SSM_CONV_DIM), const),
            pl.BlockSpec((1, LANE), const),
            pl.BlockSpec((1, LANE), const),
            pl.BlockSpec((1, SSM_WIDTH), const),
            pl.BlockSpec((1, SSM_WIDTH), const),
            pl.BlockSpec((NB, SSM_WIDTH, SSM_STATE), bat),
        ],
        out_specs=[
            pl.BlockSpec((NB, C, SSM_WIDTH), lambda b, c: (b, c, 0)),
            pl.BlockSpec((NB, SSM_WIDTH, SSM_STATE), bat),
            pl.BlockSpec((NB, 8, SSM_CONV_DIM), bat),
        ],
        out_shape=[jax.ShapeDtypeStruct((B, L, SSM_WIDTH), BF16),
                   jax.ShapeDtypeStruct((B, SSM_WIDTH, SSM_STATE), F32),
                   jax.ShapeDtypeStruct((B, 8, SSM_CONV_DIM), F32)],
        scratch_shapes=[pltpu.VMEM((NB, C + 8, SSM_CONV_DIM), F32), pltpu.VMEM((NB, SSM_WIDTH, SSM_STATE), F32)],
        compiler_params=_cparams(("parallel", "arbitrary")),
        name="ssd",
    )(a16, a16, a32, prefix, cw, cb, dtb, alog, dsk, sn, h0)


def _head_norm(x, gain, bd):
    ss = x * x
    hi = ss.astype(BF16)
    lo = (ss - hi.astype(F32)).astype(BF16)
    ms = _dot(hi, bd) + _dot(lo, bd)
    return x * lax.rsqrt(ms + EPS) * gain


def _head_masks(rows):
    lane = lax.broadcasted_iota(jnp.int32, (rows, ATT_WIDTH), 1)
    return [(lane >= h * ATT_HEAD_DIM) & (lane < (h + 1) * ATT_HEAD_DIM) for h in range(ATT_HEADS)]


def _stack_heads(q, masks):
    return jnp.concatenate([jnp.where(m, q, 0.0) for m in masks], axis=0).astype(BF16)


def _unstack_heads(pv, m, d, masks):
    R = pv.shape[0] // ATT_HEADS
    acc, mx, den = pv[0:R], jnp.broadcast_to(m[0:R], (R, ATT_WIDTH)), jnp.broadcast_to(d[0:R], (R, ATT_WIDTH))
    for h in range(1, ATT_HEADS):
        sl = slice(h * R, (h + 1) * R)
        acc = jnp.where(masks[h], pv[sl], acc)
        mx = jnp.where(masks[h], m[sl], mx)
        den = jnp.where(masks[h], d[sl], den)
    return acc, mx, den


def _attend(qb, kb, vb, bias4, masks):
    s = _dot_nt(_stack_heads(qb, masks), kb) + bias4
    m = jnp.max(s, axis=-1, keepdims=True)
    p = jnp.exp(s - m)
    d = jnp.sum(p, axis=-1, keepdims=True)
    return _unstack_heads(_dot(p.astype(BF16), vb), m, d, masks)


def _ld2(pair, s):
    return jnp.concatenate([pair[0][s, :], pair[1][s, :]], axis=1)


def _st2(pair, s, val):
    pair[0][s, :] = val[:, :LANE]
    pair[1][s, :] = val[:, LANE:]


def _attn_prompt_kernel(att_ref, gq_ref, gk_ref, bd_ref, y_ref, kv0_ref, kv1_ref, kv2_ref,
                        *scr, L):
    qn, kn, vv, ya, ma, za = (scr[2 * j:2 * j + 2] for j in range(6))
    g = pl.program_id(1)
    RB = 256
    QB = ATT_KEYS

    def group(gi, dil, keep, kv_ref):
        M = L // dil
        nb = M // QB
        KW = min(2 * QB, M)
        bd = bd_ref[...]

        def norm_body(i, carry):
            r0 = pl.multiple_of(i * RB, RB)
            x = att_ref[pl.ds(r0, RB), :].astype(F32)
            _st2(qn, pl.ds(r0, RB), _head_norm(x[:, :ATT_WIDTH], gq_ref[gi:gi + 1, :], bd) * ATT_SCALE)
            _st2(kn, pl.ds(r0, RB), _head_norm(x[:, ATT_WIDTH:2 * ATT_WIDTH], gk_ref[gi:gi + 1, :], bd))
            _st2(vv, pl.ds(r0, RB), x[:, 2 * ATT_WIDTH:])
            return carry

        lax.fori_loop(0, L // RB, norm_body, 0)
        kv_ref[0, :, 0:ATT_WIDTH] = _ld2(kn, slice(L - keep, L))
        kv_ref[0, :, ATT_WIDTH:2 * ATT_WIDTH] = _ld2(vv, slice(L - keep, L))

        masks = _head_masks(QB)
        iq = lax.broadcasted_iota(jnp.int32, (ATT_HEADS * QB, KW), 0) & (QB - 1)
        ik = lax.broadcasted_iota(jnp.int32, (ATT_HEADS * QB, KW), 1)

        def run_blocks(first, count, lead):
            dist = (iq - ik) if lead else (QB + iq - ik)
            bias4 = jnp.where((dist >= 0) & (dist <= ATT_KEYS), 0.0, NEG)
            unroll = max(u for u in (4, 3, 2, 1) if count % u == 0)

            def blk_body(j, carry):
                loaded = []
                for u in range(unroll):
                    it = first + j * unroll + u
                    n = it // dil
                    r = it - n * dil
                    k0 = 0 if lead else (n - 1) * QB
                    if dil == 1:
                        qs = pl.ds(pl.multiple_of(n * QB, QB), QB)
                        ks = pl.ds(pl.multiple_of(k0, QB), KW)
                    else:
                        qs = pl.ds(n * QB * dil + r, QB, stride=dil)
                        ks = pl.ds(k0 * dil + r, KW, stride=dil)
                    old = None if gi == 0 else (_ld2(ya, qs), _ld2(ma, qs), _ld2(za, qs))
                    loaded.append((qs, _ld2(qn, qs), _ld2(kn, ks).astype(BF16), _ld2(vv, ks).astype(BF16), old))
                for qs, qb, kb, vb, old in loaded:
                    acc, mx, den = _attend(qb, kb, vb, bias4, masks)
                    if gi > 0:
                        yo, mo, zo = old
                        mn = jnp.maximum(mo, mx)
                        eo = jnp.exp(mo - mn)
                        en = jnp.exp(mx - mn)
                        acc, mx, den = yo * eo + acc * en, mn, zo * eo + den * en
                    _st2(ya, qs, acc)
                    _st2(ma, qs, mx)
                    _st2(za, qs, den)
                return carry

            lax.fori_loop(0, count // unroll, blk_body, 0)

        run_blocks(0, dil, True)
        if nb > 1:
            run_blocks(dil, dil * (nb - 1), False)

    for gi in range(3):
        @pl.when(g == gi)
        def _(gi=gi):
            group(gi, ATT_DILATIONS[gi], min(ATT_WINDOWS[gi], L), (kv0_ref, kv1_ref, kv2_ref)[gi])

    @pl.when(g == 2)
    def _():
        def out_body(i, carry):
            r0 = pl.multiple_of(i * RB, RB)
            y_ref[pl.ds(r0, RB), :] = (_ld2(ya, pl.ds(r0, RB)) / _ld2(za, pl.ds(r0, RB))).astype(BF16)
            return carry
        lax.fori_loop(0, L // RB, out_body, 0)


def _attn_prompt(a16, gq, gk, bd, B, L):
    keeps = [min(w, L) for w in ATT_WINDOWS]
    const = lambda b, g: (0, 0)
    bat = lambda b, g: (b, 0, 0)
    return pl.pallas_call(
        functools.partial(_attn_prompt_kernel, L=L),
        grid=(B, 3),
        in_specs=[
            pl.BlockSpec((L, 3 * ATT_WIDTH), lambda b, g: (b, A16_ATT // (3 * ATT_WIDTH) + g)),
            pl.BlockSpec((3, ATT_WIDTH), const),
            pl.BlockSpec((3, ATT_WIDTH), const),
            pl.BlockSpec((ATT_WIDTH, ATT_WIDTH), const),
        ],
        out_specs=[pl.BlockSpec((L, ATT_WIDTH), lambda b, g: (b, 0))]
        + [pl.BlockSpec((1, k, 2 * ATT_WIDTH), bat) for k in keeps],
        out_shape=[jax.ShapeDtypeStruct((B * L, ATT_WIDTH), BF16)]
        + [jax.ShapeDtypeStruct((B, k, 2 * ATT_WIDTH), F32) for k in keeps],
        scratch_shapes=[pltpu.VMEM((L, LANE), F32) for _ in range(12)],
        compiler_params=_cparams(("parallel", "arbitrary")),
        name="attn_prompt",
    )(a16, gq, gk, bd)


def _attn_sample_kernel(a0_ref, a1_ref, a2_ref, b0_ref, b1_ref, b2_ref, gq_ref, gk_ref, bd_ref,
                        y_ref, k0_ref, k1_ref, k2_ref):
    R = SAMPLE_ROWS
    bd = bd_ref[...]
    masks = _head_masks(R)
    ya = ma = za = None
    for gi, (a_ref, b_ref, k_ref) in enumerate(((a0_ref, b0_ref, k0_ref), (a1_ref, b1_ref, k1_ref),
                                                (a2_ref, b2_ref, k2_ref))):
        dil = ATT_DILATIONS[gi]
        wb = b_ref.shape[3]
        x = a_ref[...].astype(F32)
        q = _head_norm(x[:, :ATT_WIDTH], gq_ref[gi:gi + 1, :], bd) * ATT_SCALE
        k = _head_norm(x[:, ATT_WIDTH:2 * ATT_WIDTH], gk_ref[gi:gi + 1, :], bd)
        v = x[:, 2 * ATT_WIDTH:]
        k_ref[0] = jnp.concatenate([k[:8], v[:8]], axis=1)
        k_t = b_ref[0, 0, 0:ATT_WIDTH, :].astype(BF16)
        v_t = b_ref[0, 0, ATT_WIDTH:2 * ATT_WIDTH, :].astype(BF16)
        kb, vb = k.astype(BF16), v.astype(BF16)

        def bias(dist):
            return jnp.where((dist >= 0) & (dist <= ATT_KEYS * dil) & ((dist & (dil - 1)) == 0), 0.0, NEG)

        R4 = ATT_HEADS * R
        row_b = lax.broadcasted_iota(jnp.int32, (R4, wb), 0) & (R - 1)
        row_n = lax.broadcasted_iota(jnp.int32, (R4, R), 0) & (R - 1)
        qs = _stack_heads(q, masks)
        s_b = _dot(qs, k_t) + bias(wb + row_b - lax.broadcasted_iota(jnp.int32, (R4, wb), 1))
        s_n = _dot_nt(qs, kb) + bias(row_n - lax.broadcasted_iota(jnp.int32, (R4, R), 1))
        m = jnp.maximum(jnp.max(s_b, axis=-1, keepdims=True), jnp.max(s_n, axis=-1, keepdims=True))
        p_b = jnp.exp(s_b - m)
        p_n = jnp.exp(s_n - m)
        d = jnp.sum(p_b, axis=-1, keepdims=True) + jnp.sum(p_n, axis=-1, keepdims=True)
        pv = _dot_nt(p_b.astype(BF16), v_t) + _dot(p_n.astype(BF16), vb)
        acc, mx, den = _unstack_heads(pv, m, d, masks)
        if gi == 0:
            ya, ma, za = acc, mx, den
        else:
            mn = jnp.maximum(ma, mx)
            eo = jnp.exp(ma - mn)
            en = jnp.exp(mx - mn)
            ya, za, ma = ya * eo + acc * en, za * eo + den * en, mn
    y_ref[...] = (ya / za).astype(BF16)


def _attn_sample(a16, bufs, layer, gq, gk, bd, B):
    R = SAMPLE_ROWS
    const = lambda b: (0, 0)
    bat = lambda b: (b, 0, 0)
    base = A16_ATT // (3 * ATT_WIDTH)
    return pl.pallas_call(
        _attn_sample_kernel,
        grid=(B,),
        in_specs=[pl.BlockSpec((R, 3 * ATT_WIDTH), functools.partial(lambda b, j: (b, j), j=base + g))
                  for g in range(3)]
        + [pl.BlockSpec((1, 1, 2 * ATT_WIDTH, buf.shape[3]), lambda b: (layer, b, 0, 0)) for buf in bufs]
        + [pl.BlockSpec((3, ATT_WIDTH), const), pl.BlockSpec((3, ATT_WIDTH), const),
           pl.BlockSpec((ATT_WIDTH, ATT_WIDTH), const)],
        out_specs=[pl.BlockSpec((R, ATT_WIDTH), lambda b: (b, 0))]
        + [pl.BlockSpec((1, 8, 2 * ATT_WIDTH), bat) for _ in range(3)],
        out_shape=[jax.ShapeDtypeStruct((B * R, ATT_WIDTH), BF16)]
        + [jax.ShapeDtypeStruct((B, 8, 2 * ATT_WIDTH), F32) for _ in range(3)],
        compiler_params=_cparams(("parallel",)),
        name="attn_sample",
    )(a16, a16, a16, *bufs, gq, gk, bd)


def _merge_kernel(yh_ref, ys_ref, ya_ref, gt_ref, x_ref, wh_ref, ws_ref, wa_ref, wo_ref, o_ref):
    gt = jax.nn.sigmoid(gt_ref[...].astype(F32))
    mixed = (gt[:, 0:D_MODEL] * _dot(yh_ref[...], wh_ref[...])
             + gt[:, D_MODEL:2 * D_MODEL] * _dot(ys_ref[...], ws_ref[...])
             + gt[:, 2 * D_MODEL:] * _dot(ya_ref[...], wa_ref[...]))
    o_ref[...] = x_ref[...] + _dot(mixed.astype(BF16), wo_ref[...])


def _merge(y_hg, y_ssm, y_att, a16, x, wh, ws, wa, wo, tm):
    T = x.shape[0]
    const = lambda i: (0, 0)
    rowb = lambda w: pl.BlockSpec((tm, w), lambda i: (i, 0))
    wsp = lambda w: pl.BlockSpec(w.shape, const, pipeline_mode=pl.Buffered(1))
    return pl.pallas_call(
        _merge_kernel,
        grid=(pl.cdiv(T, tm),),
        in_specs=[rowb(HG_WIDTH), rowb(SSM_WIDTH), rowb(ATT_WIDTH),
                  pl.BlockSpec((tm, 3 * D_MODEL), lambda i: (i, A16_GATE // (3 * D_MODEL))),
                  rowb(D_MODEL), wsp(wh), wsp(ws), wsp(wa), wsp(wo)],
        out_specs=rowb(D_MODEL),
        out_shape=jax.ShapeDtypeStruct((T, D_MODEL), F32),
        compiler_params=_cparams(("parallel",)),
        name="merge_out_proj",
    )(y_hg, y_ssm, y_att, a16, x, wh, ws, wa, wo)


def _ffn_kernel(x_ref, g_ref, wg_ref, wu_ref, wd_ref, *rest):
    n = (len(rest) - 1) // 2
    cast_in, o_ref, cast_out = rest[:n], rest[n], rest[n + 1:]
    x = x_ref[...]
    h = (x * lax.rsqrt(jnp.mean(x * x, axis=-1, keepdims=True) + EPS) * g_ref[...]).astype(BF16)
    a = _silu(_dot(h, wg_ref[...])) * _dot(h, wu_ref[...])
    o_ref[...] = x + _dot(a.astype(BF16), wd_ref[...])
    for src, dst in zip(cast_in, cast_out):
        dst[...] = src[...].astype(BF16)


def _ffn(x, gain, w_up, wd, tm, cast=()):
    T = x.shape[0]
    dff = wd.shape[0]
    steps = pl.cdiv(T, tm)
    const = lambda i: (0, 0)
    wsp = lambda w: pl.BlockSpec(w.shape, const, pipeline_mode=pl.Buffered(1))
    slab = [pl.BlockSpec((w.shape[0] // steps, w.shape[1]), lambda i: (i, 0)) for w in cast]
    for w in cast:
        assert w.shape[0] % (16 * steps) == 0
    outs = pl.pallas_call(
        _ffn_kernel,
        grid=(steps,),
        in_specs=[pl.BlockSpec((tm, D_MODEL), lambda i: (i, 0)), pl.BlockSpec((1, D_MODEL), const),
                  pl.BlockSpec((D_MODEL, dff), const, pipeline_mode=pl.Buffered(1)),
                  pl.BlockSpec((D_MODEL, dff), lambda i: (0, 1), pipeline_mode=pl.Buffered(1)),
                  wsp(wd)] + slab,
        out_specs=[pl.BlockSpec((tm, D_MODEL), lambda i: (i, 0))] + slab,
        out_shape=[jax.ShapeDtypeStruct((T, D_MODEL), F32)] + [jax.ShapeDtypeStruct(w.shape, BF16) for w in cast],
        compiler_params=_cparams(("parallel",)),
        name="ffn_dense",
    )(x, gain, w_up, w_up, wd, *cast)
    return outs[0], outs[1:]


def _router_kernel(x_ref, g_ref, wr_ref, h_ref, comb_ref, combt_ref):
    x = x_ref[...]
    h = x * lax.rsqrt(jnp.mean(x * x, axis=-1, keepdims=True) + EPS) * g_ref[...]
    hb = h.astype(BF16)
    h_ref[...] = hb
    h_lo = (h - hb.astype(F32)).astype(BF16)
    w = wr_ref[...]
    w_hi = w.astype(BF16)
    w_lo = (w - w_hi.astype(F32)).astype(BF16)
    logits = _dot(hb, w_hi) + _dot(hb, w_lo) + _dot(h_lo, w_hi)
    lane = lax.broadcasted_iota(jnp.int32, logits.shape, 1)
    logits = jnp.where(lane < N_EXPERTS, logits, NEG)
    v1 = jnp.max(logits, axis=-1, keepdims=True)
    i1 = jnp.min(jnp.where(logits == v1, lane, LANE), axis=-1, keepdims=True)
    rest = jnp.where(lane == i1, NEG, logits)
    v2 = jnp.max(rest, axis=-1, keepdims=True)
    i2 = jnp.min(jnp.where(rest == v2, lane, LANE), axis=-1, keepdims=True)
    e = jnp.exp(v2 - v1)
    w1 = 1.0 / (1.0 + e)
    w2 = e / (1.0 + e)
    comb = jnp.where(lane == i1, w1, 0.0) + jnp.where(lane == i2, w2, 0.0)
    comb_ref[...] = comb
    combt_ref[...] = comb.T[0:N_EXPERTS, :]


def _router(x, gain, wr, tm):
    T = x.shape[0]
    const = lambda i: (0, 0)
    return pl.pallas_call(
        _router_kernel,
        grid=(pl.cdiv(T, tm),),
        in_specs=[pl.BlockSpec((tm, D_MODEL), lambda i: (i, 0)), pl.BlockSpec((1, D_MODEL), const),
                  pl.BlockSpec((D_MODEL, LANE), const)],
        out_specs=[pl.BlockSpec((tm, D_MODEL), lambda i: (i, 0)), pl.BlockSpec((tm, LANE), lambda i: (i, 0)),
                   pl.BlockSpec((N_EXPERTS, tm), lambda i: (0, i))],
        out_shape=[jax.ShapeDtypeStruct((T, D_MODEL), BF16), jax.ShapeDtypeStruct((T, LANE), F32),
                   jax.ShapeDtypeStruct((N_EXPERTS, T), F32)],
        compiler_params=_cparams(("parallel",)),
        name="moe_router",
    )(x, gain, wr)


def _moe_kernel(h_ref, x_ref, comb_ref, combt_ref, tri_ref, wg_ref, wu_ref, wd_ref, o_ref,
                rank_scr, hs_scr, ys_scr, *, TB, SB, CH, nf):
    e = pl.program_id(1)
    f = pl.program_id(2)

    nsub = TB // SB
    subs = [slice(j * SB, (j + 1) * SB) for j in range(nsub)]

    @pl.when((e == 0) & (f == 0))
    def _():
        o_ref[...] = x_ref[...]
        routed_all = jnp.where(combt_ref[...] > 0.0, 1.0, 0.0).astype(BF16)
        for sb in subs:
            rank_scr[:, sb] = _dot(routed_all[:, sb], tri_ref[...])

    routed = combt_ref[pl.ds(e, 1), :] > 0.0
    rank = rank_scr[pl.ds(e, 1), :]
    count = None
    for sb in subs:
        cj = jnp.sum(jnp.where(routed[:, sb], 1.0, 0.0)).astype(jnp.int32)
        count = cj if count is None else jnp.maximum(count, cj)
    lane = lax.broadcasted_iota(jnp.int32, (TB, LANE), 1)
    gate = jnp.sum(jnp.where(lane == e, comb_ref[...], 0.0), axis=-1, keepdims=True)

    def body(c, carry, CH):
        slot = lax.broadcasted_iota(jnp.int32, (CH, SB), 0).astype(F32)
        if isinstance(c, int):
            rows, base = pl.ds(c * nsub * CH, nsub * CH), float(c * CH)
        else:
            rows, base = pl.ds(pl.multiple_of(c * (nsub * CH), 16), nsub * CH), (c * CH).astype(F32)
        onehots = [jnp.where((rank[:, sb] - base == slot) & routed[:, sb], 1.0, 0.0).astype(BF16) for sb in subs]

        @pl.when(f == 0)
        def _():
            hs_scr[rows, :] = jnp.concatenate(
                [_dot(p, h_ref[sb, :]) for p, sb in zip(onehots, subs)], axis=0).astype(BF16)

        hs = hs_scr[rows, :]
        a = _silu(_dot(hs, wg_ref[0])) * _dot(hs, wu_ref[0])
        y = _dot(a.astype(BF16), wd_ref[0])

        @pl.when(f == 0)
        def _():
            ys_scr[rows, :] = y

        @pl.when(f > 0)
        def _():
            ys_scr[rows, :] += y

        @pl.when(f == nf - 1)
        def _():
            ys = ys_scr[rows, :].astype(BF16)
            for j, (p, sb) in enumerate(zip(onehots, subs)):
                o_ref[sb, :] += gate[sb] * _dot_tn(p, ys[j * CH:(j + 1) * CH])

        return carry

    lo = 0
    for ch in CH[:-1]:
        @pl.when((count > lo) & (count <= ch))
        def _(ch=ch):
            body(0, 0, ch)
        lo = ch

    @pl.when(count > lo)
    def _():
        lax.fori_loop(0, (count + CH[-1] - 1) // CH[-1], functools.partial(body, CH=CH[-1]), 0)


def _moe(x, h, comb, combt, w_up, wd, TB, SB, CH, nf):
    T = x.shape[0]
    dff = wd.shape[1]
    tf = dff // nf
    slots = pl.cdiv(SB, CH[-1]) * CH[-1] * (TB // SB)
    tri = jnp.triu(jnp.ones((SB, SB), BF16), k=1)
    once = pl.Buffered(1)
    return pl.pallas_call(
        functools.partial(_moe_kernel, TB=TB, SB=SB, CH=CH, nf=nf),
        grid=(T // TB, N_EXPERTS, nf),
        in_specs=[pl.BlockSpec((TB, D_MODEL), lambda i, e, f: (i, 0), pipeline_mode=once),
                  pl.BlockSpec((TB, D_MODEL), lambda i, e, f: (i, 0), pipeline_mode=once),
                  pl.BlockSpec((TB, LANE), lambda i, e, f: (i, 0)),
                  pl.BlockSpec((N_EXPERTS, TB), lambda i, e, f: (0, i)),
                  pl.BlockSpec((SB, SB), lambda i, e, f: (0, 0), pipeline_mode=once),
                  pl.BlockSpec((1, D_MODEL, tf), lambda i, e, f: (e, 0, f)),
                  pl.BlockSpec((1, D_MODEL, tf), lambda i, e, f: (e, 0, nf + f)),
                  pl.BlockSpec((1, tf, D_MODEL), lambda i, e, f: (e, f, 0))],
        out_specs=pl.BlockSpec((TB, D_MODEL), lambda i, e, f: (i, 0)),
        out_shape=jax.ShapeDtypeStruct((T, D_MODEL), F32),
        scratch_shapes=[pltpu.VMEM((N_EXPERTS, TB), F32), pltpu.VMEM((slots, D_MODEL), BF16),
                        pltpu.VMEM((slots, D_MODEL), F32)],
        compiler_params=_cparams(("parallel", "arbitrary", "arbitrary")),
        name="moe_top2",
    )(h, x, comb, combt, tri, w_up, w_up, wd)


def _prep_w_in(w):
    wt = w.T
    hq, hf, hi, hg = (wt[j * HG_WIDTH:(j + 1) * HG_WIDTH] for j in range(4))
    z = wt[_OFF_SSM:_OFF_SSM + SSM_WIDTH]
    xbc = wt[_OFF_SSM + SSM_WIDTH:_OFF_SSM + SSM_WIDTH + SSM_CONV_DIM]
    dt = wt[_OFF_ATT - SSM_HEADS:_OFF_ATT]
    att = wt[_OFF_ATT:_OFF_GATE]
    gates = wt[_OFF_GATE:]
    w16 = jnp.concatenate([gates, xbc, hq, hi, hg, z, att], axis=0).astype(BF16)
    w32 = jnp.concatenate([hf, dt, jnp.zeros((LANE - SSM_HEADS, D_MODEL), w.dtype)], axis=0).astype(BF16)
    return w16, w32


def _pad_lanes(v, n, value=0.0):
    v = v.reshape(1, -1).astype(F32)
    return jnp.pad(v, ((0, 0), (0, n - v.shape[1])), constant_values=value)


def _run_group(x, B, L, C_h, C_s, valid, NB, tm, layer_params, states, kv_bufs):
    outs = []
    for l, P in enumerate(layer_params):
        st = states[l]
        a16, a32 = _norm_proj(x, P["mix_norm"], P["w16"], P["w32"], tm)
        a16b, a32b = a16.reshape(B, L, W16), a32.reshape(B, L, W32)
        y_hg, s_hg = _hgrn(a16b, a32b, P["lb"], P["hg_norm"], st["hgrn_t"], B, L, C_h, HG_SUB, valid, NB)
        y_ssm, s_ssm, s_conv = _ssd(a16b, a32b, st["conv"], P["conv_w"], P["conv_b"], P["dt_bias"], P["a_log"],
                                    P["d_skip"], P["ssm_norm"], st["ssm"], B, L, C_s, valid, NB)
        y_hg, y_ssm = y_hg.reshape(B * L, HG_WIDTH), y_ssm.reshape(B * L, SSM_WIDTH)
        if kv_bufs is None:
            y_att, kv0, kv1, kv2 = _attn_prompt(a16, P["gq"], P["gk"], P["bd"], B, L)
        else:
            y_att, kv0, kv1, kv2 = _attn_sample(a16, kv_bufs, l, P["gq"], P["gk"], P["bd"], B)
        x = _merge(y_hg, y_ssm, y_att, a16, x, P["w_out_hg"], P["w_out_ssm"], P["w_out_att"], P["w_o"], tm)
        if l % 2 == 0:
            F = P["ffn"]
            nxt = layer_params[l + 1]["moe"] if l + 1 < len(layer_params) else {}
            pending = [k for k in ("w_up", "wd") if k + "_f32" in nxt and k not in nxt]
            cast = [nxt[k + "_f32"].reshape(-1, nxt[k + "_f32"].shape[-1]) for k in pending]
            x, done = _ffn(x, P["ffn_norm"], F["w_up"], F["wd"], min(tm, 256), cast)
            for k, w in zip(pending, done):
                nxt[k] = w.reshape(nxt[k + "_f32"].shape)
        else:
            F = P["moe"]
            h, comb, combt = _router(x, P["ffn_norm"], F["wr"], tm)
            x = _moe(x, h, comb, combt, F["w_up"], F["wd"], min(MOE_BLOCK, x.shape[0]), min(MOE_SUB, x.shape[0]), MOE_SLOTS, 2)
        outs.append((s_hg, s_ssm, s_conv, kv0, kv1, kv2))
    return x, outs


def kernel(x_prompt, x_sample, state_hgrn, state_ssm, state_conv, cache_kv_w128, cache_kv_w512, cache_kv_w2048, w_in, mix_norm, hgrn_lb_logits, hgrn_norm, ssm_conv_w, ssm_conv_b, ssm_dt_bias, ssm_a_log, ssm_d, ssm_norm, att_q_norm, att_k_norm, w_out_hgrn, w_out_ssm, w_out_att, w_o, ffn_norm, w_ffn_up, w_ffn_down, w_router, w_moe_up, w_moe_down):
    depth = w_in.shape[0]
    Bp, Lp, _ = x_prompt.shape
    Bs, Ls, _ = x_sample.shape
    R = SAMPLE_ROWS
    for buf, w in zip((cache_kv_w128, cache_kv_w512, cache_kv_w2048), ATT_WINDOWS):
        assert buf.shape[2] == w, "sample attention assumes full window buffers"

    lb_soft = jax.nn.softmax(hgrn_lb_logits.astype(F32), axis=0)
    lb_table = jnp.cumsum(lb_soft, axis=0) - lb_soft[0]
    bd = jnp.kron(jnp.eye(ATT_HEADS, dtype=F32), jnp.full((ATT_HEAD_DIM, ATT_HEAD_DIM), 1.0 / ATT_HEAD_DIM, F32)).astype(BF16)

    layer_params = []
    for l in range(depth):
        w16, w32 = _prep_w_in(w_in[l])
        P = dict(
            w16=w16, w32=w32,
            mix_norm=mix_norm[l].reshape(1, -1), ffn_norm=ffn_norm[l].reshape(1, -1),
            lb=lb_table[l].reshape(1, -1), hg_norm=hgrn_norm[l].reshape(1, -1),
            conv_w=jnp.pad(ssm_conv_w[l], ((0, 8 - SSM_CONV), (0, 0))), conv_b=ssm_conv_b[l].reshape(1, -1),
            dt_bias=_pad_lanes(ssm_dt_bias[l], LANE), a_log=_pad_lanes(ssm_a_log[l], LANE),
            d_skip=jnp.repeat(ssm_d[l].astype(F32), SSM_HEAD_DIM).reshape(1, -1),
            ssm_norm=ssm_norm[l].reshape(1, -1),
            gq=jnp.tile(att_q_norm[l], (1, ATT_HEADS)), gk=jnp.tile(att_k_norm[l], (1, ATT_HEADS)), bd=bd,
            w_out_hg=w_out_hgrn[l].astype(BF16), w_out_ssm=w_out_ssm[l].astype(BF16),
            w_out_att=w_out_att[l].astype(BF16), w_o=w_o[l].astype(BF16),
        )
        if l % 2 == 0:
            P["ffn"] = dict(w_up=w_ffn_up[l // 2].astype(BF16), wd=w_ffn_down[l // 2].astype(BF16))
        else:
            P["moe"] = dict(wr=jnp.pad(w_router[l // 2].astype(F32), ((0, 0), (0, LANE - N_EXPERTS))),
                            w_up_f32=w_moe_up[l // 2], wd_f32=w_moe_down[l // 2])
        layer_params.append(P)

    zero_p = dict(hgrn_t=jnp.zeros((Bp, HG_WIDTH, HG_DIM), F32), ssm=jnp.zeros((Bp, SSM_WIDTH, SSM_STATE), F32),
                  conv=jnp.zeros((Bp, 8, SSM_CONV_DIM), F32))
    xp, outs_p = _run_group(x_prompt.reshape(Bp * Lp, D_MODEL), Bp, Lp, 64, 64, 64, 4, 512,
                            layer_params, [zero_p] * depth, None)

    xs = jnp.pad(x_sample, ((0, 0), (0, R - Ls), (0, 0))).reshape(Bs * R, D_MODEL)
    st_s = []
    for l in range(depth):
        st_s.append(dict(
            hgrn_t=jnp.swapaxes(state_hgrn[l], -1, -2).reshape(Bs, HG_WIDTH, HG_DIM),
            ssm=state_ssm[l].reshape(Bs, SSM_WIDTH, SSM_STATE),
            conv=jnp.pad(state_conv[l], ((0, 0), (8 - (SSM_CONV - 1), 0), (0, 0)))))
    kv_s = [jnp.transpose(c, (0, 1, 3, 4, 5, 2)).reshape(depth, Bs, 2 * ATT_WIDTH, c.shape[2])
            for c in (cache_kv_w128, cache_kv_w512, cache_kv_w2048)]
    xs, outs_s = _run_group(xs, Bs, R, R, R, Ls, 4, Bs * R, layer_params, st_s, kv_s)

    def pack(outs, B, nkv):
        s_hg = jnp.stack([jnp.swapaxes(o[0].reshape(B, HG_HEADS, HG_DIM, HG_DIM), -1, -2) for o in outs])
        s_ssm = jnp.stack([o[1].reshape(B, SSM_HEADS, SSM_HEAD_DIM, SSM_STATE) for o in outs])
        s_conv = jnp.stack([o[2][:, 8 - (SSM_CONV - 1):] for o in outs])
        kvs = []
        for j in range(3):
            n = nkv[j]
            kvs.append(jnp.stack([o[3 + j][:, :n].reshape(B, n, 2, ATT_HEADS, ATT_HEAD_DIM) for o in outs]))
        return (s_hg, s_ssm, s_conv, *kvs)

    keeps = [min(w, Lp) for w in ATT_WINDOWS]
    y_prompt = xp.reshape(Bp, Lp, D_MODEL)
    y_sample = xs.reshape(Bs, R, D_MODEL)[:, :Ls]
    return (y_prompt, y_sample, *pack(outs_p, Bp, keeps), *pack(outs_s, Bs, [Ls] * 3))
```

```python
import functools

import numpy as np
import jax
import jax.numpy as jnp
from jax import lax
from jax.experimental import pallas as pl
from jax.experimental.pallas import tpu as pltpu

F32 = jnp.float32
BF16 = jnp.bfloat16

D_MODEL = 1024
HG_HEADS, HG_DIM = 4, 128
HG_WIDTH = HG_HEADS * HG_DIM
SSM_HEADS, SSM_HEAD_DIM, SSM_GROUPS, SSM_STATE, SSM_CONV = 8, 64, 2, 128, 4
SSM_WIDTH = SSM_HEADS * SSM_HEAD_DIM
SSM_CONV_DIM = SSM_WIDTH + 2 * SSM_GROUPS * SSM_STATE
ATT_WINDOWS = (128, 512, 2048)
ATT_DILATIONS = (1, 4, 16)
ATT_ORDER = (2, 1, 0)
ATT_HEADS, ATT_HEAD_DIM = 4, 64
ATT_WIDTH = ATT_HEADS * ATT_HEAD_DIM
ATT_KEYS = 128
ATT_SCALE = ATT_HEAD_DIM ** -0.5
N_EXPERTS = 8
EPS = 1e-6
NEG = -1e30

_OFF_SSM = 4 * HG_WIDTH
_OFF_ATT = _OFF_SSM + SSM_WIDTH + SSM_CONV_DIM + SSM_HEADS
_OFF_GATE = _OFF_ATT + 3 * 3 * ATT_WIDTH

A16_GATE, A16_XBC, A16_HQ, A16_HI, A16_HG, A16_Z, A16_ATT = 0, 3072, 4096, 4608, 5120, 5632, 6144
W16 = 8448
A32_HF, A32_DT = 0, 512
W32 = 640

LANE = 128
SAMPLE_ROWS = 16
HG_SUB = 8
MOE_BLOCK = 1024
MOE_SUB = 512
MOE_SLOTS = (128, 144, 160)
VMEM_LIMIT = 56 * 2 ** 20


def _cparams(sem):
    return pltpu.CompilerParams(dimension_semantics=sem, vmem_limit_bytes=VMEM_LIMIT)


def _dot(a, b):
    return jnp.dot(a, b, preferred_element_type=F32)


def _dot_nt(a, b):
    return lax.dot_general(a, b, (((1,), (1,)), ((), ())), preferred_element_type=F32)


def _dot_tn(a, b):
    return lax.dot_general(a, b, (((0,), (0,)), ((), ())), preferred_element_type=F32)


def _split3(x):
    hi = x.astype(BF16)
    r = x - hi.astype(F32)
    mid = r.astype(BF16)
    lo = (r - mid.astype(F32)).astype(BF16)
    return hi, mid, lo


def _cumsum_rows(x, tril_bf16):
    hi, mid, lo = _split3(x)
    return _dot(tril_bf16, hi) + _dot(tril_bf16, mid) + _dot(tril_bf16, lo)


def _sigmoid(x):
    return 0.5 * jnp.tanh(0.5 * x) + 0.5


def _silu(x):
    return x * _sigmoid(x)


def _tril_bf16(n):
    r = lax.broadcasted_iota(jnp.int32, (n, n), 0)
    c = lax.broadcasted_iota(jnp.int32, (n, n), 1)
    return jnp.where(c <= r, 1.0, 0.0).astype(BF16)


def _norm_proj_kernel(x_ref, g_ref, w16_ref, w32_ref, o16_ref, o32_ref, *, chunk):
    x = x_ref[...]
    h = (x * lax.rsqrt(jnp.mean(x * x, axis=-1, keepdims=True) + EPS) * g_ref[...]).astype(BF16)
    for c0 in range(0, W16, chunk):
        o16_ref[:, c0:c0 + chunk] = _dot_nt(h, w16_ref[c0:c0 + chunk, :]).astype(BF16)
    o32_ref[...] = _dot_nt(h, w32_ref[...])


def _norm_proj(x, gain, w16, w32, tm):
    T = x.shape[0]
    const = lambda i: (0, 0)
    return pl.pallas_call(
        functools.partial(_norm_proj_kernel, chunk=768),
        grid=(pl.cdiv(T, tm),),
        in_specs=[
            pl.BlockSpec((tm, D_MODEL), lambda i: (i, 0)),
            pl.BlockSpec((1, D_MODEL), const),
            pl.BlockSpec((W16, D_MODEL), const, pipeline_mode=pl.Buffered(1)),
            pl.BlockSpec((W32, D_MODEL), const, pipeline_mode=pl.Buffered(1)),
        ],
        out_specs=[
            pl.BlockSpec((tm, W16), lambda i: (i, 0)),
            pl.BlockSpec((tm, W32), lambda i: (i, 0)),
        ],
        out_shape=[jax.ShapeDtypeStruct((T, W16), BF16), jax.ShapeDtypeStruct((T, W32), F32)],
        compiler_params=_cparams(("parallel",)),
        name="norm_proj",
    )(x, gain, w16, w32)


def _hgrn_tables(C, sub):
    t = np.arange(C)[:, None]
    u = np.arange(C)[None, :]
    blocks = [u <= t]
    levels = []
    l = sub
    while l < C:
        upper = (t % (2 * l)) >= l
        b = (t // (2 * l)) * (2 * l) + l - 1
        blocks.append(np.where(upper, (u > b) & (u <= t), (u > t) & (u <= b)))
        levels.append(l)
        l *= 2
    for s in range(sub):
        r = (t // sub) * sub + s
        blocks.append((u > r) & (u <= t))
    sel = [u == (t // sub) * sub + s for s in range(sub)]
    d = np.concatenate(blocks, 0).astype(np.float32)
    sl = np.concatenate(sel, 0).astype(np.float32)
    table = np.block([[d, d, d, np.zeros_like(d)], [np.zeros((sl.shape[0], 3 * C), np.float32), sl]])
    return jnp.asarray(table, BF16), tuple(levels)


def _hgrn_kernel(q_ref, f_ref, i_ref, g_ref, lb_ref, gn_ref, s0_ref, tab_ref, y_ref, sout_ref, st_scr,
                 *, NB, nchunks, **kw):
    ci = pl.program_id(1)

    @pl.when(ci == 0)
    def _():
        st_scr[...] = s0_ref[...]

    for j in range(NB):
        _hgrn_chunk(q_ref.at[j], f_ref.at[j], i_ref.at[j], g_ref.at[j], lb_ref, gn_ref, tab_ref,
                    y_ref.at[j], st_scr.at[j], **kw)

    @pl.when(ci == nchunks - 1)
    def _():
        sout_ref[...] = st_scr[...]


def _hgrn_chunk(q_ref, f_ref, i_ref, g_ref, lb_ref, gn_ref, tab_ref, y_ref, st_scr, *, C, sub, levels, valid):
    lb = lb_ref[...]
    fgate = lb + (1.0 - lb) * jax.nn.sigmoid(f_ref[...])
    logf = jnp.log(fgate)
    kk = 1.0 - fgate
    rows = lax.broadcasted_iota(jnp.int32, (C, HG_WIDTH), 0)
    if valid < C:
        logf = jnp.where(rows < valid, logf, 0.0)
        kk = jnp.where(rows < valid, kk, 0.0)
    qa = _silu(q_ref[...].astype(F32))
    va = i_ref[...].astype(F32)
    ga = g_ref[...].astype(F32)

    ed = _dot(tab_ref[...], jnp.concatenate([*_split3(logf), kk.astype(BF16)], axis=0))
    nblk = 1 + len(levels) + sub
    blk = lambda j: ed[j * C:(j + 1) * C]
    G = blk(0)
    ksel = ed[nblk * C:]

    row_c = lax.broadcasted_iota(jnp.int32, (C, C), 0)
    lane_c = lax.broadcasted_iota(jnp.int32, (C, C), 1)
    off_c = lane_c - (row_c // sub) * sub

    A = [jnp.zeros((C, C), F32) for _ in range(HG_HEADS)]
    for s in range(sub):
        prod = qa * ksel[s * C:(s + 1) * C] * jnp.exp(blk(1 + len(levels) + s))
        for h in range(HG_HEADS):
            col = jnp.sum(prod[:, h * HG_DIM:(h + 1) * HG_DIM], axis=-1, keepdims=True)
            A[h] = jnp.where(off_c == s, col, A[h])
    A = [jnp.where(lane_c <= row_c, a, 0.0) for a in A]

    for li, l in enumerate(levels):
        upper = (rows % (2 * l)) >= l
        scaled = jnp.where(upper, qa, kk) * jnp.exp(blk(1 + li))
        qt = jnp.where(upper, scaled, 0.0).astype(BF16)
        kt = jnp.where(upper, 0.0, scaled).astype(BF16)
        same = (row_c // (2 * l)) == (lane_c // (2 * l))
        for h in range(HG_HEADS):
            sl = slice(h * HG_DIM, (h + 1) * HG_DIM)
            a_l = _dot_nt(qt[:, sl], kt[:, sl])
            A[h] = A[h] + (jnp.where(same, a_l, 0.0) if 2 * l < C else a_l)

    outs = []
    for h in range(HG_HEADS):
        sl = slice(h * HG_DIM, (h + 1) * HG_DIM)
        Gh, qh, kh = G[:, sl], qa[:, sl], kk[:, sl]
        vb = va[:, sl].astype(BF16)
        st = st_scr[sl, :]
        o = _dot(A[h].astype(BF16), vb) + _dot_nt((qh * jnp.exp(Gh)).astype(BF16), st.astype(BF16))
        g_last = Gh[C - 1:C]
        kdec = kh * jnp.exp(g_last - Gh)
        st_scr[sl, :] = jnp.exp(g_last) * st + _dot_tn(vb, kdec.astype(BF16))
        on = o * lax.rsqrt(jnp.mean(o * o, axis=-1, keepdims=True) + EPS) * gn_ref[...]
        outs.append(on * _silu(ga[:, sl]))
    y_ref[...] = jnp.concatenate(outs, axis=1).astype(BF16)


def _hgrn(a16, a32, lb, gn, s0t, B, L, C, sub, valid, NB):
    nch = L // C
    table, levels = _hgrn_tables(C, sub)
    const = lambda b, c: (0, 0)
    blk = lambda col: pl.BlockSpec((NB, C, HG_WIDTH), lambda b, c: (b, c, col))
    return pl.pallas_call(
        functools.partial(_hgrn_kernel, NB=NB, C=C, sub=sub, levels=levels, valid=valid, nchunks=nch),
        grid=(B // NB, nch),
        in_specs=[
            blk(A16_HQ // HG_WIDTH), blk(A32_HF // HG_WIDTH), blk(A16_HI // HG_WIDTH), blk(A16_HG // HG_WIDTH),
            pl.BlockSpec((1, HG_WIDTH), const),
            pl.BlockSpec((1, HG_DIM), const),
            pl.BlockSpec((NB, HG_WIDTH, HG_DIM), lambda b, c: (b, 0, 0)),
            pl.BlockSpec(table.shape, const),
        ],
        out_specs=[
            pl.BlockSpec((NB, C, HG_WIDTH), lambda b, c: (b, c, 0)),
            pl.BlockSpec((NB, HG_WIDTH, HG_DIM), lambda b, c: (b, 0, 0)),
        ],
        out_shape=[jax.ShapeDtypeStruct((B, L, HG_WIDTH), BF16),
                   jax.ShapeDtypeStruct((B, HG_WIDTH, HG_DIM), F32)],
        scratch_shapes=[pltpu.VMEM((NB, HG_WIDTH, HG_DIM), F32)],
        compiler_params=_cparams(("parallel", "arbitrary")),
        name="hgrn2",
    )(a16, a32, a16, a16, lb, gn, s0t, table)


def _ssd_kernel(xbc_ref, z_ref, dt_ref, pre_ref, cw_ref, cb_ref, dtb_ref, alog_ref, dsk_ref, sn_ref,
                h0_ref, y_ref, hout_ref, cout_ref, ubuf, hs, *, NB, nchunks, **kw):
    ci = pl.program_id(1)

    @pl.when(ci == 0)
    def _():
        ubuf[:, 0:8, :] = pre_ref[...]
        hs[...] = h0_ref[...]

    for j in range(NB):
        _ssd_chunk(xbc_ref.at[j], z_ref.at[j], dt_ref.at[j], cw_ref, cb_ref, dtb_ref, alog_ref, dsk_ref, sn_ref,
                   y_ref.at[j], cout_ref.at[j], ubuf.at[j], hs.at[j], **kw)

    @pl.when(ci == nchunks - 1)
    def _():
        hout_ref[...] = hs[...]


def _ssd_chunk(xbc_ref, z_ref, dt_ref, cw_ref, cb_ref, dtb_ref, alog_ref, dsk_ref, sn_ref,
               y_ref, cout_ref, ubuf, hs, *, C, valid):
    ubuf[8:8 + C, :] = xbc_ref[...].astype(F32)
    acc = cb_ref[...] + ubuf[5:5 + C, :] * cw_ref[0:1, :]
    for j in range(1, SSM_CONV):
        acc = acc + ubuf[5 + j:5 + j + C, :] * cw_ref[j:j + 1, :]
    xc = _silu(acc)
    tail = ubuf[valid:valid + 8, :]
    cout_ref[...] = tail
    ubuf[0:8, :] = tail

    x = dt_ref[...] + dtb_ref[...]
    dt = jnp.maximum(x, 0.0) + jnp.log1p(jnp.exp(-jnp.abs(x)))
    if valid < C:
        rows = lax.broadcasted_iota(jnp.int32, (C, LANE), 0)
        dt = jnp.where(rows < valid, dt, 0.0)
    a = -jnp.exp(alog_ref[...])
    cum = _cumsum_rows(dt * a, _tril_bf16(C))
    cum_t = cum.T
    dt_t = dt.T

    ri = lax.broadcasted_iota(jnp.int32, (C, C), 0)
    cj = lax.broadcasted_iota(jnp.int32, (C, C), 1)
    tril = cj <= ri
    lane = lax.broadcasted_iota(jnp.int32, (C, LANE), 1)
    srow = lax.broadcasted_iota(jnp.int32, (LANE, LANE), 0)
    bm = [xc[:, SSM_WIDTH + g * SSM_STATE:SSM_WIDTH + (g + 1) * SSM_STATE].astype(BF16) for g in range(SSM_GROUPS)]
    off_c = SSM_WIDTH + SSM_GROUPS * SSM_STATE
    cm = [xc[:, off_c + g * SSM_STATE:off_c + (g + 1) * SSM_STATE].astype(BF16) for g in range(SSM_GROUPS)]
    cb = [_dot_nt(cm[g], bm[g]) for g in range(SSM_GROUPS)]

    ys = []
    for p in range(SSM_HEADS // 2):
        g = (2 * p) // (SSM_HEADS // SSM_GROUPS)
        xp = xc[:, p * LANE:(p + 1) * LANE]
        yp = None
        colw = []
        for k, hd in enumerate((2 * p, 2 * p + 1)):
            colb = jnp.broadcast_to(cum[:, hd:hd + 1], (C, LANE))
            rowb = jnp.broadcast_to(cum_t[hd:hd + 1, :], (C, C))
            dtr = jnp.broadcast_to(dt_t[hd:hd + 1, :], (C, C))
            w = cb[g] * jnp.exp(jnp.where(tril, colb[:, :C] - rowb, NEG)) * dtr
            half = (lane < SSM_HEAD_DIM) if k == 0 else (lane >= SSM_HEAD_DIM)
            t = _dot(w.astype(BF16), jnp.where(half, xp, 0.0).astype(BF16))
            yp = t if yp is None else yp + t
            colw.append((colb, jnp.broadcast_to(dt[:, hd:hd + 1], (C, LANE))))
        first = lane < SSM_HEAD_DIM
        cum_e = jnp.where(first, colw[0][0], colw[1][0])
        dt_e = jnp.where(first, colw[0][1], colw[1][1])
        hp = hs[p * LANE:(p + 1) * LANE, :]
        yp = yp + jnp.exp(cum_e) * _dot_nt(cm[g], hp.astype(BF16))
        c_last = cum_e[C - 1:C, :]
        w_in = jnp.exp(c_last - cum_e) * dt_e
        dec = jnp.where(srow < SSM_HEAD_DIM,
                        jnp.broadcast_to(cum[C - 1:C, 2 * p:2 * p + 1], (LANE, LANE)),
                        jnp.broadcast_to(cum[C - 1:C, 2 * p + 1:2 * p + 2], (LANE, LANE)))
        hs[p * LANE:(p + 1) * LANE, :] = jnp.exp(dec) * hp + _dot_tn((xp * w_in).astype(BF16), bm[g])
        ys.append(yp)
    y = jnp.concatenate(ys, axis=1) + dsk_ref[...] * xc[:, :SSM_WIDTH]
    yz = y * _silu(z_ref[...].astype(F32))
    y_ref[...] = (yz * lax.rsqrt(jnp.mean(yz * yz, axis=-1, keepdims=True) + EPS) * sn_ref[...]).astype(BF16)


def _ssd(a16, a32, prefix, cw, cb, dtb, alog, dsk, sn, h0, B, L, C, valid, NB):
    nch = L // C
    const = lambda b, c: (0, 0)
    bat = lambda b, c: (b, 0, 0)
    return pl.pallas_call(
        functools.partial(_ssd_kernel, NB=NB, C=C, valid=valid, nchunks=nch),
        grid=(B // NB, nch),
        in_specs=[
            pl.BlockSpec((NB, C, SSM_CONV_DIM), lambda b, c: (b, c, A16_XBC // SSM_CONV_DIM)),
            pl.BlockSpec((NB, C, SSM_WIDTH), lambda b, c: (b, c, A16_Z // SSM_WIDTH)),
            pl.BlockSpec((NB, C, LANE), lambda b, c: (b, c, A32_DT // LANE)),
            pl.BlockSpec((NB, 8, SSM_CONV_DIM), bat),
            pl.BlockSpec((8, SSM_CONV_DIM), const),
            pl.BlockSpec((1, SSM_CONV_DIM), const),
            pl.BlockSpec((1, LANE), const),
            pl.BlockSpec((1, LANE), const),
            pl.BlockSpec((1, SSM_WIDTH), const),
            pl.BlockSpec((1, SSM_WIDTH), const),
            pl.BlockSpec((NB, SSM_WIDTH, SSM_STATE), bat),
        ],
        out_specs=[
            pl.BlockSpec((NB, C, SSM_WIDTH), lambda b, c: (b, c, 0)),
            pl.BlockSpec((NB, SSM_WIDTH, SSM_STATE), bat),
            pl.BlockSpec((NB, 8, SSM_CONV_DIM), bat),
        ],
        out_shape=[jax.ShapeDtypeStruct((B, L, SSM_WIDTH), BF16),
                   jax.ShapeDtypeStruct((B, SSM_WIDTH, SSM_STATE), F32),
                   jax.ShapeDtypeStruct((B, 8, SSM_CONV_DIM), F32)],
        scratch_shapes=[pltpu.VMEM((NB, C + 8, SSM_CONV_DIM), F32), pltpu.VMEM((NB, SSM_WIDTH, SSM_STATE), F32)],
        compiler_params=_cparams(("parallel", "arbitrary")),
        name="ssd",
    )(a16, a16, a32, prefix, cw, cb, dtb, alog, dsk, sn, h0)


def _head_norm(x, gain, bd):
    ms = _dot((x * x).astype(BF16), bd)
    return x * (lax.rsqrt(ms + EPS) * gain)


def _head_masks(rows):
    lane = lax.broadcasted_iota(jnp.int32, (rows, ATT_WIDTH), 1)
    return [(lane >= h * ATT_HEAD_DIM) & (lane < (h + 1) * ATT_HEAD_DIM) for h in range(ATT_HEADS)]


def _stack_heads(q, masks):
    return jnp.concatenate([jnp.where(m, q, 0.0) for m in masks], axis=0).astype(BF16)


def _unstack_heads(pv, m, d, masks):
    R = pv.shape[0] // ATT_HEADS
    acc, mx, den = pv[0:R], jnp.broadcast_to(m[0:R], (R, ATT_WIDTH)), jnp.broadcast_to(d[0:R], (R, ATT_WIDTH))
    for h in range(1, ATT_HEADS):
        sl = slice(h * R, (h + 1) * R)
        acc = jnp.where(masks[h], pv[sl], acc)
        mx = jnp.where(masks[h], m[sl], mx)
        den = jnp.where(masks[h], d[sl], den)
    return acc, mx, den


def _attend(qb, kb, vb, bias4, masks):
    s = _dot_nt(_stack_heads(qb, masks), kb) + bias4
    m = jnp.max(s, axis=-1, keepdims=True)
    p = jnp.exp(s - m)
    d = jnp.sum(p, axis=-1, keepdims=True)
    return _unstack_heads(_dot(p.astype(BF16), vb), m, d, masks)


def _ld2(pair, s):
    return jnp.concatenate([pair[0][s, :], pair[1][s, :]], axis=1)


def _st2(pair, s, val):
    pair[0][s, :] = val[:, :LANE]
    pair[1][s, :] = val[:, LANE:]


def _attn_prompt_kernel(att_ref, gq_ref, gk_ref, bd_ref, y_ref, kv0_ref, kv1_ref, kv2_ref,
                        *scr, L):
    qn, kn, vv, ya, ma, za = (scr[2 * j:2 * j + 2] for j in range(6))
    g = pl.program_id(1)
    RB = 256
    QB = ATT_KEYS

    def group(step, gi, dil, keep, kv_ref):
        M = L // dil
        nb = M // QB
        KW = min(2 * QB, M)
        bd = bd_ref[...]

        def norm_body(i, carry):
            r0 = pl.multiple_of(i * RB, RB)
            x = att_ref[pl.ds(r0, RB), :].astype(F32)
            _st2(qn, pl.ds(r0, RB), _head_norm(x[:, :ATT_WIDTH], gq_ref[gi:gi + 1, :] * ATT_SCALE, bd))
            _st2(kn, pl.ds(r0, RB), _head_norm(x[:, ATT_WIDTH:2 * ATT_WIDTH], gk_ref[gi:gi + 1, :], bd))
            _st2(vv, pl.ds(r0, RB), x[:, 2 * ATT_WIDTH:])
            return carry

        lax.fori_loop(0, L // RB, norm_body, 0)
        kv_ref[0, :, 0:ATT_WIDTH] = _ld2(kn, slice(L - keep, L))
        kv_ref[0, :, ATT_WIDTH:2 * ATT_WIDTH] = _ld2(vv, slice(L - keep, L))

        masks = _head_masks(QB)
        iq = lax.broadcasted_iota(jnp.int32, (ATT_HEADS * QB, KW), 0) & (QB - 1)
        ik = lax.broadcasted_iota(jnp.int32, (ATT_HEADS * QB, KW), 1)

        def run_blocks(first, count, lead):
            dist = (iq - ik) if lead else (QB + iq - ik)
            bias4 = jnp.where((dist >= 0) & (dist <= ATT_KEYS), 0.0, NEG)
            unroll = max(u for u in (4, 3, 2, 1) if count % u == 0)

            def blk_body(j, carry):
                loaded = []
                for u in range(unroll):
                    it = first + j * unroll + u
                    n = it // dil
                    r = it - n * dil
                    k0 = 0 if lead else (n - 1) * QB
                    if dil == 1:
                        qs = pl.ds(pl.multiple_of(n * QB, QB), QB)
                        ks = pl.ds(pl.multiple_of(k0, QB), KW)
                    else:
                        qs = pl.ds(n * QB * dil + r, QB, stride=dil)
                        ks = pl.ds(k0 * dil + r, KW, stride=dil)
                    old = None if step == 0 else (_ld2(ya, qs), _ld2(ma, qs), _ld2(za, qs))
                    loaded.append((qs, _ld2(qn, qs), _ld2(kn, ks).astype(BF16), _ld2(vv, ks).astype(BF16), old))
                for qs, qb, kb, vb, old in loaded:
                    acc, mx, den = _attend(qb, kb, vb, bias4, masks)
                    if step > 0:
                        yo, mo, zo = old
                        mn = jnp.maximum(mo, mx)
                        eo = jnp.exp(mo - mn)
                        en = jnp.exp(mx - mn)
                        acc, mx, den = yo * eo + acc * en, mn, zo * eo + den * en
                    _st2(ya, qs, acc)
                    _st2(ma, qs, mx)
                    _st2(za, qs, den)
                return carry

            lax.fori_loop(0, count // unroll, blk_body, 0)

        run_blocks(0, dil, True)
        if nb > 1:
            run_blocks(dil, dil * (nb - 1), False)

    for step, gi in enumerate(ATT_ORDER):
        @pl.when(g == step)
        def _(step=step, gi=gi):
            group(step, gi, ATT_DILATIONS[gi], min(ATT_WINDOWS[gi], L), (kv0_ref, kv1_ref, kv2_ref)[gi])

    @pl.when(g == 2)
    def _():
        def out_body(i, carry):
            r0 = pl.multiple_of(i * RB, RB)
            y_ref[pl.ds(r0, RB), :] = (_ld2(ya, pl.ds(r0, RB)) / _ld2(za, pl.ds(r0, RB))).astype(BF16)
            return carry
        lax.fori_loop(0, L // RB, out_body, 0)


def _attn_prompt(a16, gq, gk, bd, B, L):
    keeps = [min(w, L) for w in ATT_WINDOWS]
    const = lambda b, g: (0, 0)
    bat = lambda b, g: (b, 0, 0)
    return pl.pallas_call(
        functools.partial(_attn_prompt_kernel, L=L),
        grid=(B, 3),
        in_specs=[
            pl.BlockSpec((L, 3 * ATT_WIDTH), lambda b, g: (b, A16_ATT // (3 * ATT_WIDTH) + 2 - g)),
            pl.BlockSpec((3, ATT_WIDTH), const),
            pl.BlockSpec((3, ATT_WIDTH), const),
            pl.BlockSpec((ATT_WIDTH, ATT_WIDTH), const),
        ],
        out_specs=[pl.BlockSpec((L, ATT_WIDTH), lambda b, g: (b, 0))]
        + [pl.BlockSpec((1, k, 2 * ATT_WIDTH), bat) for k in keeps],
        out_shape=[jax.ShapeDtypeStruct((B * L, ATT_WIDTH), BF16)]
        + [jax.ShapeDtypeStruct((B, k, 2 * ATT_WIDTH), F32) for k in keeps],
        scratch_shapes=[pltpu.VMEM((L, LANE), F32) for _ in range(12)],
        compiler_params=_cparams(("parallel", "arbitrary")),
        name="attn_prompt",
    )(a16, gq, gk, bd)


def _attn_sample_kernel(a0_ref, a1_ref, a2_ref, b0_ref, b1_ref, b2_ref, gq_ref, gk_ref, bd_ref,
                        y_ref, k0_ref, k1_ref, k2_ref):
    R = SAMPLE_ROWS
    bd = bd_ref[...]
    masks = _head_masks(R)
    ya = ma = za = None
    for gi, (a_ref, b_ref, k_ref) in enumerate(((a0_ref, b0_ref, k0_ref), (a1_ref, b1_ref, k1_ref),
                                                (a2_ref, b2_ref, k2_ref))):
        dil = ATT_DILATIONS[gi]
        wb = b_ref.shape[3]
        x = a_ref[...].astype(F32)
        q = _head_norm(x[:, :ATT_WIDTH], gq_ref[gi:gi + 1, :] * ATT_SCALE, bd)
        k = _head_norm(x[:, ATT_WIDTH:2 * ATT_WIDTH], gk_ref[gi:gi + 1, :], bd)
        v = x[:, 2 * ATT_WIDTH:]
        k_ref[0] = jnp.concatenate([k[:8], v[:8]], axis=1)
        k_t = b_ref[0, 0, 0:ATT_WIDTH, :].astype(BF16)
        v_t = b_ref[0, 0, ATT_WIDTH:2 * ATT_WIDTH, :].astype(BF16)
        kb, vb = k.astype(BF16), v.astype(BF16)

        def bias(dist):
            return jnp.where((dist >= 0) & (dist <= ATT_KEYS * dil) & ((dist & (dil - 1)) == 0), 0.0, NEG)

        R4 = ATT_HEADS * R
        row_b = lax.broadcasted_iota(jnp.int32, (R4, wb), 0) & (R - 1)
        row_n = lax.broadcasted_iota(jnp.int32, (R4, R), 0) & (R - 1)
        qs = _stack_heads(q, masks)
        s_b = _dot(qs, k_t) + bias(wb + row_b - lax.broadcasted_iota(jnp.int32, (R4, wb), 1))
        s_n = _dot_nt(qs, kb) + bias(row_n - lax.broadcasted_iota(jnp.int32, (R4, R), 1))
        m = jnp.maximum(jnp.max(s_b, axis=-1, keepdims=True), jnp.max(s_n, axis=-1, keepdims=True))
        p_b = jnp.exp(s_b - m)
        p_n = jnp.exp(s_n - m)
        d = jnp.sum(p_b, axis=-1, keepdims=True) + jnp.sum(p_n, axis=-1, keepdims=True)
        pv = _dot_nt(p_b.astype(BF16), v_t) + _dot(p_n.astype(BF16), vb)
        acc, mx, den = _unstack_heads(pv, m, d, masks)
        if gi == 0:
            ya, ma, za = acc, mx, den
        else:
            mn = jnp.maximum(ma, mx)
            eo = jnp.exp(ma - mn)
            en = jnp.exp(mx - mn)
            ya, za, ma = ya * eo + acc * en, za * eo + den * en, mn
    y_ref[...] = (ya / za).astype(BF16)


def _attn_sample(a16, bufs, layer, gq, gk, bd, B):
    R = SAMPLE_ROWS
    const = lambda b: (0, 0)
    bat = lambda b: (b, 0, 0)
    base = A16_ATT // (3 * ATT_WIDTH)
    return pl.pallas_call(
        _attn_sample_kernel,
        grid=(B,),
        in_specs=[pl.BlockSpec((R, 3 * ATT_WIDTH), functools.partial(lambda b, j: (b, j), j=base + g))
                  for g in range(3)]
        + [pl.BlockSpec((1, 1, 2 * ATT_WIDTH, buf.shape[3]), lambda b: (layer, b, 0, 0)) for buf in bufs]
        + [pl.BlockSpec((3, ATT_WIDTH), const), pl.BlockSpec((3, ATT_WIDTH), const),
           pl.BlockSpec((ATT_WIDTH, ATT_WIDTH), const)],
        out_specs=[pl.BlockSpec((R, ATT_WIDTH), lambda b: (b, 0))]
        + [pl.BlockSpec((1, 8, 2 * ATT_WIDTH), bat) for _ in range(3)],
        out_shape=[jax.ShapeDtypeStruct((B * R, ATT_WIDTH), BF16)]
        + [jax.ShapeDtypeStruct((B, 8, 2 * ATT_WIDTH), F32) for _ in range(3)],
        compiler_params=_cparams(("parallel",)),
        name="attn_sample",
    )(a16, a16, a16, *bufs, gq, gk, bd)


def _merge_kernel(yh_ref, ys_ref, ya_ref, gt_ref, x_ref, wh_ref, ws_ref, wa_ref, wo_ref, o_ref):
    gt = _sigmoid(gt_ref[...].astype(F32))
    mixed = (gt[:, 0:D_MODEL] * _dot(yh_ref[...], wh_ref[...])
             + gt[:, D_MODEL:2 * D_MODEL] * _dot(ys_ref[...], ws_ref[...])
             + gt[:, 2 * D_MODEL:] * _dot(ya_ref[...], wa_ref[...]))
    o_ref[...] = x_ref[...] + _dot(mixed.astype(BF16), wo_ref[...])


def _merge(y_hg, y_ssm, y_att, a16, x, wh, ws, wa, wo, tm):
    T = x.shape[0]
    const = lambda i: (0, 0)
    rowb = lambda w: pl.BlockSpec((tm, w), lambda i: (i, 0))
    wsp = lambda w: pl.BlockSpec(w.shape, const, pipeline_mode=pl.Buffered(1))
    return pl.pallas_call(
        _merge_kernel,
        grid=(pl.cdiv(T, tm),),
        in_specs=[rowb(HG_WIDTH), rowb(SSM_WIDTH), rowb(ATT_WIDTH),
                  pl.BlockSpec((tm, 3 * D_MODEL), lambda i: (i, A16_GATE // (3 * D_MODEL))),
                  rowb(D_MODEL), wsp(wh), wsp(ws), wsp(wa), wsp(wo)],
        out_specs=rowb(D_MODEL),
        out_shape=jax.ShapeDtypeStruct((T, D_MODEL), F32),
        compiler_params=_cparams(("parallel",)),
        name="merge_out_proj",
    )(y_hg, y_ssm, y_att, a16, x, wh, ws, wa, wo)


def _ffn_kernel(x_ref, g_ref, wg_ref, wu_ref, wd_ref, *rest):
    n = (len(rest) - 1) // 2
    cast_in, o_ref, cast_out = rest[:n], rest[n], rest[n + 1:]
    x = x_ref[...]
    h = (x * lax.rsqrt(jnp.mean(x * x, axis=-1, keepdims=True) + EPS) * g_ref[...]).astype(BF16)
    a = _silu(_dot(h, wg_ref[...])) * _dot(h, wu_ref[...])
    o_ref[...] = x + _dot(a.astype(BF16), wd_ref[...])
    for src, dst in zip(cast_in, cast_out):
        dst[...] = src[...].astype(BF16)


def _ffn(x, gain, w_up, wd, tm, cast=()):
    T = x.shape[0]
    dff = wd.shape[0]
    steps = pl.cdiv(T, tm)
    const = lambda i: (0, 0)
    wsp = lambda w: pl.BlockSpec(w.shape, const, pipeline_mode=pl.Buffered(1))
    slab = [pl.BlockSpec((w.shape[0] // steps, w.shape[1]), lambda i: (i, 0)) for w in cast]
    for w in cast:
        assert w.shape[0] % (16 * steps) == 0
    outs = pl.pallas_call(
        _ffn_kernel,
        grid=(steps,),
        in_specs=[pl.BlockSpec((tm, D_MODEL), lambda i: (i, 0)), pl.BlockSpec((1, D_MODEL), const),
                  pl.BlockSpec((D_MODEL, dff), const, pipeline_mode=pl.Buffered(1)),
                  pl.BlockSpec((D_MODEL, dff), lambda i: (0, 1), pipeline_mode=pl.Buffered(1)),
                  wsp(wd)] + slab,
        out_specs=[pl.BlockSpec((tm, D_MODEL), lambda i: (i, 0))] + slab,
        out_shape=[jax.ShapeDtypeStruct((T, D_MODEL), F32)] + [jax.ShapeDtypeStruct(w.shape, BF16) for w in cast],
        compiler_params=_cparams(("parallel",)),
        name="ffn_dense",
    )(x, gain, w_up, w_up, wd, *cast)
    return outs[0], outs[1:]


def _router_kernel(x_ref, g_ref, wr_ref, h_ref, comb_ref, combt_ref):
    x = x_ref[...]
    h = x * lax.rsqrt(jnp.mean(x * x, axis=-1, keepdims=True) + EPS) * g_ref[...]
    hb = h.astype(BF16)
    h_ref[...] = hb
    h_lo = (h - hb.astype(F32)).astype(BF16)
    w = wr_ref[...]
    w_hi = w.astype(BF16)
    w_lo = (w - w_hi.astype(F32)).astype(BF16)
    logits = _dot(hb, w_hi) + _dot(hb, w_lo) + _dot(h_lo, w_hi)
    lane = lax.broadcasted_iota(jnp.int32, logits.shape, 1)
    logits = jnp.where(lane < N_EXPERTS, logits, NEG)
    v1 = jnp.max(logits, axis=-1, keepdims=True)
    i1 = jnp.min(jnp.where(logits == v1, lane, LANE), axis=-1, keepdims=True)
    rest = jnp.where(lane == i1, NEG, logits)
    v2 = jnp.max(rest, axis=-1, keepdims=True)
    i2 = jnp.min(jnp.where(rest == v2, lane, LANE), axis=-1, keepdims=True)
    e = jnp.exp(v2 - v1)
    w1 = 1.0 / (1.0 + e)
    w2 = e / (1.0 + e)
    comb = jnp.where(lane == i1, w1, 0.0) + jnp.where(lane == i2, w2, 0.0)
    comb_ref[...] = comb
    combt_ref[...] = comb.T[0:N_EXPERTS, :]


def _router(x, gain, wr, tm):
    T = x.shape[0]
    const = lambda i: (0, 0)
    return pl.pallas_call(
        _router_kernel,
        grid=(pl.cdiv(T, tm),),
        in_specs=[pl.BlockSpec((tm, D_MODEL), lambda i: (i, 0)), pl.BlockSpec((1, D_MODEL), const),
                  pl.BlockSpec((D_MODEL, LANE), const)],
        out_specs=[pl.BlockSpec((tm, D_MODEL), lambda i: (i, 0)), pl.BlockSpec((tm, LANE), lambda i: (i, 0)),
                   pl.BlockSpec((N_EXPERTS, tm), lambda i: (0, i))],
        out_shape=[jax.ShapeDtypeStruct((T, D_MODEL), BF16), jax.ShapeDtypeStruct((T, LANE), F32),
                   jax.ShapeDtypeStruct((N_EXPERTS, T), F32)],
        compiler_params=_cparams(("parallel",)),
        name="moe_router",
    )(x, gain, wr)


def _moe_kernel(h_ref, x_ref, comb_ref, combt_ref, tri_ref, wg_ref, wu_ref, wd_ref, o_ref,
                rank_scr, hs_scr, ys_scr, *, TB, SB, CH, nf):
    e = pl.program_id(1)
    f = pl.program_id(2)

    nsub = TB // SB
    subs = [slice(j * SB, (j + 1) * SB) for j in range(nsub)]

    @pl.when((e == 0) & (f == 0))
    def _():
        o_ref[...] = x_ref[...]
        routed_all = jnp.where(combt_ref[...] > 0.0, 1.0, 0.0).astype(BF16)
        for sb in subs:
            rank_scr[:, sb] = _dot(routed_all[:, sb], tri_ref[...])

    routed = combt_ref[pl.ds(e, 1), :] > 0.0
    rank = rank_scr[pl.ds(e, 1), :]
    count = None
    for sb in subs:
        cj = jnp.sum(jnp.where(routed[:, sb], 1.0, 0.0)).astype(jnp.int32)
        count = cj if count is None else jnp.maximum(count, cj)
    def body(c, carry, CH):
        slot = lax.broadcasted_iota(jnp.int32, (CH, SB), 0).astype(F32)
        if isinstance(c, int):
            rows, base = pl.ds(c * nsub * CH, nsub * CH), float(c * CH)
        else:
            rows, base = pl.ds(pl.multiple_of(c * (nsub * CH), 16), nsub * CH), (c * CH).astype(F32)
        onehots = [jnp.where((rank[:, sb] - base == slot) & routed[:, sb], 1.0, 0.0).astype(BF16) for sb in subs]

        @pl.when(f == 0)
        def _():
            hs_scr[rows, :] = jnp.concatenate(
                [_dot(p, h_ref[sb, :]) for p, sb in zip(onehots, subs)], axis=0).astype(BF16)

        hs = hs_scr[rows, :]
        a = _silu(_dot(hs, wg_ref[0])) * _dot(hs, wu_ref[0])
        y = _dot(a.astype(BF16), wd_ref[0])

        @pl.when(f == 0)
        def _():
            ys_scr[rows, :] = y

        @pl.when(f > 0)
        def _():
            ys_scr[rows, :] += y

        @pl.when(f == nf - 1)
        def _():
            ys = ys_scr[rows, :].astype(BF16)
            lane = lax.broadcasted_iota(jnp.int32, (TB, LANE), 1)
            gate = jnp.sum(jnp.where(lane == e, comb_ref[...], 0.0), axis=-1, keepdims=True)
            for j, (p, sb) in enumerate(zip(onehots, subs)):
                o_ref[sb, :] += gate[sb] * _dot_tn(p, ys[j * CH:(j + 1) * CH])

        return carry

    lo = 0
    for ch in CH[:-1]:
        @pl.when((count > lo) & (count <= ch))
        def _(ch=ch):
            body(0, 0, ch)
        lo = ch

    @pl.when(count > lo)
    def _():
        lax.fori_loop(0, (count + CH[-1] - 1) // CH[-1], functools.partial(body, CH=CH[-1]), 0)


def _moe(x, h, comb, combt, w_up, wd, TB, SB, CH, nf):
    T = x.shape[0]
    dff = wd.shape[1]
    tf = dff // nf
    slots = pl.cdiv(SB, CH[-1]) * CH[-1] * (TB // SB)
    tri = jnp.triu(jnp.ones((SB, SB), BF16), k=1)
    once = pl.Buffered(1)
    return pl.pallas_call(
        functools.partial(_moe_kernel, TB=TB, SB=SB, CH=CH, nf=nf),
        grid=(T // TB, N_EXPERTS, nf),
        in_specs=[pl.BlockSpec((TB, D_MODEL), lambda i, e, f: (i, 0), pipeline_mode=once),
                  pl.BlockSpec((TB, D_MODEL), lambda i, e, f: (i, 0), pipeline_mode=once),
                  pl.BlockSpec((TB, LANE), lambda i, e, f: (i, 0)),
                  pl.BlockSpec((N_EXPERTS, TB), lambda i, e, f: (0, i)),
                  pl.BlockSpec((SB, SB), lambda i, e, f: (0, 0), pipeline_mode=once),
                  pl.BlockSpec((1, D_MODEL, tf), lambda i, e, f: (e, 0, f)),
                  pl.BlockSpec((1, D_MODEL, tf), lambda i, e, f: (e, 0, nf + f)),
                  pl.BlockSpec((1, tf, D_MODEL), lambda i, e, f: (e, f, 0))],
        out_specs=pl.BlockSpec((TB, D_MODEL), lambda i, e, f: (i, 0)),
        out_shape=jax.ShapeDtypeStruct((T, D_MODEL), F32),
        scratch_shapes=[pltpu.VMEM((N_EXPERTS, TB), F32), pltpu.VMEM((slots, D_MODEL), BF16),
                        pltpu.VMEM((slots, D_MODEL), F32)],
        compiler_params=_cparams(("parallel", "arbitrary", "arbitrary")),
        name="moe_top2",
    )(h, x, comb, combt, tri, w_up, w_up, wd)


def _prep_w_in(w):
    wt = w.T
    hq, hf, hi, hg = (wt[j * HG_WIDTH:(j + 1) * HG_WIDTH] for j in range(4))
    z = wt[_OFF_SSM:_OFF_SSM + SSM_WIDTH]
    xbc = wt[_OFF_SSM + SSM_WIDTH:_OFF_SSM + SSM_WIDTH + SSM_CONV_DIM]
    dt = wt[_OFF_ATT - SSM_HEADS:_OFF_ATT]
    att = wt[_OFF_ATT:_OFF_GATE]
    gates = wt[_OFF_GATE:]
    w16 = jnp.concatenate([gates, xbc, hq, hi, hg, z, att], axis=0).astype(BF16)
    w32 = jnp.concatenate([hf, dt, jnp.zeros((LANE - SSM_HEADS, D_MODEL), w.dtype)], axis=0).astype(BF16)
    return w16, w32


def _pad_lanes(v, n, value=0.0):
    v = v.reshape(1, -1).astype(F32)
    return jnp.pad(v, ((0, 0), (0, n - v.shape[1])), constant_values=value)


def _run_group(x, B, L, C_h, C_s, valid, NB, tm, layer_params, states, kv_bufs):
    outs = []
    for l, P in enumerate(layer_params):
        st = states[l]
        a16, a32 = _norm_proj(x, P["mix_norm"], P["w16"], P["w32"], tm)
        a16b, a32b = a16.reshape(B, L, W16), a32.reshape(B, L, W32)
        y_hg, s_hg = _hgrn(a16b, a32b, P["lb"], P["hg_norm"], st["hgrn_t"], B, L, C_h, HG_SUB, valid, NB)
        y_ssm, s_ssm, s_conv = _ssd(a16b, a32b, st["conv"], P["conv_w"], P["conv_b"], P["dt_bias"], P["a_log"],
                                    P["d_skip"], P["ssm_norm"], st["ssm"], B, L, C_s, valid, NB)
        y_hg, y_ssm = y_hg.reshape(B * L, HG_WIDTH), y_ssm.reshape(B * L, SSM_WIDTH)
        if kv_bufs is None:
            y_att, kv0, kv1, kv2 = _attn_prompt(a16, P["gq"], P["gk"], P["bd"], B, L)
        else:
            y_att, kv0, kv1, kv2 = _attn_sample(a16, kv_bufs, l, P["gq"], P["gk"], P["bd"], B)
        x = _merge(y_hg, y_ssm, y_att, a16, x, P["w_out_hg"], P["w_out_ssm"], P["w_out_att"], P["w_o"], tm)
        if l % 2 == 0:
            F = P["ffn"]
            nxt = layer_params[l + 1]["moe"] if l + 1 < len(layer_params) else {}
            pending = [k for k in ("w_up", "wd") if k + "_f32" in nxt and k not in nxt]
            cast = [nxt[k + "_f32"].reshape(-1, nxt[k + "_f32"].shape[-1]) for k in pending]
            x, done = _ffn(x, P["ffn_norm"], F["w_up"], F["wd"], min(tm, 256), cast)
            for k, w in zip(pending, done):
                nxt[k] = w.reshape(nxt[k + "_f32"].shape)
        else:
            F = P["moe"]
            h, comb, combt = _router(x, P["ffn_norm"], F["wr"], tm)
            x = _moe(x, h, comb, combt, F["w_up"], F["wd"], min(MOE_BLOCK, x.shape[0]), min(MOE_SUB, x.shape[0]), MOE_SLOTS, 2)
        outs.append((s_hg, s_ssm, s_conv, kv0, kv1, kv2))
    return x, outs


def kernel(x_prompt, x_sample, state_hgrn, state_ssm, state_conv, cache_kv_w128, cache_kv_w512, cache_kv_w2048, w_in, mix_norm, hgrn_lb_logits, hgrn_norm, ssm_conv_w, ssm_conv_b, ssm_dt_bias, ssm_a_log, ssm_d, ssm_norm, att_q_norm, att_k_norm, w_out_hgrn, w_out_ssm, w_out_att, w_o, ffn_norm, w_ffn_up, w_ffn_down, w_router, w_moe_up, w_moe_down):
    depth = w_in.shape[0]
    Bp, Lp, _ = x_prompt.shape
    Bs, Ls, _ = x_sample.shape
    R = SAMPLE_ROWS
    for buf, w in zip((cache_kv_w128, cache_kv_w512, cache_kv_w2048), ATT_WINDOWS):
        assert buf.shape[2] == w, "sample attention assumes full window buffers"

    lb_soft = jax.nn.softmax(hgrn_lb_logits.astype(F32), axis=0)
    lb_table = jnp.cumsum(lb_soft, axis=0) - lb_soft[0]
    bd = jnp.kron(jnp.eye(ATT_HEADS, dtype=F32), jnp.full((ATT_HEAD_DIM, ATT_HEAD_DIM), 1.0 / ATT_HEAD_DIM, F32)).astype(BF16)

    layer_params = []
    for l in range(depth):
        w16, w32 = _prep_w_in(w_in[l])
        P = dict(
            w16=w16, w32=w32,
            mix_norm=mix_norm[l].reshape(1, -1), ffn_norm=ffn_norm[l].reshape(1, -1),
            lb=lb_table[l].reshape(1, -1), hg_norm=hgrn_norm[l].reshape(1, -1),
            conv_w=jnp.pad(ssm_conv_w[l], ((0, 8 - SSM_CONV), (0, 0))), conv_b=ssm_conv_b[l].reshape(1, -1),
            dt_bias=_pad_lanes(ssm_dt_bias[l], LANE), a_log=_pad_lanes(ssm_a_log[l], LANE),
            d_skip=jnp.repeat(ssm_d[l].astype(F32), SSM_HEAD_DIM).reshape(1, -1),
            ssm_norm=ssm_norm[l].reshape(1, -1),
            gq=jnp.tile(att_q_norm[l], (1, ATT_HEADS)), gk=jnp.tile(att_k_norm[l], (1, ATT_HEADS)), bd=bd,
            w_out_hg=w_out_hgrn[l].astype(BF16), w_out_ssm=w_out_ssm[l].astype(BF16),
            w_out_att=w_out_att[l].astype(BF16), w_o=w_o[l].astype(BF16),
        )
        if l % 2 == 0:
            P["ffn"] = dict(w_up=w_ffn_up[l // 2].astype(BF16), wd=w_ffn_down[l // 2].astype(BF16))
        else:
            P["moe"] = dict(wr=jnp.pad(w_router[l // 2].astype(F32), ((0, 0), (0, LANE - N_EXPERTS))),
                            w_up_f32=w_moe_up[l // 2], wd_f32=w_moe_down[l // 2])
        layer_params.append(P)

    zero_p = dict(hgrn_t=jnp.zeros((Bp, HG_WIDTH, HG_DIM), F32), ssm=jnp.zeros((Bp, SSM_WIDTH, SSM_STATE), F32),
                  conv=jnp.zeros((Bp, 8, SSM_CONV_DIM), F32))
    xp, outs_p = _run_group(x_prompt.reshape(Bp * Lp, D_MODEL), Bp, Lp, 64, 64, 64, 4, 512,
                            layer_params, [zero_p] * depth, None)

    xs = jnp.pad(x_sample, ((0, 0), (0, R - Ls), (0, 0))).reshape(Bs * R, D_MODEL)
    st_s = []
    for l in range(depth):
        st_s.append(dict(
            hgrn_t=jnp.swapaxes(state_hgrn[l], -1, -2).reshape(Bs, HG_WIDTH, HG_DIM),
            ssm=state_ssm[l].reshape(Bs, SSM_WIDTH, SSM_STATE),
            conv=jnp.pad(state_conv[l], ((0, 0), (8 - (SSM_CONV - 1), 0), (0, 0)))))
    kv_s = [jnp.transpose(c, (0, 1, 3, 4, 5, 2)).reshape(depth, Bs, 2 * ATT_WIDTH, c.shape[2])
            for c in (cache_kv_w128, cache_kv_w512, cache_kv_w2048)]
    xs, outs_s = _run_group(xs, Bs, R, R, R, Ls, 4, Bs * R, layer_params, st_s, kv_s)

    def pack(outs, B, nkv):
        s_hg = jnp.stack([jnp.swapaxes(o[0].reshape(B, HG_HEADS, HG_DIM, HG_DIM), -1, -2) for o in outs])
        s_ssm = jnp.stack([o[1].reshape(B, SSM_HEADS, SSM_HEAD_DIM, SSM_STATE) for o in outs])
        s_conv = jnp.stack([o[2][:, 8 - (SSM_CONV - 1):] for o in outs])
        kvs = []
        for j in range(3):
            n = nkv[j]
            kvs.append(jnp.stack([o[3 + j][:, :n].reshape(B, n, 2, ATT_HEADS, ATT_HEAD_DIM) for o in outs]))
        return (s_hg, s_ssm, s_conv, *kvs)

    keeps = [min(w, Lp) for w in ATT_WINDOWS]
    y_prompt = xp.reshape(Bp, Lp, D_MODEL)
    y_sample = xs.reshape(Bs, R, D_MODEL)[:, :Ls]
    return (y_prompt, y_sample, *pack(outs_p, Bp, keeps), *pack(outs_s, Bs, [Ls] * 3))
```

```python
import functools

import numpy as np
import jax
import jax.numpy as jnp
from jax import lax
from jax.experimental import pallas as pl
from jax.experimental.pallas import tpu as pltpu

F32 = jnp.float32
BF16 = jnp.bfloat16

D_MODEL = 1024
HG_HEADS, HG_DIM = 4, 128
HG_WIDTH = HG_HEADS * HG_DIM
SSM_HEADS, SSM_HEAD_DIM, SSM_GROUPS, SSM_STATE, SSM_CONV = 8, 64, 2, 128, 4
SSM_WIDTH = SSM_HEADS * SSM_HEAD_DIM
SSM_CONV_DIM = SSM_WIDTH + 2 * SSM_GROUPS * SSM_STATE
ATT_WINDOWS = (128, 512, 2048)
ATT_DILATIONS = (1, 4, 16)
ATT_ORDER = (2, 1, 0)
ATT_HEADS, ATT_HEAD_DIM = 4, 64
ATT_WIDTH = ATT_HEADS * ATT_HEAD_DIM
ATT_KEYS = 128
ATT_SCALE = ATT_HEAD_DIM ** -0.5
N_EXPERTS = 8
EPS = 1e-6
NEG = -1e30

_OFF_SSM = 4 * HG_WIDTH
_OFF_ATT = _OFF_SSM + SSM_WIDTH + SSM_CONV_DIM + SSM_HEADS
_OFF_GATE = _OFF_ATT + 3 * 3 * ATT_WIDTH

A16_GATE, A16_XBC, A16_HQ, A16_HI, A16_HG, A16_Z, A16_ATT = 0, 3072, 4096, 4608, 5120, 5632, 6144
W16 = 8448
A32_HF, A32_DT = 0, 512
W32 = 640

LANE = 128
SAMPLE_ROWS = 16
HG_SUB = 8
MOE_BLOCK = 1024
MOE_SUB = 512
MOE_SLOTS = (128, 144, 160)
VMEM_LIMIT = 56 * 2 ** 20


def _cparams(sem):
    return pltpu.CompilerParams(dimension_semantics=sem, vmem_limit_bytes=VMEM_LIMIT)


def _dot(a, b):
    return jnp.dot(a, b, preferred_element_type=F32)


def _dot_nt(a, b):
    return lax.dot_general(a, b, (((1,), (1,)), ((), ())), preferred_element_type=F32)


def _dot_tn(a, b):
    return lax.dot_general(a, b, (((0,), (0,)), ((), ())), preferred_element_type=F32)


def _split3(x):
    hi = x.astype(BF16)
    r = x - hi.astype(F32)
    mid = r.astype(BF16)
    lo = (r - mid.astype(F32)).astype(BF16)
    return hi, mid, lo


def _cumsum_rows(x, tril_bf16):
    hi, mid, lo = _split3(x)
    return _dot(tril_bf16, hi) + _dot(tril_bf16, mid) + _dot(tril_bf16, lo)


def _sigmoid(x):
    return 0.5 * jnp.tanh(0.5 * x) + 0.5


def _silu(x):
    return x * _sigmoid(x)


def _tril_bf16(n):
    r = lax.broadcasted_iota(jnp.int32, (n, n), 0)
    c = lax.broadcasted_iota(jnp.int32, (n, n), 1)
    return jnp.where(c <= r, 1.0, 0.0).astype(BF16)


def _norm_proj_kernel(x_ref, g_ref, w16_ref, w32_ref, o16_ref, o32_ref, *, chunk):
    x = x_ref[...]
    h = (x * lax.rsqrt(jnp.mean(x * x, axis=-1, keepdims=True) + EPS) * g_ref[...]).astype(BF16)
    for c0 in range(0, W16, chunk):
        o16_ref[:, c0:c0 + chunk] = _dot_nt(h, w16_ref[c0:c0 + chunk, :]).astype(BF16)
    o32_ref[...] = _dot_nt(h, w32_ref[...])


def _norm_proj(x, gain, w16, w32, tm):
    T = x.shape[0]
    const = lambda i: (0, 0)
    return pl.pallas_call(
        functools.partial(_norm_proj_kernel, chunk=768),
        grid=(pl.cdiv(T, tm),),
        in_specs=[
            pl.BlockSpec((tm, D_MODEL), lambda i: (i, 0)),
            pl.BlockSpec((1, D_MODEL), const),
            pl.BlockSpec((W16, D_MODEL), const, pipeline_mode=pl.Buffered(1)),
            pl.BlockSpec((W32, D_MODEL), const, pipeline_mode=pl.Buffered(1)),
        ],
        out_specs=[
            pl.BlockSpec((tm, W16), lambda i: (i, 0)),
            pl.BlockSpec((tm, W32), lambda i: (i, 0)),
        ],
        out_shape=[jax.ShapeDtypeStruct((T, W16), BF16), jax.ShapeDtypeStruct((T, W32), F32)],
        compiler_params=_cparams(("parallel",)),
        name="norm_proj",
    )(x, gain, w16, w32)


def _hgrn_tables(C, sub):
    t = np.arange(C)[:, None]
    u = np.arange(C)[None, :]
    blocks = [u <= t]
    levels = []
    l = sub
    while l < C:
        upper = (t % (2 * l)) >= l
        b = (t // (2 * l)) * (2 * l) + l - 1
        blocks.append(np.where(upper, (u > b) & (u <= t), (u > t) & (u <= b)))
        levels.append(l)
        l *= 2
    for s in range(sub):
        r = (t // sub) * sub + s
        blocks.append((u > r) & (u <= t))
    sel = [u == (t // sub) * sub + s for s in range(sub)]
    d = np.concatenate(blocks, 0).astype(np.float32)
    sl = np.concatenate(sel, 0).astype(np.float32)
    table = np.block([[d, d, d, np.zeros_like(d)], [np.zeros((sl.shape[0], 3 * C), np.float32), sl]])
    return jnp.asarray(table, BF16), tuple(levels)


def _hgrn_chunk(q_ref, f_ref, i_ref, g_ref, lb_ref, gn_ref, tab_ref, y_ref, st_scr, *, C, sub, levels, valid):
    lb = lb_ref[...]
    fgate = lb + (1.0 - lb) * jax.nn.sigmoid(f_ref[...])
    logf = jnp.log(fgate)
    kk = 1.0 - fgate
    rows = lax.broadcasted_iota(jnp.int32, (C, HG_WIDTH), 0)
    if valid < C:
        logf = jnp.where(rows < valid, logf, 0.0)
        kk = jnp.where(rows < valid, kk, 0.0)
    qa = _silu(q_ref[...].astype(F32))
    va = i_ref[...].astype(F32)
    ga = g_ref[...].astype(F32)

    ed = _dot(tab_ref[...], jnp.concatenate([*_split3(logf), kk.astype(BF16)], axis=0))
    nblk = 1 + len(levels) + sub
    blk = lambda j: ed[j * C:(j + 1) * C]
    G = blk(0)
    ksel = ed[nblk * C:]

    row_c = lax.broadcasted_iota(jnp.int32, (C, C), 0)
    lane_c = lax.broadcasted_iota(jnp.int32, (C, C), 1)
    off_c = lane_c - (row_c // sub) * sub

    A = [jnp.zeros((C, C), F32) for _ in range(HG_HEADS)]
    for s in range(sub):
        prod = qa * ksel[s * C:(s + 1) * C] * jnp.exp(blk(1 + len(levels) + s))
        for h in range(HG_HEADS):
            col = jnp.sum(prod[:, h * HG_DIM:(h + 1) * HG_DIM], axis=-1, keepdims=True)
            A[h] = jnp.where(off_c == s, col, A[h])
    A = [jnp.where(lane_c <= row_c, a, 0.0) for a in A]

    for li, l in enumerate(levels):
        upper = (rows % (2 * l)) >= l
        scaled = jnp.where(upper, qa, kk) * jnp.exp(blk(1 + li))
        qt = jnp.where(upper, scaled, 0.0).astype(BF16)
        kt = jnp.where(upper, 0.0, scaled).astype(BF16)
        same = (row_c // (2 * l)) == (lane_c // (2 * l))
        for h in range(HG_HEADS):
            sl = slice(h * HG_DIM, (h + 1) * HG_DIM)
            a_l = _dot_nt(qt[:, sl], kt[:, sl])
            A[h] = A[h] + (jnp.where(same, a_l, 0.0) if 2 * l < C else a_l)

    outs = []
    for h in range(HG_HEADS):
        sl = slice(h * HG_DIM, (h + 1) * HG_DIM)
        Gh, qh, kh = G[:, sl], qa[:, sl], kk[:, sl]
        vb = va[:, sl].astype(BF16)
        st = st_scr[sl, :]
        o = _dot(A[h].astype(BF16), vb) + _dot_nt((qh * jnp.exp(Gh)).astype(BF16), st.astype(BF16))
        g_last = Gh[C - 1:C]
        kdec = kh * jnp.exp(g_last - Gh)
        st_scr[sl, :] = jnp.exp(g_last) * st + _dot_tn(vb, kdec.astype(BF16))
        on = o * lax.rsqrt(jnp.mean(o * o, axis=-1, keepdims=True) + EPS) * gn_ref[...]
        outs.append(on * _silu(ga[:, sl]))
    y_ref[...] = jnp.concatenate(outs, axis=1).astype(BF16)


def _ssd_chunk(xbc_ref, z_ref, dt_ref, cw_ref, cb_ref, dtb_ref, alog_ref, dsk_ref, sn_ref,
               y_ref, ubuf, hs, *, C, valid):
    ubuf[8:8 + C, :] = xbc_ref[...].astype(F32)
    acc = cb_ref[...] + ubuf[5:5 + C, :] * cw_ref[0:1, :]
    for j in range(1, SSM_CONV):
        acc = acc + ubuf[5 + j:5 + j + C, :] * cw_ref[j:j + 1, :]
    xc = _silu(acc)
    ubuf[0:8, :] = ubuf[valid:valid + 8, :]

    x = dt_ref[...] + dtb_ref[...]
    dt = jnp.maximum(x, 0.0) + jnp.log1p(jnp.exp(-jnp.abs(x)))
    if valid < C:
        rows = lax.broadcasted_iota(jnp.int32, (C, LANE), 0)
        dt = jnp.where(rows < valid, dt, 0.0)
    a = -jnp.exp(alog_ref[...])
    cum = _cumsum_rows(dt * a, _tril_bf16(C))
    cum_t = cum.T
    dt_t = dt.T

    ri = lax.broadcasted_iota(jnp.int32, (C, C), 0)
    cj = lax.broadcasted_iota(jnp.int32, (C, C), 1)
    tril = cj <= ri
    lane = lax.broadcasted_iota(jnp.int32, (C, LANE), 1)
    srow = lax.broadcasted_iota(jnp.int32, (LANE, LANE), 0)
    bm = [xc[:, SSM_WIDTH + g * SSM_STATE:SSM_WIDTH + (g + 1) * SSM_STATE].astype(BF16) for g in range(SSM_GROUPS)]
    off_c = SSM_WIDTH + SSM_GROUPS * SSM_STATE
    cm = [xc[:, off_c + g * SSM_STATE:off_c + (g + 1) * SSM_STATE].astype(BF16) for g in range(SSM_GROUPS)]
    cb = [_dot_nt(cm[g], bm[g]) for g in range(SSM_GROUPS)]

    ys = []
    for p in range(SSM_HEADS // 2):
        g = (2 * p) // (SSM_HEADS // SSM_GROUPS)
        xp = xc[:, p * LANE:(p + 1) * LANE]
        yp = None
        colw = []
        for k, hd in enumerate((2 * p, 2 * p + 1)):
            colb = jnp.broadcast_to(cum[:, hd:hd + 1], (C, LANE))
            rowb = jnp.broadcast_to(cum_t[hd:hd + 1, :], (C, C))
            dtr = jnp.broadcast_to(dt_t[hd:hd + 1, :], (C, C))
            w = cb[g] * jnp.exp(jnp.where(tril, colb[:, :C] - rowb, NEG)) * dtr
            half = (lane < SSM_HEAD_DIM) if k == 0 else (lane >= SSM_HEAD_DIM)
            t = _dot(w.astype(BF16), jnp.where(half, xp, 0.0).astype(BF16))
            yp = t if yp is None else yp + t
            colw.append((colb, jnp.broadcast_to(dt[:, hd:hd + 1], (C, LANE))))
        first = lane < SSM_HEAD_DIM
        cum_e = jnp.where(first, colw[0][0], colw[1][0])
        dt_e = jnp.where(first, colw[0][1], colw[1][1])
        hp = hs[p * LANE:(p + 1) * LANE, :]
        yp = yp + jnp.exp(cum_e) * _dot_nt(cm[g], hp.astype(BF16))
        c_last = cum_e[C - 1:C, :]
        w_in = jnp.exp(c_last - cum_e) * dt_e
        dec = jnp.where(srow < SSM_HEAD_DIM,
                        jnp.broadcast_to(cum[C - 1:C, 2 * p:2 * p + 1], (LANE, LANE)),
                        jnp.broadcast_to(cum[C - 1:C, 2 * p + 1:2 * p + 2], (LANE, LANE)))
        hs[p * LANE:(p + 1) * LANE, :] = jnp.exp(dec) * hp + _dot_tn((xp * w_in).astype(BF16), bm[g])
        ys.append(yp)
    y = jnp.concatenate(ys, axis=1) + dsk_ref[...] * xc[:, :SSM_WIDTH]
    yz = y * _silu(z_ref[...].astype(F32))
    y_ref[...] = (yz * lax.rsqrt(jnp.mean(yz * yz, axis=-1, keepdims=True) + EPS) * sn_ref[...]).astype(BF16)


def _rec_kernel(q_ref, f_ref, i_ref, g_ref, lb_ref, gn_ref, s0_ref, tab_ref,
                xbc_ref, z_ref, dt_ref, pre_ref, cw_ref, cb_ref, dtb_ref, alog_ref, dsk_ref, sn_ref, h0_ref,
                *rest, NB, nchunks, first, hg_kw, ssd_kw):
    yh_ref, sout_ref, ys_ref, hout_ref, cout_ref, st_scr, ubuf, hs = rest[-8:]
    ci = pl.program_id(1)

    @pl.when(ci == 0)
    def _():
        st_scr[...] = s0_ref[...]
        ubuf[:, 0:8, :] = pre_ref[...]
        hs[...] = h0_ref[...]

    for j in range(NB):
        _hgrn_chunk(q_ref.at[j], f_ref.at[j], i_ref.at[j], g_ref.at[j], lb_ref, gn_ref, tab_ref,
                    yh_ref.at[j], st_scr.at[j], **hg_kw)
        _ssd_chunk(xbc_ref.at[j], z_ref.at[j], dt_ref.at[j], cw_ref, cb_ref, dtb_ref, alog_ref, dsk_ref, sn_ref,
                   ys_ref.at[j], ubuf.at[j], hs.at[j], **ssd_kw)

    @pl.when(ci == nchunks - 1)
    def _():
        for d in range(sout_ref.shape[0]):
            sout_ref[d] = st_scr[...]
            hout_ref[d] = hs[...]
            cout_ref[d] = ubuf[:, 0:8, :]


def _recurrent_mixers(a16, a32, lb, gn, s0t, prefix, cw, cb, dtb, alog, dsk, sn, h0, B, L, C, sub, valid, NB,
                      layer, depth, prev):
    nch = L // C
    table, levels = _hgrn_tables(C, sub)
    const = lambda b, c: (0, 0)
    bat = lambda b, c: (b, 0, 0)
    blk = lambda w, col: pl.BlockSpec((NB, C, w), lambda b, c: (b, c, col))
    first = prev is None
    slot = (lambda rows, w: pl.BlockSpec((depth, NB, rows, w), lambda b, c: (0, b, 0, 0))) if first else \
           (lambda rows, w: pl.BlockSpec((1, NB, rows, w), lambda b, c: (layer, b, 0, 0)))
    return pl.pallas_call(
        functools.partial(_rec_kernel, NB=NB, nchunks=nch, first=first,
                          hg_kw=dict(C=C, sub=sub, levels=levels, valid=valid), ssd_kw=dict(C=C, valid=valid)),
        grid=(B // NB, nch),
        in_specs=[
            blk(HG_WIDTH, A16_HQ // HG_WIDTH), blk(HG_WIDTH, A32_HF // HG_WIDTH),
            blk(HG_WIDTH, A16_HI // HG_WIDTH), blk(HG_WIDTH, A16_HG // HG_WIDTH),
            pl.BlockSpec((1, HG_WIDTH), const),
            pl.BlockSpec((1, HG_DIM), const),
            pl.BlockSpec((NB, HG_WIDTH, HG_DIM), bat),
            pl.BlockSpec(table.shape, const),
            blk(SSM_CONV_DIM, A16_XBC // SSM_CONV_DIM), blk(SSM_WIDTH, A16_Z // SSM_WIDTH), blk(LANE, A32_DT // LANE),
            pl.BlockSpec((NB, 8, SSM_CONV_DIM), bat),
            pl.BlockSpec((8, SSM_CONV_DIM), const),
            pl.BlockSpec((1, SSM_CONV_DIM), const),
            pl.BlockSpec((1, LANE), const),
            pl.BlockSpec((1, LANE), const),
            pl.BlockSpec((1, SSM_WIDTH), const),
            pl.BlockSpec((1, SSM_WIDTH), const),
            pl.BlockSpec((NB, SSM_WIDTH, SSM_STATE), bat),
        ] + ([] if first else [pl.BlockSpec(memory_space=pl.ANY)] * 3),
        out_specs=[
            blk(HG_WIDTH, 0), slot(HG_WIDTH, HG_DIM),
            blk(SSM_WIDTH, 0), slot(SSM_WIDTH, SSM_STATE), slot(8, SSM_CONV_DIM),
        ],
        out_shape=[jax.ShapeDtypeStruct((B, L, HG_WIDTH), BF16),
                   jax.ShapeDtypeStruct((depth, B, HG_WIDTH, HG_DIM), F32),
                   jax.ShapeDtypeStruct((B, L, SSM_WIDTH), BF16),
                   jax.ShapeDtypeStruct((depth, B, SSM_WIDTH, SSM_STATE), F32),
                   jax.ShapeDtypeStruct((depth, B, 8, SSM_CONV_DIM), F32)],
        input_output_aliases={} if first else {19: 1, 20: 3, 21: 4},
        scratch_shapes=[pltpu.VMEM((NB, HG_WIDTH, HG_DIM), F32), pltpu.VMEM((NB, C + 8, SSM_CONV_DIM), F32),
                        pltpu.VMEM((NB, SSM_WIDTH, SSM_STATE), F32)],
        compiler_params=_cparams(("parallel", "arbitrary")),
        name="hgrn2_ssd",
    )(a16, a32, a16, a16, lb, gn, s0t, table, a16, a16, a32, prefix, cw, cb, dtb, alog, dsk, sn, h0,
      *([] if first else prev))


def _head_norm(x, gain, bd):
    ms = _dot((x * x).astype(BF16), bd)
    return x * (lax.rsqrt(ms + EPS) * gain)


def _head_masks(rows):
    lane = lax.broadcasted_iota(jnp.int32, (rows, ATT_WIDTH), 1)
    return [(lane >= h * ATT_HEAD_DIM) & (lane < (h + 1) * ATT_HEAD_DIM) for h in range(ATT_HEADS)]


def _stack_heads(q, masks):
    return jnp.concatenate([jnp.where(m, q, 0.0) for m in masks], axis=0).astype(BF16)


def _unstack_heads(pv, m, d, masks):
    R = pv.shape[0] // ATT_HEADS
    acc, mx, den = pv[0:R], jnp.broadcast_to(m[0:R], (R, ATT_WIDTH)), jnp.broadcast_to(d[0:R], (R, ATT_WIDTH))
    for h in range(1, ATT_HEADS):
        sl = slice(h * R, (h + 1) * R)
        acc = jnp.where(masks[h], pv[sl], acc)
        mx = jnp.where(masks[h], m[sl], mx)
        den = jnp.where(masks[h], d[sl], den)
    return acc, mx, den


def _attend(qb, kb, vb, bias4, masks):
    s = _dot_nt(_stack_heads(qb, masks), kb) + bias4
    m = jnp.max(s, axis=-1, keepdims=True)
    p = jnp.exp(s - m)
    d = jnp.sum(p, axis=-1, keepdims=True)
    return _unstack_heads(_dot(p.astype(BF16), vb), m, d, masks)


def _ld2(pair, s):
    return jnp.concatenate([pair[0][s, :], pair[1][s, :]], axis=1)


def _st2(pair, s, val):
    pair[0][s, :] = val[:, :LANE]
    pair[1][s, :] = val[:, LANE:]


def _attn_prompt_kernel(att_ref, gq_ref, gk_ref, bd_ref, *rest, L):
    y_ref, kv0_ref, kv1_ref, kv2_ref = rest[-16:-12]
    scr = rest[-12:]
    qn, kn, vv, ya, ma, za = (scr[2 * j:2 * j + 2] for j in range(6))
    g = pl.program_id(1)
    RB = 256
    QB = ATT_KEYS

    def group(step, gi, dil, keep, kv_ref):
        M = L // dil
        nb = M // QB
        KW = min(2 * QB, M)
        bd = bd_ref[...]

        def norm_body(i, carry):
            r0 = pl.multiple_of(i * RB, RB)
            x = att_ref[pl.ds(r0, RB), :].astype(F32)
            _st2(qn, pl.ds(r0, RB), _head_norm(x[:, :ATT_WIDTH], gq_ref[gi:gi + 1, :] * ATT_SCALE, bd))
            _st2(kn, pl.ds(r0, RB), _head_norm(x[:, ATT_WIDTH:2 * ATT_WIDTH], gk_ref[gi:gi + 1, :], bd))
            _st2(vv, pl.ds(r0, RB), x[:, 2 * ATT_WIDTH:])
            return carry

        lax.fori_loop(0, L // RB, norm_body, 0)
        cw = min(keep, RB)
        for c0 in range(0, keep, cw):
            rows = slice(L - keep + c0, L - keep + c0 + cw)
            for p, half in enumerate((kn[0], kn[1], vv[0], vv[1])):
                piece = half[rows, :].T
                for d in range(kv_ref.shape[0]):
                    kv_ref[d, 0, p * LANE:(p + 1) * LANE, c0:c0 + cw] = piece

        masks = _head_masks(QB)
        iq = lax.broadcasted_iota(jnp.int32, (ATT_HEADS * QB, KW), 0) & (QB - 1)
        ik = lax.broadcasted_iota(jnp.int32, (ATT_HEADS * QB, KW), 1)

        def run_blocks(first, count, lead):
            dist = (iq - ik) if lead else (QB + iq - ik)
            bias4 = jnp.where((dist >= 0) & (dist <= ATT_KEYS), 0.0, NEG)
            unroll = max(u for u in (4, 3, 2, 1) if count % u == 0)

            def blk_body(j, carry):
                loaded = []
                for u in range(unroll):
                    it = first + j * unroll + u
                    n = it // dil
                    r = it - n * dil
                    k0 = 0 if lead else (n - 1) * QB
                    if dil == 1:
                        qs = pl.ds(pl.multiple_of(n * QB, QB), QB)
                        ks = pl.ds(pl.multiple_of(k0, QB), KW)
                    else:
                        qs = pl.ds(n * QB * dil + r, QB, stride=dil)
                        ks = pl.ds(k0 * dil + r, KW, stride=dil)
                    old = None if step == 0 else (_ld2(ya, qs), _ld2(ma, qs), _ld2(za, qs))
                    loaded.append((qs, _ld2(qn, qs), _ld2(kn, ks).astype(BF16), _ld2(vv, ks).astype(BF16), old))
                for qs, qb, kb, vb, old in loaded:
                    acc, mx, den = _attend(qb, kb, vb, bias4, masks)
                    if step > 0:
                        yo, mo, zo = old
                        mn = jnp.maximum(mo, mx)
                        eo = jnp.exp(mo - mn)
                        en = jnp.exp(mx - mn)
                        acc, mx, den = yo * eo + acc * en, mn, zo * eo + den * en
                    _st2(ya, qs, acc)
                    _st2(ma, qs, mx)
                    _st2(za, qs, den)
                return carry

            lax.fori_loop(0, count // unroll, blk_body, 0)

        run_blocks(0, dil, True)
        if nb > 1:
            run_blocks(dil, dil * (nb - 1), False)

    for step, gi in enumerate(ATT_ORDER):
        @pl.when(g == step)
        def _(step=step, gi=gi):
            group(step, gi, ATT_DILATIONS[gi], min(ATT_WINDOWS[gi], L), (kv0_ref, kv1_ref, kv2_ref)[gi])

    @pl.when(g == 2)
    def _():
        def out_body(i, carry):
            r0 = pl.multiple_of(i * RB, RB)
            y_ref[pl.ds(r0, RB), :] = (_ld2(ya, pl.ds(r0, RB)) / _ld2(za, pl.ds(r0, RB))).astype(BF16)
            return carry
        lax.fori_loop(0, L // RB, out_body, 0)


def _attn_prompt(a16, gq, gk, bd, B, L, layer, depth, prev):
    keeps = [min(w, L) for w in ATT_WINDOWS]
    const = lambda b, g: (0, 0)
    first = prev is None
    slot = (lambda k: pl.BlockSpec((depth, 1, 2 * ATT_WIDTH, k), lambda b, g: (0, b, 0, 0))) if first else \
           (lambda k: pl.BlockSpec((1, 1, 2 * ATT_WIDTH, k), lambda b, g: (layer, b, 0, 0)))
    return pl.pallas_call(
        functools.partial(_attn_prompt_kernel, L=L),
        grid=(B, 3),
        in_specs=[
            pl.BlockSpec((L, 3 * ATT_WIDTH), lambda b, g: (b, A16_ATT // (3 * ATT_WIDTH) + 2 - g)),
            pl.BlockSpec((3, ATT_WIDTH), const),
            pl.BlockSpec((3, ATT_WIDTH), const),
            pl.BlockSpec((ATT_WIDTH, ATT_WIDTH), const),
        ] + ([] if first else [pl.BlockSpec(memory_space=pl.ANY)] * 3),
        out_specs=[pl.BlockSpec((L, ATT_WIDTH), lambda b, g: (b, 0))] + [slot(k) for k in keeps],
        out_shape=[jax.ShapeDtypeStruct((B * L, ATT_WIDTH), BF16)]
        + [jax.ShapeDtypeStruct((depth, B, 2 * ATT_WIDTH, k), F32) for k in keeps],
        input_output_aliases={} if first else {4: 1, 5: 2, 6: 3},
        scratch_shapes=[pltpu.VMEM((L, LANE), F32) for _ in range(12)],
        compiler_params=_cparams(("parallel", "arbitrary")),
        name="attn_prompt",
    )(a16, gq, gk, bd, *([] if first else prev))


def _attn_sample_kernel(a0_ref, a1_ref, a2_ref, b0_ref, b1_ref, b2_ref, gq_ref, gk_ref, bd_ref,
                        y_ref, k0_ref, k1_ref, k2_ref):
    R = SAMPLE_ROWS
    bd = bd_ref[...]
    masks = _head_masks(R)
    ya = ma = za = None
    for gi, (a_ref, b_ref, k_ref) in enumerate(((a0_ref, b0_ref, k0_ref), (a1_ref, b1_ref, k1_ref),
                                                (a2_ref, b2_ref, k2_ref))):
        dil = ATT_DILATIONS[gi]
        wb = b_ref.shape[3]
        x = a_ref[...].astype(F32)
        q = _head_norm(x[:, :ATT_WIDTH], gq_ref[gi:gi + 1, :] * ATT_SCALE, bd)
        k = _head_norm(x[:, ATT_WIDTH:2 * ATT_WIDTH], gk_ref[gi:gi + 1, :], bd)
        v = x[:, 2 * ATT_WIDTH:]
        k_ref[0] = jnp.concatenate([k[:8], v[:8]], axis=1)
        k_t = b_ref[0, 0, 0:ATT_WIDTH, :].astype(BF16)
        v_t = b_ref[0, 0, ATT_WIDTH:2 * ATT_WIDTH, :].astype(BF16)
        kb, vb = k.astype(BF16), v.astype(BF16)

        def bias(dist):
            return jnp.where((dist >= 0) & (dist <= ATT_KEYS * dil) & ((dist & (dil - 1)) == 0), 0.0, NEG)

        R4 = ATT_HEADS * R
        row_b = lax.broadcasted_iota(jnp.int32, (R4, wb), 0) & (R - 1)
        row_n = lax.broadcasted_iota(jnp.int32, (R4, R), 0) & (R - 1)
        qs = _stack_heads(q, masks)
        s_b = _dot(qs, k_t) + bias(wb + row_b - lax.broadcasted_iota(jnp.int32, (R4, wb), 1))
        s_n = _dot_nt(qs, kb) + bias(row_n - lax.broadcasted_iota(jnp.int32, (R4, R), 1))
        m = jnp.maximum(jnp.max(s_b, axis=-1, keepdims=True), jnp.max(s_n, axis=-1, keepdims=True))
        p_b = jnp.exp(s_b - m)
        p_n = jnp.exp(s_n - m)
        d = jnp.sum(p_b, axis=-1, keepdims=True) + jnp.sum(p_n, axis=-1, keepdims=True)
        pv = _dot_nt(p_b.astype(BF16), v_t) + _dot(p_n.astype(BF16), vb)
        acc, mx, den = _unstack_heads(pv, m, d, masks)
        if gi == 0:
            ya, ma, za = acc, mx, den
        else:
            mn = jnp.maximum(ma, mx)
            eo = jnp.exp(ma - mn)
            en = jnp.exp(mx - mn)
            ya, za, ma = ya * eo + acc * en, za * eo + den * en, mn
    y_ref[...] = (ya / za).astype(BF16)


def _attn_sample(a16, bufs, layer, gq, gk, bd, B):
    R = SAMPLE_ROWS
    const = lambda b: (0, 0)
    bat = lambda b: (b, 0, 0)
    base = A16_ATT // (3 * ATT_WIDTH)
    return pl.pallas_call(
        _attn_sample_kernel,
        grid=(B,),
        in_specs=[pl.BlockSpec((R, 3 * ATT_WIDTH), functools.partial(lambda b, j: (b, j), j=base + g))
                  for g in range(3)]
        + [pl.BlockSpec((1, 1, 2 * ATT_WIDTH, buf.shape[3]), lambda b: (layer, b, 0, 0)) for buf in bufs]
        + [pl.BlockSpec((3, ATT_WIDTH), const), pl.BlockSpec((3, ATT_WIDTH), const),
           pl.BlockSpec((ATT_WIDTH, ATT_WIDTH), const)],
        out_specs=[pl.BlockSpec((R, ATT_WIDTH), lambda b: (b, 0))]
        + [pl.BlockSpec((1, 8, 2 * ATT_WIDTH), bat) for _ in range(3)],
        out_shape=[jax.ShapeDtypeStruct((B * R, ATT_WIDTH), BF16)]
        + [jax.ShapeDtypeStruct((B, 8, 2 * ATT_WIDTH), F32) for _ in range(3)],
        compiler_params=_cparams(("parallel",)),
        name="attn_sample",
    )(a16, a16, a16, *bufs, gq, gk, bd)


def _merge_kernel(yh_ref, ys_ref, ya_ref, gt_ref, x_ref, wh_ref, ws_ref, wa_ref, wo_ref, o_ref):
    gt = _sigmoid(gt_ref[...].astype(F32))
    mixed = (gt[:, 0:D_MODEL] * _dot(yh_ref[...], wh_ref[...])
             + gt[:, D_MODEL:2 * D_MODEL] * _dot(ys_ref[...], ws_ref[...])
             + gt[:, 2 * D_MODEL:] * _dot(ya_ref[...], wa_ref[...]))
    o_ref[...] = x_ref[...] + _dot(mixed.astype(BF16), wo_ref[...])


def _merge(y_hg, y_ssm, y_att, a16, x, wh, ws, wa, wo, tm):
    T = x.shape[0]
    const = lambda i: (0, 0)
    rowb = lambda w: pl.BlockSpec((tm, w), lambda i: (i, 0))
    wsp = lambda w: pl.BlockSpec(w.shape, const, pipeline_mode=pl.Buffered(1))
    return pl.pallas_call(
        _merge_kernel,
        grid=(pl.cdiv(T, tm),),
        in_specs=[rowb(HG_WIDTH), rowb(SSM_WIDTH), rowb(ATT_WIDTH),
                  pl.BlockSpec((tm, 3 * D_MODEL), lambda i: (i, A16_GATE // (3 * D_MODEL))),
                  rowb(D_MODEL), wsp(wh), wsp(ws), wsp(wa), wsp(wo)],
        out_specs=rowb(D_MODEL),
        out_shape=jax.ShapeDtypeStruct((T, D_MODEL), F32),
        compiler_params=_cparams(("parallel",)),
        name="merge_out_proj",
    )(y_hg, y_ssm, y_att, a16, x, wh, ws, wa, wo)


def _ffn_kernel(x_ref, g_ref, wg_ref, wu_ref, wd_ref, *rest):
    n = (len(rest) - 1) // 2
    cast_in, o_ref, cast_out = rest[:n], rest[n], rest[n + 1:]
    x = x_ref[...]
    h = (x * lax.rsqrt(jnp.mean(x * x, axis=-1, keepdims=True) + EPS) * g_ref[...]).astype(BF16)
    a = _silu(_dot(h, wg_ref[...])) * _dot(h, wu_ref[...])
    o_ref[...] = x + _dot(a.astype(BF16), wd_ref[...])
    for src, dst in zip(cast_in, cast_out):
        dst[...] = src[...].astype(BF16)


def _ffn(x, gain, w_up, wd, tm, cast=()):
    T = x.shape[0]
    dff = wd.shape[0]
    steps = pl.cdiv(T, tm)
    const = lambda i: (0, 0)
    wsp = lambda w: pl.BlockSpec(w.shape, const, pipeline_mode=pl.Buffered(1))
    slab = [pl.BlockSpec((w.shape[0] // steps, w.shape[1]), lambda i: (i, 0)) for w in cast]
    for w in cast:
        assert w.shape[0] % (16 * steps) == 0
    outs = pl.pallas_call(
        _ffn_kernel,
        grid=(steps,),
        in_specs=[pl.BlockSpec((tm, D_MODEL), lambda i: (i, 0)), pl.BlockSpec((1, D_MODEL), const),
                  pl.BlockSpec((D_MODEL, dff), const, pipeline_mode=pl.Buffered(1)),
                  pl.BlockSpec((D_MODEL, dff), lambda i: (0, 1), pipeline_mode=pl.Buffered(1)),
                  wsp(wd)] + slab,
        out_specs=[pl.BlockSpec((tm, D_MODEL), lambda i: (i, 0))] + slab,
        out_shape=[jax.ShapeDtypeStruct((T, D_MODEL), F32)] + [jax.ShapeDtypeStruct(w.shape, BF16) for w in cast],
        compiler_params=_cparams(("parallel",)),
        name="ffn_dense",
    )(x, gain, w_up, w_up, wd, *cast)
    return outs[0], outs[1:]


def _router_kernel(x_ref, g_ref, wr_ref, h_ref, comb_ref, combt_ref):
    x = x_ref[...]
    h = x * lax.rsqrt(jnp.mean(x * x, axis=-1, keepdims=True) + EPS) * g_ref[...]
    hb = h.astype(BF16)
    h_ref[...] = hb
    h_lo = (h - hb.astype(F32)).astype(BF16)
    w = wr_ref[...]
    w_hi = w.astype(BF16)
    w_lo = (w - w_hi.astype(F32)).astype(BF16)
    logits = _dot(hb, w_hi) + _dot(hb, w_lo) + _dot(h_lo, w_hi)
    lane = lax.broadcasted_iota(jnp.int32, logits.shape, 1)
    logits = jnp.where(lane < N_EXPERTS, logits, NEG)
    v1 = jnp.max(logits, axis=-1, keepdims=True)
    i1 = jnp.min(jnp.where(logits == v1, lane, LANE), axis=-1, keepdims=True)
    rest = jnp.where(lane == i1, NEG, logits)
    v2 = jnp.max(rest, axis=-1, keepdims=True)
    i2 = jnp.min(jnp.where(rest == v2, lane, LANE), axis=-1, keepdims=True)
    e = jnp.exp(v2 - v1)
    w1 = 1.0 / (1.0 + e)
    w2 = e / (1.0 + e)
    comb = jnp.where(lane == i1, w1, 0.0) + jnp.where(lane == i2, w2, 0.0)
    comb_ref[...] = comb
    combt_ref[...] = comb.T[0:N_EXPERTS, :]


def _router(x, gain, wr, tm):
    T = x.shape[0]
    const = lambda i: (0, 0)
    return pl.pallas_call(
        _router_kernel,
        grid=(pl.cdiv(T, tm),),
        in_specs=[pl.BlockSpec((tm, D_MODEL), lambda i: (i, 0)), pl.BlockSpec((1, D_MODEL), const),
                  pl.BlockSpec((D_MODEL, LANE), const)],
        out_specs=[pl.BlockSpec((tm, D_MODEL), lambda i: (i, 0)), pl.BlockSpec((tm, LANE), lambda i: (i, 0)),
                   pl.BlockSpec((N_EXPERTS, tm), lambda i: (0, i))],
        out_shape=[jax.ShapeDtypeStruct((T, D_MODEL), BF16), jax.ShapeDtypeStruct((T, LANE), F32),
                   jax.ShapeDtypeStruct((N_EXPERTS, T), F32)],
        compiler_params=_cparams(("parallel",)),
        name="moe_router",
    )(x, gain, wr)


def _moe_kernel(h_ref, x_ref, comb_ref, combt_ref, tri_ref, wg_ref, wu_ref, wd_ref, o_ref,
                rank_scr, hs_scr, ys_scr, *, TB, SB, CH, nf):
    e = pl.program_id(1)
    f = pl.program_id(2)

    nsub = TB // SB
    subs = [slice(j * SB, (j + 1) * SB) for j in range(nsub)]

    @pl.when((e == 0) & (f == 0))
    def _():
        o_ref[...] = x_ref[...]
        routed_all = jnp.where(combt_ref[...] > 0.0, 1.0, 0.0).astype(BF16)
        for sb in subs:
            rank_scr[:, sb] = _dot(routed_all[:, sb], tri_ref[...])

    routed = combt_ref[pl.ds(e, 1), :] > 0.0
    rank = rank_scr[pl.ds(e, 1), :]
    count = None
    for sb in subs:
        cj = jnp.sum(jnp.where(routed[:, sb], 1.0, 0.0)).astype(jnp.int32)
        count = cj if count is None else jnp.maximum(count, cj)
    def body(c, carry, CH):
        slot = lax.broadcasted_iota(jnp.int32, (CH, SB), 0).astype(F32)
        if isinstance(c, int):
            rows, base = pl.ds(c * nsub * CH, nsub * CH), float(c * CH)
        else:
            rows, base = pl.ds(pl.multiple_of(c * (nsub * CH), 16), nsub * CH), (c * CH).astype(F32)
        onehots = [jnp.where((rank[:, sb] - base == slot) & routed[:, sb], 1.0, 0.0).astype(BF16) for sb in subs]

        @pl.when(f == 0)
        def _():
            hs_scr[rows, :] = jnp.concatenate(
                [_dot(p, h_ref[sb, :]) for p, sb in zip(onehots, subs)], axis=0).astype(BF16)

        hs = hs_scr[rows, :]
        a = _silu(_dot(hs, wg_ref[0])) * _dot(hs, wu_ref[0])
        y = _dot(a.astype(BF16), wd_ref[0])

        @pl.when(f == 0)
        def _():
            ys_scr[rows, :] = y

        @pl.when(f > 0)
        def _():
            ys_scr[rows, :] += y

        @pl.when(f == nf - 1)
        def _():
            ys = ys_scr[rows, :].astype(BF16)
            lane = lax.broadcasted_iota(jnp.int32, (TB, LANE), 1)
            gate = jnp.sum(jnp.where(lane == e, comb_ref[...], 0.0), axis=-1, keepdims=True)
            for j, (p, sb) in enumerate(zip(onehots, subs)):
                o_ref[sb, :] += gate[sb] * _dot_tn(p, ys[j * CH:(j + 1) * CH])

        return carry

    lo = 0
    for ch in CH[:-1]:
        @pl.when((count > lo) & (count <= ch))
        def _(ch=ch):
            body(0, 0, ch)
        lo = ch

    @pl.when(count > lo)
    def _():
        lax.fori_loop(0, (count + CH[-1] - 1) // CH[-1], functools.partial(body, CH=CH[-1]), 0)


def _moe(x, h, comb, combt, w_up, wd, TB, SB, CH, nf):
    T = x.shape[0]
    dff = wd.shape[1]
    tf = dff // nf
    slots = pl.cdiv(SB, CH[-1]) * CH[-1] * (TB // SB)
    tri = jnp.triu(jnp.ones((SB, SB), BF16), k=1)
    once = pl.Buffered(1)
    return pl.pallas_call(
        functools.partial(_moe_kernel, TB=TB, SB=SB, CH=CH, nf=nf),
        grid=(T // TB, N_EXPERTS, nf),
        in_specs=[pl.BlockSpec((TB, D_MODEL), lambda i, e, f: (i, 0), pipeline_mode=once),
                  pl.BlockSpec((TB, D_MODEL), lambda i, e, f: (i, 0), pipeline_mode=once),
                  pl.BlockSpec((TB, LANE), lambda i, e, f: (i, 0)),
                  pl.BlockSpec((N_EXPERTS, TB), lambda i, e, f: (0, i)),
                  pl.BlockSpec((SB, SB), lambda i, e, f: (0, 0), pipeline_mode=once),
                  pl.BlockSpec((1, D_MODEL, tf), lambda i, e, f: (e, 0, f)),
                  pl.BlockSpec((1, D_MODEL, tf), lambda i, e, f: (e, 0, nf + f)),
                  pl.BlockSpec((1, tf, D_MODEL), lambda i, e, f: (e, f, 0))],
        out_specs=pl.BlockSpec((TB, D_MODEL), lambda i, e, f: (i, 0)),
        out_shape=jax.ShapeDtypeStruct((T, D_MODEL), F32),
        scratch_shapes=[pltpu.VMEM((N_EXPERTS, TB), F32), pltpu.VMEM((slots, D_MODEL), BF16),
                        pltpu.VMEM((slots, D_MODEL), F32)],
        compiler_params=_cparams(("parallel", "arbitrary", "arbitrary")),
        name="moe_top2",
    )(h, x, comb, combt, tri, w_up, w_up, wd)


def _prep_w_in(w):
    wt = w.T
    hq, hf, hi, hg = (wt[j * HG_WIDTH:(j + 1) * HG_WIDTH] for j in range(4))
    z = wt[_OFF_SSM:_OFF_SSM + SSM_WIDTH]
    xbc = wt[_OFF_SSM + SSM_WIDTH:_OFF_SSM + SSM_WIDTH + SSM_CONV_DIM]
    dt = wt[_OFF_ATT - SSM_HEADS:_OFF_ATT]
    att = wt[_OFF_ATT:_OFF_GATE]
    gates = wt[_OFF_GATE:]
    w16 = jnp.concatenate([gates, xbc, hq, hi, hg, z, att], axis=0).astype(BF16)
    w32 = jnp.concatenate([hf, dt, jnp.zeros((LANE - SSM_HEADS, D_MODEL), w.dtype)], axis=0).astype(BF16)
    return w16, w32


def _pad_lanes(v, n, value=0.0):
    v = v.reshape(1, -1).astype(F32)
    return jnp.pad(v, ((0, 0), (0, n - v.shape[1])), constant_values=value)


def _run_group(x, B, L, C_h, C_s, valid, NB, tm, layer_params, states, kv_bufs):
    depth = len(layer_params)
    st_out, kv_out = None, (None if kv_bufs is None else [])
    for l, P in enumerate(layer_params):
        st = states[l]
        a16, a32 = _norm_proj(x, P["mix_norm"], P["w16"], P["w32"], tm)
        a16b, a32b = a16.reshape(B, L, W16), a32.reshape(B, L, W32)
        y_hg, s_hg, y_ssm, s_ssm, s_conv = _recurrent_mixers(
            a16b, a32b, P["lb"], P["hg_norm"], st["hgrn_t"], st["conv"], P["conv_w"], P["conv_b"], P["dt_bias"],
            P["a_log"], P["d_skip"], P["ssm_norm"], st["ssm"], B, L, C_h, HG_SUB, valid, NB, l, depth, st_out)
        st_out = (s_hg, s_ssm, s_conv)
        y_hg, y_ssm = y_hg.reshape(B * L, HG_WIDTH), y_ssm.reshape(B * L, SSM_WIDTH)
        if kv_bufs is None:
            y_att, *kv_out = _attn_prompt(a16, P["gq"], P["gk"], P["bd"], B, L, l, depth, kv_out)
        else:
            y_att, *kv = _attn_sample(a16, kv_bufs, l, P["gq"], P["gk"], P["bd"], B)
            kv_out.append(kv)
        x = _merge(y_hg, y_ssm, y_att, a16, x, P["w_out_hg"], P["w_out_ssm"], P["w_out_att"], P["w_o"], tm)
        if l % 2 == 0:
            F = P["ffn"]
            nxt = layer_params[l + 1]["moe"] if l + 1 < len(layer_params) else {}
            pending = [k for k in ("w_up", "wd") if k + "_f32" in nxt and k not in nxt]
            cast = [nxt[k + "_f32"].reshape(-1, nxt[k + "_f32"].shape[-1]) for k in pending]
            x, done = _ffn(x, P["ffn_norm"], F["w_up"], F["wd"], min(tm, 256), cast)
            for k, w in zip(pending, done):
                nxt[k] = w.reshape(nxt[k + "_f32"].shape)
        else:
            F = P["moe"]
            h, comb, combt = _router(x, P["ffn_norm"], F["wr"], tm)
            x = _moe(x, h, comb, combt, F["w_up"], F["wd"], min(MOE_BLOCK, x.shape[0]), min(MOE_SUB, x.shape[0]), MOE_SLOTS, 2)
    return x, st_out, kv_out


def kernel(x_prompt, x_sample, state_hgrn, state_ssm, state_conv, cache_kv_w128, cache_kv_w512, cache_kv_w2048, w_in, mix_norm, hgrn_lb_logits, hgrn_norm, ssm_conv_w, ssm_conv_b, ssm_dt_bias, ssm_a_log, ssm_d, ssm_norm, att_q_norm, att_k_norm, w_out_hgrn, w_out_ssm, w_out_att, w_o, ffn_norm, w_ffn_up, w_ffn_down, w_router, w_moe_up, w_moe_down):
    depth = w_in.shape[0]
    Bp, Lp, _ = x_prompt.shape
    Bs, Ls, _ = x_sample.shape
    R = SAMPLE_ROWS
    for buf, w in zip((cache_kv_w128, cache_kv_w512, cache_kv_w2048), ATT_WINDOWS):
        assert buf.shape[2] == w, "sample attention assumes full window buffers"

    lb_soft = jax.nn.softmax(hgrn_lb_logits.astype(F32), axis=0)
    lb_table = jnp.cumsum(lb_soft, axis=0) - lb_soft[0]
    bd = jnp.kron(jnp.eye(ATT_HEADS, dtype=F32), jnp.full((ATT_HEAD_DIM, ATT_HEAD_DIM), 1.0 / ATT_HEAD_DIM, F32)).astype(BF16)

    layer_params = []
    for l in range(depth):
        w16, w32 = _prep_w_in(w_in[l])
        P = dict(
            w16=w16, w32=w32,
            mix_norm=mix_norm[l].reshape(1, -1), ffn_norm=ffn_norm[l].reshape(1, -1),
            lb=lb_table[l].reshape(1, -1), hg_norm=hgrn_norm[l].reshape(1, -1),
            conv_w=jnp.pad(ssm_conv_w[l], ((0, 8 - SSM_CONV), (0, 0))), conv_b=ssm_conv_b[l].reshape(1, -1),
            dt_bias=_pad_lanes(ssm_dt_bias[l], LANE), a_log=_pad_lanes(ssm_a_log[l], LANE),
            d_skip=jnp.repeat(ssm_d[l].astype(F32), SSM_HEAD_DIM).reshape(1, -1),
            ssm_norm=ssm_norm[l].reshape(1, -1),
            gq=jnp.tile(att_q_norm[l], (1, ATT_HEADS)), gk=jnp.tile(att_k_norm[l], (1, ATT_HEADS)), bd=bd,
            w_out_hg=w_out_hgrn[l].astype(BF16), w_out_ssm=w_out_ssm[l].astype(BF16),
            w_out_att=w_out_att[l].astype(BF16), w_o=w_o[l].astype(BF16),
        )
        if l % 2 == 0:
            P["ffn"] = dict(w_up=w_ffn_up[l // 2].astype(BF16), wd=w_ffn_down[l // 2].astype(BF16))
        else:
            P["moe"] = dict(wr=jnp.pad(w_router[l // 2].astype(F32), ((0, 0), (0, LANE - N_EXPERTS))),
                            w_up_f32=w_moe_up[l // 2], wd_f32=w_moe_down[l // 2])
        layer_params.append(P)

    zero_p = dict(hgrn_t=jnp.zeros((Bp, HG_WIDTH, HG_DIM), F32), ssm=jnp.zeros((Bp, SSM_WIDTH, SSM_STATE), F32),
                  conv=jnp.zeros((Bp, 8, SSM_CONV_DIM), F32))
    xp, st_p, kv_p = _run_group(x_prompt.reshape(Bp * Lp, D_MODEL), Bp, Lp, 64, 64, 64, 4, 512,
                                layer_params, [zero_p] * depth, None)

    xs = jnp.pad(x_sample, ((0, 0), (0, R - Ls), (0, 0))).reshape(Bs * R, D_MODEL)
    st_s = []
    for l in range(depth):
        st_s.append(dict(
            hgrn_t=jnp.swapaxes(state_hgrn[l], -1, -2).reshape(Bs, HG_WIDTH, HG_DIM),
            ssm=state_ssm[l].reshape(Bs, SSM_WIDTH, SSM_STATE),
            conv=jnp.pad(state_conv[l], ((0, 0), (8 - (SSM_CONV - 1), 0), (0, 0)))))
    kv_s = [jnp.transpose(c, (0, 1, 3, 4, 5, 2)).reshape(depth, Bs, 2 * ATT_WIDTH, c.shape[2])
            for c in (cache_kv_w128, cache_kv_w512, cache_kv_w2048)]
    xs, st_s, kv_s = _run_group(xs, Bs, R, R, R, Ls, 4, Bs * R, layer_params, st_s, kv_s)

    def states(st, B):
        s_hg, s_ssm, s_conv = st
        return (jnp.swapaxes(s_hg.reshape(depth, B, HG_HEADS, HG_DIM, HG_DIM), -1, -2),
                s_ssm.reshape(depth, B, SSM_HEADS, SSM_HEAD_DIM, SSM_STATE),
                s_conv[:, :, 8 - (SSM_CONV - 1):])

    kv_prompt = [jnp.transpose(kv.reshape(depth, Bp, 2, ATT_HEADS, ATT_HEAD_DIM, kv.shape[3]), (0, 1, 5, 2, 3, 4))
                 for kv in kv_p]
    kv_sample = [jnp.stack([kv_s[l][j][:, :Ls].reshape(Bs, Ls, 2, ATT_HEADS, ATT_HEAD_DIM) for l in range(depth)])
                 for j in range(3)]
    y_prompt = xp.reshape(Bp, Lp, D_MODEL)
    y_sample = xs.reshape(Bs, R, D_MODEL)[:, :Ls]
    return (y_prompt, y_sample, *states(st_p, Bp), *kv_prompt, *states(st_s, Bs), *kv_sample)
```

```python
import functools

import numpy as np
import jax
import jax.numpy as jnp
from jax import lax
from jax.experimental import pallas as pl
from jax.experimental.pallas import tpu as pltpu

F32 = jnp.float32
BF16 = jnp.bfloat16

D_MODEL = 1024
HG_HEADS, HG_DIM = 4, 128
HG_WIDTH = HG_HEADS * HG_DIM
SSM_HEADS, SSM_HEAD_DIM, SSM_GROUPS, SSM_STATE, SSM_CONV = 8, 64, 2, 128, 4
SSM_WIDTH = SSM_HEADS * SSM_HEAD_DIM
SSM_CONV_DIM = SSM_WIDTH + 2 * SSM_GROUPS * SSM_STATE
ATT_WINDOWS = (128, 512, 2048)
ATT_DILATIONS = (1, 4, 16)
ATT_ORDER = (2, 1, 0)
ATT_HEADS, ATT_HEAD_DIM = 4, 64
ATT_WIDTH = ATT_HEADS * ATT_HEAD_DIM
ATT_KEYS = 128
ATT_SCALE = ATT_HEAD_DIM ** -0.5
N_EXPERTS = 8
EPS = 1e-6
NEG = -1e30

_OFF_SSM = 4 * HG_WIDTH
_OFF_ATT = _OFF_SSM + SSM_WIDTH + SSM_CONV_DIM + SSM_HEADS
_OFF_GATE = _OFF_ATT + 3 * 3 * ATT_WIDTH

A16_GATE, A16_XBC, A16_HQ, A16_HI, A16_HG, A16_Z, A16_ATT = 0, 3072, 4096, 4608, 5120, 5632, 6144
W16 = 8448
A32_HF, A32_DT = 0, 512
W32 = 640

LANE = 128
SAMPLE_ROWS = 16
HG_SUB = 8
MOE_BLOCK = 1024
MOE_SUB = 512
MOE_SLOTS = (128, 144, 160)
VMEM_LIMIT = 56 * 2 ** 20


def _cparams(sem):
    return pltpu.CompilerParams(dimension_semantics=sem, vmem_limit_bytes=VMEM_LIMIT)


def _dot(a, b):
    return jnp.dot(a, b, preferred_element_type=F32)


def _dot_nt(a, b):
    return lax.dot_general(a, b, (((1,), (1,)), ((), ())), preferred_element_type=F32)


def _dot_tn(a, b):
    return lax.dot_general(a, b, (((0,), (0,)), ((), ())), preferred_element_type=F32)


def _split3(x):
    hi = x.astype(BF16)
    r = x - hi.astype(F32)
    mid = r.astype(BF16)
    lo = (r - mid.astype(F32)).astype(BF16)
    return hi, mid, lo


def _cumsum_rows(x, tril_bf16):
    hi, mid, lo = _split3(x)
    return _dot(tril_bf16, hi) + _dot(tril_bf16, mid) + _dot(tril_bf16, lo)


def _sigmoid(x):
    return 0.5 * jnp.tanh(0.5 * x) + 0.5


def _silu(x):
    return x * _sigmoid(x)


def _tril_bf16(n):
    r = lax.broadcasted_iota(jnp.int32, (n, n), 0)
    c = lax.broadcasted_iota(jnp.int32, (n, n), 1)
    return jnp.where(c <= r, 1.0, 0.0).astype(BF16)


def _norm_proj_kernel(x_ref, g_ref, w16_ref, w32_ref, o16_ref, o32_ref, *, chunk):
    x = x_ref[...]
    h = (x * lax.rsqrt(jnp.mean(x * x, axis=-1, keepdims=True) + EPS) * g_ref[...]).astype(BF16)
    for c0 in range(0, W16, chunk):
        o16_ref[:, c0:c0 + chunk] = _dot_nt(h, w16_ref[c0:c0 + chunk, :]).astype(BF16)
    o32_ref[...] = _dot_nt(h, w32_ref[...])


def _norm_proj(x, gain, w16, w32, tm):
    T = x.shape[0]
    const = lambda i: (0, 0)
    return pl.pallas_call(
        functools.partial(_norm_proj_kernel, chunk=768),
        grid=(pl.cdiv(T, tm),),
        in_specs=[
            pl.BlockSpec((tm, D_MODEL), lambda i: (i, 0)),
            pl.BlockSpec((1, D_MODEL), const),
            pl.BlockSpec((W16, D_MODEL), const, pipeline_mode=pl.Buffered(1)),
            pl.BlockSpec((W32, D_MODEL), const, pipeline_mode=pl.Buffered(1)),
        ],
        out_specs=[
            pl.BlockSpec((tm, W16), lambda i: (i, 0)),
            pl.BlockSpec((tm, W32), lambda i: (i, 0)),
        ],
        out_shape=[jax.ShapeDtypeStruct((T, W16), BF16), jax.ShapeDtypeStruct((T, W32), F32)],
        compiler_params=_cparams(("parallel",)),
        name="norm_proj",
    )(x, gain, w16, w32)


def _hgrn_tables(C, sub):
    t = np.arange(C)[:, None]
    u = np.arange(C)[None, :]
    blocks = [u <= t]
    levels = []
    l = sub
    while l < C:
        upper = (t % (2 * l)) >= l
        b = (t // (2 * l)) * (2 * l) + l - 1
        blocks.append(np.where(upper, (u > b) & (u <= t), (u > t) & (u <= b)))
        levels.append(l)
        l *= 2
    for s in range(sub):
        r = (t // sub) * sub + s
        blocks.append((u > r) & (u <= t))
    sel = [u == (t // sub) * sub + s for s in range(sub)]
    d = np.concatenate(blocks, 0).astype(np.float32)
    sl = np.concatenate(sel, 0).astype(np.float32)
    table = np.block([[d, d, d, np.zeros_like(d)], [np.zeros((sl.shape[0], 3 * C), np.float32), sl]])
    return jnp.asarray(table, BF16), tuple(levels)


def _hgrn_chunk(q_ref, f_ref, i_ref, g_ref, lb_ref, gn_ref, tab_ref, y_ref, st_scr, *, C, sub, levels, valid):
    lb = lb_ref[...]
    fgate = lb + (1.0 - lb) * jax.nn.sigmoid(f_ref[...])
    logf = jnp.log(fgate)
    kk = 1.0 - fgate
    rows = lax.broadcasted_iota(jnp.int32, (C, HG_WIDTH), 0)
    if valid < C:
        logf = jnp.where(rows < valid, logf, 0.0)
        kk = jnp.where(rows < valid, kk, 0.0)
    qa = _silu(q_ref[...].astype(F32))
    va = i_ref[...].astype(F32)
    ga = g_ref[...].astype(F32)

    ed = _dot(tab_ref[...], jnp.concatenate([*_split3(logf), kk.astype(BF16)], axis=0))
    nblk = 1 + len(levels) + sub
    blk = lambda j: ed[j * C:(j + 1) * C]
    G = blk(0)
    ksel = ed[nblk * C:]

    row_c = lax.broadcasted_iota(jnp.int32, (C, C), 0)
    lane_c = lax.broadcasted_iota(jnp.int32, (C, C), 1)
    off_c = lane_c - (row_c // sub) * sub

    A = [jnp.zeros((C, C), F32) for _ in range(HG_HEADS)]
    for s in range(sub):
        prod = qa * ksel[s * C:(s + 1) * C] * jnp.exp(blk(1 + len(levels) + s))
        for h in range(HG_HEADS):
            col = jnp.sum(prod[:, h * HG_DIM:(h + 1) * HG_DIM], axis=-1, keepdims=True)
            A[h] = jnp.where(off_c == s, col, A[h])
    A = [jnp.where(lane_c <= row_c, a, 0.0) for a in A]

    for li, l in enumerate(levels):
        upper = (rows % (2 * l)) >= l
        scaled = jnp.where(upper, qa, kk) * jnp.exp(blk(1 + li))
        qt = jnp.where(upper, scaled, 0.0).astype(BF16)
        kt = jnp.where(upper, 0.0, scaled).astype(BF16)
        same = (row_c // (2 * l)) == (lane_c // (2 * l))
        for h in range(HG_HEADS):
            sl = slice(h * HG_DIM, (h + 1) * HG_DIM)
            a_l = _dot_nt(qt[:, sl], kt[:, sl])
            A[h] = A[h] + (jnp.where(same, a_l, 0.0) if 2 * l < C else a_l)

    outs = []
    for h in range(HG_HEADS):
        sl = slice(h * HG_DIM, (h + 1) * HG_DIM)
        Gh, qh, kh = G[:, sl], qa[:, sl], kk[:, sl]
        vb = va[:, sl].astype(BF16)
        st = st_scr[sl, :]
        o = _dot(A[h].astype(BF16), vb) + _dot_nt((qh * jnp.exp(Gh)).astype(BF16), st.astype(BF16))
        g_last = Gh[C - 1:C]
        kdec = kh * jnp.exp(g_last - Gh)
        st_scr[sl, :] = jnp.exp(g_last) * st + _dot_tn(vb, kdec.astype(BF16))
        on = o * lax.rsqrt(jnp.mean(o * o, axis=-1, keepdims=True) + EPS) * gn_ref[...]
        outs.append(on * _silu(ga[:, sl]))
    y_ref[...] = jnp.concatenate(outs, axis=1).astype(BF16)


def _ssd_chunk(xbc_ref, z_ref, dt_ref, cw_ref, cb_ref, dtb_ref, alog_ref, dsk_ref, sn_ref,
               y_ref, ubuf, hs, *, C, valid):
    ubuf[8:8 + C, :] = xbc_ref[...].astype(F32)
    acc = cb_ref[...] + ubuf[5:5 + C, :] * cw_ref[0:1, :]
    for j in range(1, SSM_CONV):
        acc = acc + ubuf[5 + j:5 + j + C, :] * cw_ref[j:j + 1, :]
    xc = _silu(acc)
    ubuf[0:8, :] = ubuf[valid:valid + 8, :]

    x = dt_ref[...] + dtb_ref[...]
    dt = jnp.maximum(x, 0.0) + jnp.log1p(jnp.exp(-jnp.abs(x)))
    if valid < C:
        rows = lax.broadcasted_iota(jnp.int32, (C, LANE), 0)
        dt = jnp.where(rows < valid, dt, 0.0)
    a = -jnp.exp(alog_ref[...])
    cum = _cumsum_rows(dt * a, _tril_bf16(C))
    cum_t = cum.T
    dt_t = dt.T

    ri = lax.broadcasted_iota(jnp.int32, (C, C), 0)
    cj = lax.broadcasted_iota(jnp.int32, (C, C), 1)
    tril = cj <= ri
    lane = lax.broadcasted_iota(jnp.int32, (C, LANE), 1)
    srow = lax.broadcasted_iota(jnp.int32, (LANE, LANE), 0)
    bm = [xc[:, SSM_WIDTH + g * SSM_STATE:SSM_WIDTH + (g + 1) * SSM_STATE].astype(BF16) for g in range(SSM_GROUPS)]
    off_c = SSM_WIDTH + SSM_GROUPS * SSM_STATE
    cm = [xc[:, off_c + g * SSM_STATE:off_c + (g + 1) * SSM_STATE].astype(BF16) for g in range(SSM_GROUPS)]
    cb = [_dot_nt(cm[g], bm[g]) for g in range(SSM_GROUPS)]

    ys = []
    for p in range(SSM_HEADS // 2):
        g = (2 * p) // (SSM_HEADS // SSM_GROUPS)
        xp = xc[:, p * LANE:(p + 1) * LANE]
        yp = None
        colw = []
        for k, hd in enumerate((2 * p, 2 * p + 1)):
            colb = jnp.broadcast_to(cum[:, hd:hd + 1], (C, LANE))
            rowb = jnp.broadcast_to(cum_t[hd:hd + 1, :], (C, C))
            dtr = jnp.broadcast_to(dt_t[hd:hd + 1, :], (C, C))
            w = cb[g] * jnp.exp(jnp.where(tril, colb[:, :C] - rowb, NEG)) * dtr
            half = (lane < SSM_HEAD_DIM) if k == 0 else (lane >= SSM_HEAD_DIM)
            t = _dot(w.astype(BF16), jnp.where(half, xp, 0.0).astype(BF16))
            yp = t if yp is None else yp + t
            colw.append((colb, jnp.broadcast_to(dt[:, hd:hd + 1], (C, LANE))))
        first = lane < SSM_HEAD_DIM
        cum_e = jnp.where(first, colw[0][0], colw[1][0])
        dt_e = jnp.where(first, colw[0][1], colw[1][1])
        hp = hs[p * LANE:(p + 1) * LANE, :]
        yp = yp + jnp.exp(cum_e) * _dot_nt(cm[g], hp.astype(BF16))
        c_last = cum_e[C - 1:C, :]
        w_in = jnp.exp(c_last - cum_e) * dt_e
        dec = jnp.where(srow < SSM_HEAD_DIM,
                        jnp.broadcast_to(cum[C - 1:C, 2 * p:2 * p + 1], (LANE, LANE)),
                        jnp.broadcast_to(cum[C - 1:C, 2 * p + 1:2 * p + 2], (LANE, LANE)))
        hs[p * LANE:(p + 1) * LANE, :] = jnp.exp(dec) * hp + _dot_tn((xp * w_in).astype(BF16), bm[g])
        ys.append(yp)
    y = jnp.concatenate(ys, axis=1) + dsk_ref[...] * xc[:, :SSM_WIDTH]
    yz = y * _silu(z_ref[...].astype(F32))
    y_ref[...] = (yz * lax.rsqrt(jnp.mean(yz * yz, axis=-1, keepdims=True) + EPS) * sn_ref[...]).astype(BF16)


def _rec_kernel(q_ref, f_ref, i_ref, g_ref, lb_ref, gn_ref, s0_ref, tab_ref,
                xbc_ref, z_ref, dt_ref, pre_ref, cw_ref, cb_ref, dtb_ref, alog_ref, dsk_ref, sn_ref, h0_ref,
                *rest, NB, nchunks, first, hg_kw, ssd_kw):
    yh_ref, sout_ref, ys_ref, hout_ref, cout_ref, st_scr, ubuf, hs = rest[-8:]
    ci = pl.program_id(1)

    @pl.when(ci == 0)
    def _():
        st_scr[...] = s0_ref[...]
        ubuf[:, 0:8, :] = pre_ref[...]
        hs[...] = h0_ref[...]

    for j in range(NB):
        _hgrn_chunk(q_ref.at[j], f_ref.at[j], i_ref.at[j], g_ref.at[j], lb_ref, gn_ref, tab_ref,
                    yh_ref.at[j], st_scr.at[j], **hg_kw)
        _ssd_chunk(xbc_ref.at[j], z_ref.at[j], dt_ref.at[j], cw_ref, cb_ref, dtb_ref, alog_ref, dsk_ref, sn_ref,
                   ys_ref.at[j], ubuf.at[j], hs.at[j], **ssd_kw)

    @pl.when(ci == nchunks - 1)
    def _():
        for d in range(sout_ref.shape[0]):
            sout_ref[d] = st_scr[...]
            hout_ref[d] = hs[...]
            cout_ref[d] = ubuf[:, 0:8, :]


def _recurrent_mixers(a16, a32, lb, gn, s0t, prefix, cw, cb, dtb, alog, dsk, sn, h0, B, L, C, sub, valid, NB,
                      layer, depth, prev):
    nch = L // C
    table, levels = _hgrn_tables(C, sub)
    const = lambda b, c: (0, 0)
    bat = lambda b, c: (b, 0, 0)
    blk = lambda w, col: pl.BlockSpec((NB, C, w), lambda b, c: (b, c, col))
    first = prev is None
    slot = (lambda rows, w: pl.BlockSpec((depth, NB, rows, w), lambda b, c: (0, b, 0, 0))) if first else \
           (lambda rows, w: pl.BlockSpec((1, NB, rows, w), lambda b, c: (layer, b, 0, 0)))
    return pl.pallas_call(
        functools.partial(_rec_kernel, NB=NB, nchunks=nch, first=first,
                          hg_kw=dict(C=C, sub=sub, levels=levels, valid=valid), ssd_kw=dict(C=C, valid=valid)),
        grid=(B // NB, nch),
        in_specs=[
            blk(HG_WIDTH, A16_HQ // HG_WIDTH), blk(HG_WIDTH, A32_HF // HG_WIDTH),
            blk(HG_WIDTH, A16_HI // HG_WIDTH), blk(HG_WIDTH, A16_HG // HG_WIDTH),
            pl.BlockSpec((1, HG_WIDTH), const),
            pl.BlockSpec((1, HG_DIM), const),
            pl.BlockSpec((NB, HG_WIDTH, HG_DIM), bat),
            pl.BlockSpec(table.shape, const),
            blk(SSM_CONV_DIM, A16_XBC // SSM_CONV_DIM), blk(SSM_WIDTH, A16_Z // SSM_WIDTH), blk(LANE, A32_DT // LANE),
            pl.BlockSpec((NB, 8, SSM_CONV_DIM), bat),
            pl.BlockSpec((8, SSM_CONV_DIM), const),
            pl.BlockSpec((1, SSM_CONV_DIM), const),
            pl.BlockSpec((1, LANE), const),
            pl.BlockSpec((1, LANE), const),
            pl.BlockSpec((1, SSM_WIDTH), const),
            pl.BlockSpec((1, SSM_WIDTH), const),
            pl.BlockSpec((NB, SSM_WIDTH, SSM_STATE), bat),
        ] + ([] if first else [pl.BlockSpec(memory_space=pl.ANY)] * 3),
        out_specs=[
            blk(HG_WIDTH, 0), slot(HG_WIDTH, HG_DIM),
            blk(SSM_WIDTH, 0), slot(SSM_WIDTH, SSM_STATE), slot(8, SSM_CONV_DIM),
        ],
        out_shape=[jax.ShapeDtypeStruct((B, L, HG_WIDTH), BF16),
                   jax.ShapeDtypeStruct((depth, B, HG_WIDTH, HG_DIM), F32),
                   jax.ShapeDtypeStruct((B, L, SSM_WIDTH), BF16),
                   jax.ShapeDtypeStruct((depth, B, SSM_WIDTH, SSM_STATE), F32),
                   jax.ShapeDtypeStruct((depth, B, 8, SSM_CONV_DIM), F32)],
        input_output_aliases={} if first else {19: 1, 20: 3, 21: 4},
        scratch_shapes=[pltpu.VMEM((NB, HG_WIDTH, HG_DIM), F32), pltpu.VMEM((NB, C + 8, SSM_CONV_DIM), F32),
                        pltpu.VMEM((NB, SSM_WIDTH, SSM_STATE), F32)],
        compiler_params=_cparams(("parallel", "arbitrary")),
        name="hgrn2_ssd",
    )(a16, a32, a16, a16, lb, gn, s0t, table, a16, a16, a32, prefix, cw, cb, dtb, alog, dsk, sn, h0,
      *([] if first else prev))


def _head_norm(x, gain, bd):
    ms = _dot((x * x).astype(BF16), bd)
    return x * (lax.rsqrt(ms + EPS) * gain)


def _head_masks(rows):
    lane = lax.broadcasted_iota(jnp.int32, (rows, ATT_WIDTH), 1)
    return [(lane >= h * ATT_HEAD_DIM) & (lane < (h + 1) * ATT_HEAD_DIM) for h in range(ATT_HEADS)]


def _stack_heads(q, masks):
    return jnp.concatenate([jnp.where(m, q, 0.0) for m in masks], axis=0).astype(BF16)


def _unstack_heads(pv, m, d, masks):
    R = pv.shape[0] // ATT_HEADS
    acc, mx, den = pv[0:R], jnp.broadcast_to(m[0:R], (R, ATT_WIDTH)), jnp.broadcast_to(d[0:R], (R, ATT_WIDTH))
    for h in range(1, ATT_HEADS):
        sl = slice(h * R, (h + 1) * R)
        acc = jnp.where(masks[h], pv[sl], acc)
        mx = jnp.where(masks[h], m[sl], mx)
        den = jnp.where(masks[h], d[sl], den)
    return acc, mx, den


def _attend(qb, kb, vb, bias4, masks):
    s = _dot_nt(_stack_heads(qb, masks), kb) + bias4
    m = jnp.max(s, axis=-1, keepdims=True)
    p = jnp.exp(s - m)
    d = jnp.sum(p, axis=-1, keepdims=True)
    return _unstack_heads(_dot(p.astype(BF16), vb), m, d, masks)


def _ld2(pair, s):
    return jnp.concatenate([pair[0][s, :], pair[1][s, :]], axis=1)


def _st2(pair, s, val):
    pair[0][s, :] = val[:, :LANE]
    pair[1][s, :] = val[:, LANE:]


def _attn_prompt_kernel(att_ref, gq_ref, gk_ref, bd_ref, *rest, L):
    y_ref, kv0_ref, kv1_ref, kv2_ref = rest[-16:-12]
    scr = rest[-12:]
    qn, kn, vv, ya, ma, za = (scr[2 * j:2 * j + 2] for j in range(6))
    g = pl.program_id(1)
    RB = 256
    QB = ATT_KEYS

    def group(step, gi, dil, keep, kv_ref):
        M = L // dil
        nb = M // QB
        KW = min(2 * QB, M)
        bd = bd_ref[...]

        def norm_body(i, carry):
            r0 = pl.multiple_of(i * RB, RB)
            x = att_ref[pl.ds(r0, RB), :].astype(F32)
            _st2(qn, pl.ds(r0, RB), _head_norm(x[:, :ATT_WIDTH], gq_ref[gi:gi + 1, :] * ATT_SCALE, bd))
            _st2(kn, pl.ds(r0, RB), _head_norm(x[:, ATT_WIDTH:2 * ATT_WIDTH], gk_ref[gi:gi + 1, :], bd))
            _st2(vv, pl.ds(r0, RB), x[:, 2 * ATT_WIDTH:])
            return carry

        lax.fori_loop(0, L // RB, norm_body, 0)
        cw = min(keep, RB)
        for c0 in range(0, keep, cw):
            rows = slice(L - keep + c0, L - keep + c0 + cw)
            for p, half in enumerate((kn[0], kn[1], vv[0], vv[1])):
                piece = half[rows, :].T
                for d in range(kv_ref.shape[0]):
                    kv_ref[d, 0, p * LANE:(p + 1) * LANE, c0:c0 + cw] = piece

        masks = _head_masks(QB)
        iq = lax.broadcasted_iota(jnp.int32, (ATT_HEADS * QB, KW), 0) & (QB - 1)
        ik = lax.broadcasted_iota(jnp.int32, (ATT_HEADS * QB, KW), 1)

        def run_blocks(first, count, lead):
            dist = (iq - ik) if lead else (QB + iq - ik)
            bias4 = jnp.where((dist >= 0) & (dist <= ATT_KEYS), 0.0, NEG)
            unroll = max(u for u in (4, 3, 2, 1) if count % u == 0)

            def blk_body(j, carry):
                loaded = []
                for u in range(unroll):
                    it = first + j * unroll + u
                    n = it // dil
                    r = it - n * dil
                    k0 = 0 if lead else (n - 1) * QB
                    if dil == 1:
                        qs = pl.ds(pl.multiple_of(n * QB, QB), QB)
                        ks = pl.ds(pl.multiple_of(k0, QB), KW)
                    else:
                        qs = pl.ds(n * QB * dil + r, QB, stride=dil)
                        ks = pl.ds(k0 * dil + r, KW, stride=dil)
                    old = None if step == 0 else (_ld2(ya, qs), _ld2(ma, qs), _ld2(za, qs))
                    loaded.append((qs, _ld2(qn, qs), _ld2(kn, ks).astype(BF16), _ld2(vv, ks).astype(BF16), old))
                for qs, qb, kb, vb, old in loaded:
                    acc, mx, den = _attend(qb, kb, vb, bias4, masks)
                    if step > 0:
                        yo, mo, zo = old
                        mn = jnp.maximum(mo, mx)
                        eo = jnp.exp(mo - mn)
                        en = jnp.exp(mx - mn)
                        acc, mx, den = yo * eo + acc * en, mn, zo * eo + den * en
                    _st2(ya, qs, acc)
                    _st2(ma, qs, mx)
                    _st2(za, qs, den)
                return carry

            lax.fori_loop(0, count // unroll, blk_body, 0)

        run_blocks(0, dil, True)
        if nb > 1:
            run_blocks(dil, dil * (nb - 1), False)

    for step, gi in enumerate(ATT_ORDER):
        @pl.when(g == step)
        def _(step=step, gi=gi):
            group(step, gi, ATT_DILATIONS[gi], min(ATT_WINDOWS[gi], L), (kv0_ref, kv1_ref, kv2_ref)[gi])

    @pl.when(g == 2)
    def _():
        def out_body(i, carry):
            r0 = pl.multiple_of(i * RB, RB)
            y_ref[pl.ds(r0, RB), :] = (_ld2(ya, pl.ds(r0, RB)) / _ld2(za, pl.ds(r0, RB))).astype(BF16)
            return carry
        lax.fori_loop(0, L // RB, out_body, 0)


def _attn_prompt(a16, gq, gk, bd, B, L, layer, depth, prev):
    keeps = [min(w, L) for w in ATT_WINDOWS]
    const = lambda b, g: (0, 0)
    first = prev is None
    slot = (lambda k: pl.BlockSpec((depth, 1, 2 * ATT_WIDTH, k), lambda b, g: (0, b, 0, 0))) if first else \
           (lambda k: pl.BlockSpec((1, 1, 2 * ATT_WIDTH, k), lambda b, g: (layer, b, 0, 0)))
    return pl.pallas_call(
        functools.partial(_attn_prompt_kernel, L=L),
        grid=(B, 3),
        in_specs=[
            pl.BlockSpec((L, 3 * ATT_WIDTH), lambda b, g: (b, A16_ATT // (3 * ATT_WIDTH) + 2 - g)),
            pl.BlockSpec((3, ATT_WIDTH), const),
            pl.BlockSpec((3, ATT_WIDTH), const),
            pl.BlockSpec((ATT_WIDTH, ATT_WIDTH), const),
        ] + ([] if first else [pl.BlockSpec(memory_space=pl.ANY)] * 3),
        out_specs=[pl.BlockSpec((L, ATT_WIDTH), lambda b, g: (b, 0))] + [slot(k) for k in keeps],
        out_shape=[jax.ShapeDtypeStruct((B * L, ATT_WIDTH), BF16)]
        + [jax.ShapeDtypeStruct((depth, B, 2 * ATT_WIDTH, k), F32) for k in keeps],
        input_output_aliases={} if first else {4: 1, 5: 2, 6: 3},
        scratch_shapes=[pltpu.VMEM((L, LANE), F32) for _ in range(12)],
        compiler_params=_cparams(("parallel", "arbitrary")),
        name="attn_prompt",
    )(a16, gq, gk, bd, *([] if first else prev))


def _attn_sample_kernel(a0_ref, a1_ref, a2_ref, b0_ref, b1_ref, b2_ref, gq_ref, gk_ref, bd_ref,
                        y_ref, k0_ref, k1_ref, k2_ref):
    R = SAMPLE_ROWS
    bd = bd_ref[...]
    masks = _head_masks(R)
    ya = ma = za = None
    for gi, (a_ref, b_ref, k_ref) in enumerate(((a0_ref, b0_ref, k0_ref), (a1_ref, b1_ref, k1_ref),
                                                (a2_ref, b2_ref, k2_ref))):
        dil = ATT_DILATIONS[gi]
        wb = b_ref.shape[3]
        x = a_ref[...].astype(F32)
        q = _head_norm(x[:, :ATT_WIDTH], gq_ref[gi:gi + 1, :] * ATT_SCALE, bd)
        k = _head_norm(x[:, ATT_WIDTH:2 * ATT_WIDTH], gk_ref[gi:gi + 1, :], bd)
        v = x[:, 2 * ATT_WIDTH:]
        k_ref[0] = jnp.concatenate([k[:8], v[:8]], axis=1)
        k_t = b_ref[0, 0, 0:ATT_WIDTH, :].astype(BF16)
        v_t = b_ref[0, 0, ATT_WIDTH:2 * ATT_WIDTH, :].astype(BF16)
        kb, vb = k.astype(BF16), v.astype(BF16)

        def bias(dist):
            return jnp.where((dist >= 0) & (dist <= ATT_KEYS * dil) & ((dist & (dil - 1)) == 0), 0.0, NEG)

        R4 = ATT_HEADS * R
        row_b = lax.broadcasted_iota(jnp.int32, (R4, wb), 0) & (R - 1)
        row_n = lax.broadcasted_iota(jnp.int32, (R4, R), 0) & (R - 1)
        qs = _stack_heads(q, masks)
        s_b = _dot(qs, k_t) + bias(wb + row_b - lax.broadcasted_iota(jnp.int32, (R4, wb), 1))
        s_n = _dot_nt(qs, kb) + bias(row_n - lax.broadcasted_iota(jnp.int32, (R4, R), 1))
        m = jnp.maximum(jnp.max(s_b, axis=-1, keepdims=True), jnp.max(s_n, axis=-1, keepdims=True))
        p_b = jnp.exp(s_b - m)
        p_n = jnp.exp(s_n - m)
        d = jnp.sum(p_b, axis=-1, keepdims=True) + jnp.sum(p_n, axis=-1, keepdims=True)
        pv = _dot_nt(p_b.astype(BF16), v_t) + _dot(p_n.astype(BF16), vb)
        acc, mx, den = _unstack_heads(pv, m, d, masks)
        if gi == 0:
            ya, ma, za = acc, mx, den
        else:
            mn = jnp.maximum(ma, mx)
            eo = jnp.exp(ma - mn)
            en = jnp.exp(mx - mn)
            ya, za, ma = ya * eo + acc * en, za * eo + den * en, mn
    y_ref[...] = (ya / za).astype(BF16)


def _attn_sample(a16, bufs, layer, gq, gk, bd, B):
    R = SAMPLE_ROWS
    const = lambda b: (0, 0)
    bat = lambda b: (b, 0, 0)
    base = A16_ATT // (3 * ATT_WIDTH)
    return pl.pallas_call(
        _attn_sample_kernel,
        grid=(B,),
        in_specs=[pl.BlockSpec((R, 3 * ATT_WIDTH), functools.partial(lambda b, j: (b, j), j=base + g))
                  for g in range(3)]
        + [pl.BlockSpec((1, 1, 2 * ATT_WIDTH, buf.shape[3]), lambda b: (layer, b, 0, 0)) for buf in bufs]
        + [pl.BlockSpec((3, ATT_WIDTH), const), pl.BlockSpec((3, ATT_WIDTH), const),
           pl.BlockSpec((ATT_WIDTH, ATT_WIDTH), const)],
        out_specs=[pl.BlockSpec((R, ATT_WIDTH), lambda b: (b, 0))]
        + [pl.BlockSpec((1, 8, 2 * ATT_WIDTH), bat) for _ in range(3)],
        out_shape=[jax.ShapeDtypeStruct((B * R, ATT_WIDTH), BF16)]
        + [jax.ShapeDtypeStruct((B, 8, 2 * ATT_WIDTH), F32) for _ in range(3)],
        compiler_params=_cparams(("parallel",)),
        name="attn_sample",
    )(a16, a16, a16, *bufs, gq, gk, bd)


def _merge_kernel(yh_ref, ys_ref, ya_ref, gt_ref, x_ref, wh_ref, ws_ref, wa_ref, wo_ref, o_ref):
    gt = _sigmoid(gt_ref[...].astype(F32))
    mixed = (gt[:, 0:D_MODEL] * _dot(yh_ref[...], wh_ref[...])
             + gt[:, D_MODEL:2 * D_MODEL] * _dot(ys_ref[...], ws_ref[...])
             + gt[:, 2 * D_MODEL:] * _dot(ya_ref[...], wa_ref[...]))
    o_ref[...] = x_ref[...] + _dot(mixed.astype(BF16), wo_ref[...])


def _merge(y_hg, y_ssm, y_att, a16, x, wh, ws, wa, wo, tm):
    T = x.shape[0]
    const = lambda i: (0, 0)
    rowb = lambda w: pl.BlockSpec((tm, w), lambda i: (i, 0))
    wsp = lambda w: pl.BlockSpec(w.shape, const, pipeline_mode=pl.Buffered(1))
    return pl.pallas_call(
        _merge_kernel,
        grid=(pl.cdiv(T, tm),),
        in_specs=[rowb(HG_WIDTH), rowb(SSM_WIDTH), rowb(ATT_WIDTH),
                  pl.BlockSpec((tm, 3 * D_MODEL), lambda i: (i, A16_GATE // (3 * D_MODEL))),
                  rowb(D_MODEL), wsp(wh), wsp(ws), wsp(wa), wsp(wo)],
        out_specs=rowb(D_MODEL),
        out_shape=jax.ShapeDtypeStruct((T, D_MODEL), F32),
        compiler_params=_cparams(("parallel",)),
        name="merge_out_proj",
    )(y_hg, y_ssm, y_att, a16, x, wh, ws, wa, wo)


def _ffn_kernel(x_ref, g_ref, wg_ref, wu_ref, wd_ref, *rest):
    n = (len(rest) - 1) // 2
    cast_in, o_ref, cast_out = rest[:n], rest[n], rest[n + 1:]
    x = x_ref[...]
    h = (x * lax.rsqrt(jnp.mean(x * x, axis=-1, keepdims=True) + EPS) * g_ref[...]).astype(BF16)
    a = _silu(_dot(h, wg_ref[...])) * _dot(h, wu_ref[...])
    o_ref[...] = x + _dot(a.astype(BF16), wd_ref[...])
    for src, dst in zip(cast_in, cast_out):
        dst[...] = src[...].astype(BF16)


def _ffn(x, gain, w_up, wd, tm, cast=()):
    T = x.shape[0]
    dff = wd.shape[0]
    steps = pl.cdiv(T, tm)
    const = lambda i: (0, 0)
    wsp = lambda w: pl.BlockSpec(w.shape, const, pipeline_mode=pl.Buffered(1))
    slab = [pl.BlockSpec((w.shape[0] // steps, w.shape[1]), lambda i: (i, 0)) for w in cast]
    for w in cast:
        assert w.shape[0] % (16 * steps) == 0
    outs = pl.pallas_call(
        _ffn_kernel,
        grid=(steps,),
        in_specs=[pl.BlockSpec((tm, D_MODEL), lambda i: (i, 0)), pl.BlockSpec((1, D_MODEL), const),
                  pl.BlockSpec((D_MODEL, dff), const, pipeline_mode=pl.Buffered(1)),
                  pl.BlockSpec((D_MODEL, dff), lambda i: (0, 1), pipeline_mode=pl.Buffered(1)),
                  wsp(wd)] + slab,
        out_specs=[pl.BlockSpec((tm, D_MODEL), lambda i: (i, 0))] + slab,
        out_shape=[jax.ShapeDtypeStruct((T, D_MODEL), F32)] + [jax.ShapeDtypeStruct(w.shape, BF16) for w in cast],
        compiler_params=_cparams(("parallel",)),
        name="ffn_dense",
    )(x, gain, w_up, w_up, wd, *cast)
    return outs[0], outs[1:]


def _router_kernel(x_ref, g_ref, wr_ref, h_ref, comb_ref, combt_ref):
    x = x_ref[...]
    h = x * lax.rsqrt(jnp.mean(x * x, axis=-1, keepdims=True) + EPS) * g_ref[...]
    hb = h.astype(BF16)
    h_ref[...] = hb
    h_lo = (h - hb.astype(F32)).astype(BF16)
    w = wr_ref[...]
    w_hi = w.astype(BF16)
    w_lo = (w - w_hi.astype(F32)).astype(BF16)
    logits = _dot(hb, w_hi) + _dot(hb, w_lo) + _dot(h_lo, w_hi)
    lane = lax.broadcasted_iota(jnp.int32, logits.shape, 1)
    logits = jnp.where(lane < N_EXPERTS, logits, NEG)
    v1 = jnp.max(logits, axis=-1, keepdims=True)
    i1 = jnp.min(jnp.where(logits == v1, lane, LANE), axis=-1, keepdims=True)
    rest = jnp.where(lane == i1, NEG, logits)
    v2 = jnp.max(rest, axis=-1, keepdims=True)
    i2 = jnp.min(jnp.where(rest == v2, lane, LANE), axis=-1, keepdims=True)
    e = jnp.exp(v2 - v1)
    w1 = 1.0 / (1.0 + e)
    w2 = e / (1.0 + e)
    comb = jnp.where(lane == i1, w1, 0.0) + jnp.where(lane == i2, w2, 0.0)
    comb_ref[...] = comb
    combt_ref[...] = comb.T[0:N_EXPERTS, :]


def _router(x, gain, wr, tm):
    T = x.shape[0]
    const = lambda i: (0, 0)
    return pl.pallas_call(
        _router_kernel,
        grid=(pl.cdiv(T, tm),),
        in_specs=[pl.BlockSpec((tm, D_MODEL), lambda i: (i, 0)), pl.BlockSpec((1, D_MODEL), const),
                  pl.BlockSpec((D_MODEL, LANE), const)],
        out_specs=[pl.BlockSpec((tm, D_MODEL), lambda i: (i, 0)), pl.BlockSpec((tm, LANE), lambda i: (i, 0)),
                   pl.BlockSpec((N_EXPERTS, tm), lambda i: (0, i))],
        out_shape=[jax.ShapeDtypeStruct((T, D_MODEL), BF16), jax.ShapeDtypeStruct((T, LANE), F32),
                   jax.ShapeDtypeStruct((N_EXPERTS, T), F32)],
        compiler_params=_cparams(("parallel",)),
        name="moe_router",
    )(x, gain, wr)


def _moe_kernel(h_ref, x_ref, comb_ref, combt_ref, tri_ref, wg_ref, wu_ref, wd_ref, o_ref,
                rank_scr, hs_scr, ys_scr, *, TB, SB, CH, nf):
    e = pl.program_id(1)
    f = pl.program_id(2)

    nsub = TB // SB
    subs = [slice(j * SB, (j + 1) * SB) for j in range(nsub)]

    @pl.when((e == 0) & (f == 0))
    def _():
        o_ref[...] = x_ref[...]
        routed_all = jnp.where(combt_ref[...] > 0.0, 1.0, 0.0).astype(BF16)
        for sb in subs:
            rank_scr[:, sb] = _dot(routed_all[:, sb], tri_ref[...])

    routed = combt_ref[pl.ds(e, 1), :] > 0.0
    rank = rank_scr[pl.ds(e, 1), :]
    count = None
    for sb in subs:
        cj = jnp.sum(jnp.where(routed[:, sb], 1.0, 0.0)).astype(jnp.int32)
        count = cj if count is None else jnp.maximum(count, cj)
    def body(c, carry, CH):
        slot = lax.broadcasted_iota(jnp.int32, (CH, SB), 0).astype(F32)
        if isinstance(c, int):
            rows, base = pl.ds(c * nsub * CH, nsub * CH), float(c * CH)
        else:
            rows, base = pl.ds(pl.multiple_of(c * (nsub * CH), 16), nsub * CH), (c * CH).astype(F32)
        onehots = [jnp.where((rank[:, sb] - base == slot) & routed[:, sb], 1.0, 0.0).astype(BF16) for sb in subs]

        def expert(hs):
            a = _silu(_dot(hs, wg_ref[0])) * _dot(hs, wu_ref[0])
            return _dot(a.astype(BF16), wd_ref[0])

        @pl.when(f == 0)
        def _():
            hs = jnp.concatenate([_dot(p, h_ref[sb, :]) for p, sb in zip(onehots, subs)], axis=0).astype(BF16)
            hs_scr[rows, :] = hs
            ys_scr[rows, :] = expert(hs)

        if nf > 2:
            @pl.when((f > 0) & (f < nf - 1))
            def _():
                ys_scr[rows, :] += expert(hs_scr[rows, :])

        @pl.when(f == nf - 1)
        def _():
            ys = (ys_scr[rows, :] + expert(hs_scr[rows, :])).astype(BF16)
            lane = lax.broadcasted_iota(jnp.int32, (TB, LANE), 1)
            gate = jnp.sum(jnp.where(lane == e, comb_ref[...], 0.0), axis=-1, keepdims=True)
            for j, (p, sb) in enumerate(zip(onehots, subs)):
                o_ref[sb, :] += gate[sb] * _dot_tn(p, ys[j * CH:(j + 1) * CH])

        return carry

    lo = 0
    for ch in CH[:-1]:
        @pl.when((count > lo) & (count <= ch))
        def _(ch=ch):
            body(0, 0, ch)
        lo = ch

    @pl.when(count > lo)
    def _():
        lax.fori_loop(0, (count + CH[-1] - 1) // CH[-1], functools.partial(body, CH=CH[-1]), 0)


def _moe(x, h, comb, combt, w_up, wd, TB, SB, CH, nf):
    assert nf >= 2, "the first and last d_ff positions are separate code paths"
    T = x.shape[0]
    dff = wd.shape[1]
    tf = dff // nf
    slots = pl.cdiv(SB, CH[-1]) * CH[-1] * (TB // SB)
    tri = jnp.triu(jnp.ones((SB, SB), BF16), k=1)
    once = pl.Buffered(1)
    return pl.pallas_call(
        functools.partial(_moe_kernel, TB=TB, SB=SB, CH=CH, nf=nf),
        grid=(T // TB, N_EXPERTS, nf),
        in_specs=[pl.BlockSpec((TB, D_MODEL), lambda i, e, f: (i, 0), pipeline_mode=once),
                  pl.BlockSpec((TB, D_MODEL), lambda i, e, f: (i, 0), pipeline_mode=once),
                  pl.BlockSpec((TB, LANE), lambda i, e, f: (i, 0)),
                  pl.BlockSpec((N_EXPERTS, TB), lambda i, e, f: (0, i)),
                  pl.BlockSpec((SB, SB), lambda i, e, f: (0, 0), pipeline_mode=once),
                  pl.BlockSpec((1, D_MODEL, tf), lambda i, e, f: (e, 0, f)),
                  pl.BlockSpec((1, D_MODEL, tf), lambda i, e, f: (e, 0, nf + f)),
                  pl.BlockSpec((1, tf, D_MODEL), lambda i, e, f: (e, f, 0))],
        out_specs=pl.BlockSpec((TB, D_MODEL), lambda i, e, f: (i, 0)),
        out_shape=jax.ShapeDtypeStruct((T, D_MODEL), F32),
        scratch_shapes=[pltpu.VMEM((N_EXPERTS, TB), F32), pltpu.VMEM((slots, D_MODEL), BF16),
                        pltpu.VMEM((slots, D_MODEL), F32)],
        compiler_params=_cparams(("parallel", "arbitrary", "arbitrary")),
        name="moe_top2",
    )(h, x, comb, combt, tri, w_up, w_up, wd)


def _prep_w_in(w):
    wt = w.T
    hq, hf, hi, hg = (wt[j * HG_WIDTH:(j + 1) * HG_WIDTH] for j in range(4))
    z = wt[_OFF_SSM:_OFF_SSM + SSM_WIDTH]
    xbc = wt[_OFF_SSM + SSM_WIDTH:_OFF_SSM + SSM_WIDTH + SSM_CONV_DIM]
    dt = wt[_OFF_ATT - SSM_HEADS:_OFF_ATT]
    att = wt[_OFF_ATT:_OFF_GATE]
    gates = wt[_OFF_GATE:]
    w16 = jnp.concatenate([gates, xbc, hq, hi, hg, z, att], axis=0).astype(BF16)
    w32 = jnp.concatenate([hf, dt, jnp.zeros((LANE - SSM_HEADS, D_MODEL), w.dtype)], axis=0).astype(BF16)
    return w16, w32


def _pad_lanes(v, n, value=0.0):
    v = v.reshape(1, -1).astype(F32)
    return jnp.pad(v, ((0, 0), (0, n - v.shape[1])), constant_values=value)


def _run_group(x, B, L, C_h, C_s, valid, NB, tm, layer_params, states, kv_bufs):
    depth = len(layer_params)
    st_out, kv_out = None, (None if kv_bufs is None else [])
    for l, P in enumerate(layer_params):
        st = states[l]
        a16, a32 = _norm_proj(x, P["mix_norm"], P["w16"], P["w32"], tm)
        a16b, a32b = a16.reshape(B, L, W16), a32.reshape(B, L, W32)
        y_hg, s_hg, y_ssm, s_ssm, s_conv = _recurrent_mixers(
            a16b, a32b, P["lb"], P["hg_norm"], st["hgrn_t"], st["conv"], P["conv_w"], P["conv_b"], P["dt_bias"],
            P["a_log"], P["d_skip"], P["ssm_norm"], st["ssm"], B, L, C_h, HG_SUB, valid, NB, l, depth, st_out)
        st_out = (s_hg, s_ssm, s_conv)
        y_hg, y_ssm = y_hg.reshape(B * L, HG_WIDTH), y_ssm.reshape(B * L, SSM_WIDTH)
        if kv_bufs is None:
            y_att, *kv_out = _attn_prompt(a16, P["gq"], P["gk"], P["bd"], B, L, l, depth, kv_out)
        else:
            y_att, *kv = _attn_sample(a16, kv_bufs, l, P["gq"], P["gk"], P["bd"], B)
            kv_out.append(kv)
        x = _merge(y_hg, y_ssm, y_att, a16, x, P["w_out_hg"], P["w_out_ssm"], P["w_out_att"], P["w_o"], tm)
        if l % 2 == 0:
            F = P["ffn"]
            nxt = layer_params[l + 1]["moe"] if l + 1 < len(layer_params) else {}
            pending = [k for k in ("w_up", "wd") if k + "_f32" in nxt and k not in nxt]
            cast = [nxt[k + "_f32"].reshape(-1, nxt[k + "_f32"].shape[-1]) for k in pending]
            x, done = _ffn(x, P["ffn_norm"], F["w_up"], F["wd"], min(tm, 256), cast)
            for k, w in zip(pending, done):
                nxt[k] = w.reshape(nxt[k + "_f32"].shape)
        else:
            F = P["moe"]
            h, comb, combt = _router(x, P["ffn_norm"], F["wr"], tm)
            x = _moe(x, h, comb, combt, F["w_up"], F["wd"], min(MOE_BLOCK, x.shape[0]), min(MOE_SUB, x.shape[0]), MOE_SLOTS, 2)
    return x, st_out, kv_out


def kernel(x_prompt, x_sample, state_hgrn, state_ssm, state_conv, cache_kv_w128, cache_kv_w512, cache_kv_w2048, w_in, mix_norm, hgrn_lb_logits, hgrn_norm, ssm_conv_w, ssm_conv_b, ssm_dt_bias, ssm_a_log, ssm_d, ssm_norm, att_q_norm, att_k_norm, w_out_hgrn, w_out_ssm, w_out_att, w_o, ffn_norm, w_ffn_up, w_ffn_down, w_router, w_moe_up, w_moe_down):
    depth = w_in.shape[0]
    Bp, Lp, _ = x_prompt.shape
    Bs, Ls, _ = x_sample.shape
    R = SAMPLE_ROWS
    for buf, w in zip((cache_kv_w128, cache_kv_w512, cache_kv_w2048), ATT_WINDOWS):
        assert buf.shape[2] == w, "sample attention assumes full window buffers"

    lb_soft = jax.nn.softmax(hgrn_lb_logits.astype(F32), axis=0)
    lb_table = jnp.cumsum(lb_soft, axis=0) - lb_soft[0]
    bd = jnp.kron(jnp.eye(ATT_HEADS, dtype=F32), jnp.full((ATT_HEAD_DIM, ATT_HEAD_DIM), 1.0 / ATT_HEAD_DIM, F32)).astype(BF16)

    layer_params = []
    for l in range(depth):
        w16, w32 = _prep_w_in(w_in[l])
        P = dict(
            w16=w16, w32=w32,
            mix_norm=mix_norm[l].reshape(1, -1), ffn_norm=ffn_norm[l].reshape(1, -1),
            lb=lb_table[l].reshape(1, -1), hg_norm=hgrn_norm[l].reshape(1, -1),
            conv_w=jnp.pad(ssm_conv_w[l], ((0, 8 - SSM_CONV), (0, 0))), conv_b=ssm_conv_b[l].reshape(1, -1),
            dt_bias=_pad_lanes(ssm_dt_bias[l], LANE), a_log=_pad_lanes(ssm_a_log[l], LANE),
            d_skip=jnp.repeat(ssm_d[l].astype(F32), SSM_HEAD_DIM).reshape(1, -1),
            ssm_norm=ssm_norm[l].reshape(1, -1),
            gq=jnp.tile(att_q_norm[l], (1, ATT_HEADS)), gk=jnp.tile(att_k_norm[l], (1, ATT_HEADS)), bd=bd,
            w_out_hg=w_out_hgrn[l].astype(BF16), w_out_ssm=w_out_ssm[l].astype(BF16),
            w_out_att=w_out_att[l].astype(BF16), w_o=w_o[l].astype(BF16),
        )
        if l % 2 == 0:
            P["ffn"] = dict(w_up=w_ffn_up[l // 2].astype(BF16), wd=w_ffn_down[l // 2].astype(BF16))
        else:
            P["moe"] = dict(wr=jnp.pad(w_router[l // 2].astype(F32), ((0, 0), (0, LANE - N_EXPERTS))),
                            w_up_f32=w_moe_up[l // 2], wd_f32=w_moe_down[l // 2])
        layer_params.append(P)

    zero_p = dict(hgrn_t=jnp.zeros((Bp, HG_WIDTH, HG_DIM), F32), ssm=jnp.zeros((Bp, SSM_WIDTH, SSM_STATE), F32),
                  conv=jnp.zeros((Bp, 8, SSM_CONV_DIM), F32))
    xp, st_p, kv_p = _run_group(x_prompt.reshape(Bp * Lp, D_MODEL), Bp, Lp, 64, 64, 64, 4, 512,
                                layer_params, [zero_p] * depth, None)

    xs = jnp.pad(x_sample, ((0, 0), (0, R - Ls), (0, 0))).reshape(Bs * R, D_MODEL)
    st_s = []
    for l in range(depth):
        st_s.append(dict(
            hgrn_t=jnp.swapaxes(state_hgrn[l], -1, -2).reshape(Bs, HG_WIDTH, HG_DIM),
            ssm=state_ssm[l].reshape(Bs, SSM_WIDTH, SSM_STATE),
            conv=jnp.pad(state_conv[l], ((0, 0), (8 - (SSM_CONV - 1), 0), (0, 0)))))
    kv_s = [jnp.transpose(c, (0, 1, 3, 4, 5, 2)).reshape(depth, Bs, 2 * ATT_WIDTH, c.shape[2])
            for c in (cache_kv_w128, cache_kv_w512, cache_kv_w2048)]
    xs, st_s, kv_s = _run_group(xs, Bs, R, R, R, Ls, 4, Bs * R, layer_params, st_s, kv_s)

    def states(st, B):
        s_hg, s_ssm, s_conv = st
        return (jnp.swapaxes(s_hg.reshape(depth, B, HG_HEADS, HG_DIM, HG_DIM), -1, -2),
                s_ssm.reshape(depth, B, SSM_HEADS, SSM_HEAD_DIM, SSM_STATE),
                s_conv[:, :, 8 - (SSM_CONV - 1):])

    kv_prompt = [jnp.transpose(kv.reshape(depth, Bp, 2, ATT_HEADS, ATT_HEAD_DIM, kv.shape[3]), (0, 1, 5, 2, 3, 4))
                 for kv in kv_p]
    kv_sample = [jnp.stack([kv_s[l][j][:, :Ls].reshape(Bs, Ls, 2, ATT_HEADS, ATT_HEAD_DIM) for l in range(depth)])
                 for j in range(3)]
    y_prompt = xp.reshape(Bp, Lp, D_MODEL)
    y_sample = xs.reshape(Bs, R, D_MODEL)[:, :Ls]
    return (y_prompt, y_sample, *states(st_p, Bp), *kv_prompt, *states(st_s, Bs), *kv_sample)
```

```python
import functools

import numpy as np
import jax
import jax.numpy as jnp
from jax import lax
from jax.experimental import pallas as pl
from jax.experimental.pallas import tpu as pltpu

F32 = jnp.float32
BF16 = jnp.bfloat16

D_MODEL = 1024
HG_HEADS, HG_DIM = 4, 128
HG_WIDTH = HG_HEADS * HG_DIM
SSM_HEADS, SSM_HEAD_DIM, SSM_GROUPS, SSM_STATE, SSM_CONV = 8, 64, 2, 128, 4
SSM_WIDTH = SSM_HEADS * SSM_HEAD_DIM
SSM_CONV_DIM = SSM_WIDTH + 2 * SSM_GROUPS * SSM_STATE
ATT_WINDOWS = (128, 512, 2048)
ATT_DILATIONS = (1, 4, 16)
ATT_ORDER = (2, 1, 0)
ATT_HEADS, ATT_HEAD_DIM = 4, 64
ATT_WIDTH = ATT_HEADS * ATT_HEAD_DIM
ATT_KEYS = 128
ATT_SCALE = ATT_HEAD_DIM ** -0.5
N_EXPERTS = 8
EPS = 1e-6
NEG = -1e30

_OFF_SSM = 4 * HG_WIDTH
_OFF_ATT = _OFF_SSM + SSM_WIDTH + SSM_CONV_DIM + SSM_HEADS
_OFF_GATE = _OFF_ATT + 3 * 3 * ATT_WIDTH

A16_GATE, A16_XBC, A16_HQ, A16_HI, A16_HG, A16_Z, A16_ATT = 0, 3072, 4096, 4608, 5120, 5632, 6144
W16 = 8448
A32_HF, A32_DT = 0, 512
W32 = 640

LANE = 128
SAMPLE_ROWS = 16
HG_SUB = 8
MOE_BLOCK = 2048
MOE_HALF = 1024
MOE_SUB = 512
MOE_SLOTS = (128, 144, 160)
VMEM_LIMIT = 56 * 2 ** 20


def _cparams(sem):
    return pltpu.CompilerParams(dimension_semantics=sem, vmem_limit_bytes=VMEM_LIMIT)


def _dot(a, b):
    return jnp.dot(a, b, preferred_element_type=F32)


def _dot_nt(a, b):
    return lax.dot_general(a, b, (((1,), (1,)), ((), ())), preferred_element_type=F32)


def _dot_tn(a, b):
    return lax.dot_general(a, b, (((0,), (0,)), ((), ())), preferred_element_type=F32)


def _split3(x):
    hi = x.astype(BF16)
    r = x - hi.astype(F32)
    mid = r.astype(BF16)
    lo = (r - mid.astype(F32)).astype(BF16)
    return hi, mid, lo


def _cumsum_rows(x, tril_bf16):
    hi, mid, lo = _split3(x)
    return _dot(tril_bf16, hi) + _dot(tril_bf16, mid) + _dot(tril_bf16, lo)


def _sigmoid(x):
    return 0.5 * jnp.tanh(0.5 * x) + 0.5


def _silu(x):
    return x * _sigmoid(x)


def _tril_bf16(n):
    r = lax.broadcasted_iota(jnp.int32, (n, n), 0)
    c = lax.broadcasted_iota(jnp.int32, (n, n), 1)
    return jnp.where(c <= r, 1.0, 0.0).astype(BF16)


def _norm_proj_kernel(x_ref, g_ref, w16_ref, w32_ref, o16_ref, o32_ref, *, chunk):
    x = x_ref[...]
    h = (x * lax.rsqrt(jnp.mean(x * x, axis=-1, keepdims=True) + EPS) * g_ref[...]).astype(BF16)
    for c0 in range(0, W16, chunk):
        o16_ref[:, c0:c0 + chunk] = _dot_nt(h, w16_ref[c0:c0 + chunk, :]).astype(BF16)
    o32_ref[...] = _dot_nt(h, w32_ref[...])


def _norm_proj(x, gain, w16, w32, tm):
    T = x.shape[0]
    const = lambda i: (0, 0)
    return pl.pallas_call(
        functools.partial(_norm_proj_kernel, chunk=768),
        grid=(pl.cdiv(T, tm),),
        in_specs=[
            pl.BlockSpec((tm, D_MODEL), lambda i: (i, 0)),
            pl.BlockSpec((1, D_MODEL), const),
            pl.BlockSpec((W16, D_MODEL), const, pipeline_mode=pl.Buffered(1)),
            pl.BlockSpec((W32, D_MODEL), const, pipeline_mode=pl.Buffered(1)),
        ],
        out_specs=[
            pl.BlockSpec((tm, W16), lambda i: (i, 0)),
            pl.BlockSpec((tm, W32), lambda i: (i, 0)),
        ],
        out_shape=[jax.ShapeDtypeStruct((T, W16), BF16), jax.ShapeDtypeStruct((T, W32), F32)],
        compiler_params=_cparams(("parallel",)),
        name="norm_proj",
    )(x, gain, w16, w32)


def _hgrn_tables(C, sub):
    t = np.arange(C)[:, None]
    u = np.arange(C)[None, :]
    blocks = [u <= t]
    levels = []
    l = sub
    while l < C:
        upper = (t % (2 * l)) >= l
        b = (t // (2 * l)) * (2 * l) + l - 1
        blocks.append(np.where(upper, (u > b) & (u <= t), (u > t) & (u <= b)))
        levels.append(l)
        l *= 2
    for s in range(sub):
        r = (t // sub) * sub + s
        blocks.append((u > r) & (u <= t))
    sel = [u == (t // sub) * sub + s for s in range(sub)]
    d = np.concatenate(blocks, 0).astype(np.float32)
    sl = np.concatenate(sel, 0).astype(np.float32)
    table = np.block([[d, d, d, np.zeros_like(d)], [np.zeros((sl.shape[0], 3 * C), np.float32), sl]])
    return jnp.asarray(table, BF16), tuple(levels)


def _hgrn_chunk(q_ref, f_ref, i_ref, g_ref, lb_ref, gn_ref, tab_ref, y_ref, st_scr, *, C, sub, levels, valid):
    lb = lb_ref[...]
    fgate = lb + (1.0 - lb) * jax.nn.sigmoid(f_ref[...])
    logf = jnp.log(fgate)
    kk = 1.0 - fgate
    rows = lax.broadcasted_iota(jnp.int32, (C, HG_WIDTH), 0)
    if valid < C:
        logf = jnp.where(rows < valid, logf, 0.0)
        kk = jnp.where(rows < valid, kk, 0.0)
    qa = _silu(q_ref[...].astype(F32))
    va = i_ref[...].astype(F32)
    ga = g_ref[...].astype(F32)

    ed = _dot(tab_ref[...], jnp.concatenate([*_split3(logf), kk.astype(BF16)], axis=0))
    nblk = 1 + len(levels) + sub
    blk = lambda j: ed[j * C:(j + 1) * C]
    G = blk(0)
    ksel = ed[nblk * C:]

    row_c = lax.broadcasted_iota(jnp.int32, (C, C), 0)
    lane_c = lax.broadcasted_iota(jnp.int32, (C, C), 1)
    off_c = lane_c - (row_c // sub) * sub

    A = [jnp.zeros((C, C), F32) for _ in range(HG_HEADS)]
    for s in range(sub):
        prod = qa * ksel[s * C:(s + 1) * C] * jnp.exp(blk(1 + len(levels) + s))
        for h in range(HG_HEADS):
            col = jnp.sum(prod[:, h * HG_DIM:(h + 1) * HG_DIM], axis=-1, keepdims=True)
            A[h] = jnp.where(off_c == s, col, A[h])
    A = [jnp.where(lane_c <= row_c, a, 0.0) for a in A]

    for li, l in enumerate(levels):
        upper = (rows % (2 * l)) >= l
        scaled = jnp.where(upper, qa, kk) * jnp.exp(blk(1 + li))
        qt = jnp.where(upper, scaled, 0.0).astype(BF16)
        kt = jnp.where(upper, 0.0, scaled).astype(BF16)
        same = (row_c // (2 * l)) == (lane_c // (2 * l))
        for h in range(HG_HEADS):
            sl = slice(h * HG_DIM, (h + 1) * HG_DIM)
            a_l = _dot_nt(qt[:, sl], kt[:, sl])
            A[h] = A[h] + (jnp.where(same, a_l, 0.0) if 2 * l < C else a_l)

    outs = []
    for h in range(HG_HEADS):
        sl = slice(h * HG_DIM, (h + 1) * HG_DIM)
        Gh, qh, kh = G[:, sl], qa[:, sl], kk[:, sl]
        vb = va[:, sl].astype(BF16)
        st = st_scr[sl, :]
        o = _dot(A[h].astype(BF16), vb) + _dot_nt((qh * jnp.exp(Gh)).astype(BF16), st.astype(BF16))
        g_last = Gh[C - 1:C]
        kdec = kh * jnp.exp(g_last - Gh)
        st_scr[sl, :] = jnp.exp(g_last) * st + _dot_tn(vb, kdec.astype(BF16))
        on = o * lax.rsqrt(jnp.mean(o * o, axis=-1, keepdims=True) + EPS) * gn_ref[...]
        outs.append(on * _silu(ga[:, sl]))
    y_ref[...] = jnp.concatenate(outs, axis=1).astype(BF16)


def _ssd_chunk(xbc_ref, z_ref, dt_ref, cw_ref, cb_ref, dtb_ref, alog_ref, dsk_ref, sn_ref,
               y_ref, ubuf, hs, *, C, valid):
    ubuf[8:8 + C, :] = xbc_ref[...].astype(F32)
    acc = cb_ref[...] + ubuf[5:5 + C, :] * cw_ref[0:1, :]
    for j in range(1, SSM_CONV):
        acc = acc + ubuf[5 + j:5 + j + C, :] * cw_ref[j:j + 1, :]
    xc = _silu(acc)
    ubuf[0:8, :] = ubuf[valid:valid + 8, :]

    x = dt_ref[...] + dtb_ref[...]
    dt = jnp.maximum(x, 0.0) + jnp.log1p(jnp.exp(-jnp.abs(x)))
    if valid < C:
        rows = lax.broadcasted_iota(jnp.int32, (C, LANE), 0)
        dt = jnp.where(rows < valid, dt, 0.0)
    a = -jnp.exp(alog_ref[...])
    cum = _cumsum_rows(dt * a, _tril_bf16(C))
    cum_t = cum.T
    dt_t = dt.T

    ri = lax.broadcasted_iota(jnp.int32, (C, C), 0)
    cj = lax.broadcasted_iota(jnp.int32, (C, C), 1)
    tril = cj <= ri
    lane = lax.broadcasted_iota(jnp.int32, (C, LANE), 1)
    srow = lax.broadcasted_iota(jnp.int32, (LANE, LANE), 0)
    bm = [xc[:, SSM_WIDTH + g * SSM_STATE:SSM_WIDTH + (g + 1) * SSM_STATE].astype(BF16) for g in range(SSM_GROUPS)]
    off_c = SSM_WIDTH + SSM_GROUPS * SSM_STATE
    cm = [xc[:, off_c + g * SSM_STATE:off_c + (g + 1) * SSM_STATE].astype(BF16) for g in range(SSM_GROUPS)]
    cb = [_dot_nt(cm[g], bm[g]) for g in range(SSM_GROUPS)]

    ys = []
    for p in range(SSM_HEADS // 2):
        g = (2 * p) // (SSM_HEADS // SSM_GROUPS)
        xp = xc[:, p * LANE:(p + 1) * LANE]
        yp = None
        colw = []
        for k, hd in enumerate((2 * p, 2 * p + 1)):
            colb = jnp.broadcast_to(cum[:, hd:hd + 1], (C, LANE))
            rowb = jnp.broadcast_to(cum_t[hd:hd + 1, :], (C, C))
            dtr = jnp.broadcast_to(dt_t[hd:hd + 1, :], (C, C))
            w = cb[g] * jnp.exp(jnp.where(tril, colb[:, :C] - rowb, NEG)) * dtr
            half = (lane < SSM_HEAD_DIM) if k == 0 else (lane >= SSM_HEAD_DIM)
            t = _dot(w.astype(BF16), jnp.where(half, xp, 0.0).astype(BF16))
            yp = t if yp is None else yp + t
            colw.append((colb, jnp.broadcast_to(dt[:, hd:hd + 1], (C, LANE))))
        first = lane < SSM_HEAD_DIM
        cum_e = jnp.where(first, colw[0][0], colw[1][0])
        dt_e = jnp.where(first, colw[0][1], colw[1][1])
        hp = hs[p * LANE:(p + 1) * LANE, :]
        yp = yp + jnp.exp(cum_e) * _dot_nt(cm[g], hp.astype(BF16))
        c_last = cum_e[C - 1:C, :]
        w_in = jnp.exp(c_last - cum_e) * dt_e
        dec = jnp.where(srow < SSM_HEAD_DIM,
                        jnp.broadcast_to(cum[C - 1:C, 2 * p:2 * p + 1], (LANE, LANE)),
                        jnp.broadcast_to(cum[C - 1:C, 2 * p + 1:2 * p + 2], (LANE, LANE)))
        hs[p * LANE:(p + 1) * LANE, :] = jnp.exp(dec) * hp + _dot_tn((xp * w_in).astype(BF16), bm[g])
        ys.append(yp)
    y = jnp.concatenate(ys, axis=1) + dsk_ref[...] * xc[:, :SSM_WIDTH]
    yz = y * _silu(z_ref[...].astype(F32))
    y_ref[...] = (yz * lax.rsqrt(jnp.mean(yz * yz, axis=-1, keepdims=True) + EPS) * sn_ref[...]).astype(BF16)


def _rec_kernel(q_ref, f_ref, i_ref, g_ref, lb_ref, gn_ref, s0_ref, tab_ref,
                xbc_ref, z_ref, dt_ref, pre_ref, cw_ref, cb_ref, dtb_ref, alog_ref, dsk_ref, sn_ref, h0_ref,
                *rest, NB, nchunks, first, hg_kw, ssd_kw):
    yh_ref, sout_ref, ys_ref, hout_ref, cout_ref, st_scr, ubuf, hs = rest[-8:]
    ci = pl.program_id(1)

    @pl.when(ci == 0)
    def _():
        st_scr[...] = s0_ref[...]
        ubuf[:, 0:8, :] = pre_ref[...]
        hs[...] = h0_ref[...]

    for j in range(NB):
        _hgrn_chunk(q_ref.at[j], f_ref.at[j], i_ref.at[j], g_ref.at[j], lb_ref, gn_ref, tab_ref,
                    yh_ref.at[j], st_scr.at[j], **hg_kw)
        _ssd_chunk(xbc_ref.at[j], z_ref.at[j], dt_ref.at[j], cw_ref, cb_ref, dtb_ref, alog_ref, dsk_ref, sn_ref,
                   ys_ref.at[j], ubuf.at[j], hs.at[j], **ssd_kw)

    @pl.when(ci == nchunks - 1)
    def _():
        for d in range(sout_ref.shape[0]):
            sout_ref[d] = st_scr[...]
            hout_ref[d] = hs[...]
            cout_ref[d] = ubuf[:, 0:8, :]


def _recurrent_mixers(a16, a32, lb, gn, s0t, prefix, cw, cb, dtb, alog, dsk, sn, h0, B, L, C, sub, valid, NB,
                      layer, depth, prev):
    nch = L // C
    table, levels = _hgrn_tables(C, sub)
    const = lambda b, c: (0, 0)
    bat = lambda b, c: (b, 0, 0)
    blk = lambda w, col: pl.BlockSpec((NB, C, w), lambda b, c: (b, c, col))
    first = prev is None
    slot = (lambda rows, w: pl.BlockSpec((depth, NB, rows, w), lambda b, c: (0, b, 0, 0))) if first else \
           (lambda rows, w: pl.BlockSpec((1, NB, rows, w), lambda b, c: (layer, b, 0, 0)))
    return pl.pallas_call(
        functools.partial(_rec_kernel, NB=NB, nchunks=nch, first=first,
                          hg_kw=dict(C=C, sub=sub, levels=levels, valid=valid), ssd_kw=dict(C=C, valid=valid)),
        grid=(B // NB, nch),
        in_specs=[
            blk(HG_WIDTH, A16_HQ // HG_WIDTH), blk(HG_WIDTH, A32_HF // HG_WIDTH),
            blk(HG_WIDTH, A16_HI // HG_WIDTH), blk(HG_WIDTH, A16_HG // HG_WIDTH),
            pl.BlockSpec((1, HG_WIDTH), const),
            pl.BlockSpec((1, HG_DIM), const),
            pl.BlockSpec((NB, HG_WIDTH, HG_DIM), bat),
            pl.BlockSpec(table.shape, const),
            blk(SSM_CONV_DIM, A16_XBC // SSM_CONV_DIM), blk(SSM_WIDTH, A16_Z // SSM_WIDTH), blk(LANE, A32_DT // LANE),
            pl.BlockSpec((NB, 8, SSM_CONV_DIM), bat),
            pl.BlockSpec((8, SSM_CONV_DIM), const),
            pl.BlockSpec((1, SSM_CONV_DIM), const),
            pl.BlockSpec((1, LANE), const),
            pl.BlockSpec((1, LANE), const),
            pl.BlockSpec((1, SSM_WIDTH), const),
            pl.BlockSpec((1, SSM_WIDTH), const),
            pl.BlockSpec((NB, SSM_WIDTH, SSM_STATE), bat),
        ] + ([] if first else [pl.BlockSpec(memory_space=pl.ANY)] * 3),
        out_specs=[
            blk(HG_WIDTH, 0), slot(HG_WIDTH, HG_DIM),
            blk(SSM_WIDTH, 0), slot(SSM_WIDTH, SSM_STATE), slot(8, SSM_CONV_DIM),
        ],
        out_shape=[jax.ShapeDtypeStruct((B, L, HG_WIDTH), BF16),
                   jax.ShapeDtypeStruct((depth, B, HG_WIDTH, HG_DIM), F32),
                   jax.ShapeDtypeStruct((B, L, SSM_WIDTH), BF16),
                   jax.ShapeDtypeStruct((depth, B, SSM_WIDTH, SSM_STATE), F32),
                   jax.ShapeDtypeStruct((depth, B, 8, SSM_CONV_DIM), F32)],
        input_output_aliases={} if first else {19: 1, 20: 3, 21: 4},
        scratch_shapes=[pltpu.VMEM((NB, HG_WIDTH, HG_DIM), F32), pltpu.VMEM((NB, C + 8, SSM_CONV_DIM), F32),
                        pltpu.VMEM((NB, SSM_WIDTH, SSM_STATE), F32)],
        compiler_params=_cparams(("parallel", "arbitrary")),
        name="hgrn2_ssd",
    )(a16, a32, a16, a16, lb, gn, s0t, table, a16, a16, a32, prefix, cw, cb, dtb, alog, dsk, sn, h0,
      *([] if first else prev))


def _head_norm(x, gain, bd):
    ms = _dot((x * x).astype(BF16), bd)
    return x * (lax.rsqrt(ms + EPS) * gain)


def _head_masks(rows):
    lane = lax.broadcasted_iota(jnp.int32, (rows, ATT_WIDTH), 1)
    return [(lane >= h * ATT_HEAD_DIM) & (lane < (h + 1) * ATT_HEAD_DIM) for h in range(ATT_HEADS)]


def _stack_heads(q, masks):
    return jnp.concatenate([jnp.where(m, q, 0.0) for m in masks], axis=0).astype(BF16)


def _unstack_heads(pv, m, d, masks):
    R = pv.shape[0] // ATT_HEADS
    acc, mx, den = pv[0:R], jnp.broadcast_to(m[0:R], (R, ATT_WIDTH)), jnp.broadcast_to(d[0:R], (R, ATT_WIDTH))
    for h in range(1, ATT_HEADS):
        sl = slice(h * R, (h + 1) * R)
        acc = jnp.where(masks[h], pv[sl], acc)
        mx = jnp.where(masks[h], m[sl], mx)
        den = jnp.where(masks[h], d[sl], den)
    return acc, mx, den


def _attend(qb, kb, vb, bias4, masks):
    s = _dot_nt(_stack_heads(qb, masks), kb) + bias4
    m = jnp.max(s, axis=-1, keepdims=True)
    p = jnp.exp(s - m)
    d = jnp.sum(p, axis=-1, keepdims=True)
    return _unstack_heads(_dot(p.astype(BF16), vb), m, d, masks)


def _ld2(pair, s):
    return jnp.concatenate([pair[0][s, :], pair[1][s, :]], axis=1)


def _st2(pair, s, val):
    pair[0][s, :] = val[:, :LANE]
    pair[1][s, :] = val[:, LANE:]


def _attn_prompt_kernel(att_ref, gq_ref, gk_ref, bd_ref, *rest, L):
    y_ref, kv0_ref, kv1_ref, kv2_ref = rest[-16:-12]
    scr = rest[-12:]
    qn, kn, vv, ya, ma, za = (scr[2 * j:2 * j + 2] for j in range(6))
    g = pl.program_id(1)
    RB = 256
    QB = ATT_KEYS

    def group(step, gi, dil, keep, kv_ref):
        M = L // dil
        nb = M // QB
        KW = min(2 * QB, M)
        bd = bd_ref[...]

        def norm_body(i, carry):
            r0 = pl.multiple_of(i * RB, RB)
            x = att_ref[pl.ds(r0, RB), :].astype(F32)
            _st2(qn, pl.ds(r0, RB), _head_norm(x[:, :ATT_WIDTH], gq_ref[gi:gi + 1, :] * ATT_SCALE, bd))
            _st2(kn, pl.ds(r0, RB), _head_norm(x[:, ATT_WIDTH:2 * ATT_WIDTH], gk_ref[gi:gi + 1, :], bd))
            _st2(vv, pl.ds(r0, RB), x[:, 2 * ATT_WIDTH:])
            return carry

        lax.fori_loop(0, L // RB, norm_body, 0)
        cw = min(keep, RB)
        for c0 in range(0, keep, cw):
            rows = slice(L - keep + c0, L - keep + c0 + cw)
            for p, half in enumerate((kn[0], kn[1], vv[0], vv[1])):
                piece = half[rows, :].T
                for d in range(kv_ref.shape[0]):
                    kv_ref[d, 0, p * LANE:(p + 1) * LANE, c0:c0 + cw] = piece

        masks = _head_masks(QB)
        iq = lax.broadcasted_iota(jnp.int32, (ATT_HEADS * QB, KW), 0) & (QB - 1)
        ik = lax.broadcasted_iota(jnp.int32, (ATT_HEADS * QB, KW), 1)

        def run_blocks(first, count, lead):
            dist = (iq - ik) if lead else (QB + iq - ik)
            bias4 = jnp.where((dist >= 0) & (dist <= ATT_KEYS), 0.0, NEG)
            unroll = max(u for u in (4, 3, 2, 1) if count % u == 0)

            def blk_body(j, carry):
                loaded = []
                for u in range(unroll):
                    it = first + j * unroll + u
                    n = it // dil
                    r = it - n * dil
                    k0 = 0 if lead else (n - 1) * QB
                    if dil == 1:
                        qs = pl.ds(pl.multiple_of(n * QB, QB), QB)
                        ks = pl.ds(pl.multiple_of(k0, QB), KW)
                    else:
                        qs = pl.ds(n * QB * dil + r, QB, stride=dil)
                        ks = pl.ds(k0 * dil + r, KW, stride=dil)
                    old = None if step == 0 else (_ld2(ya, qs), _ld2(ma, qs), _ld2(za, qs))
                    loaded.append((qs, _ld2(qn, qs), _ld2(kn, ks).astype(BF16), _ld2(vv, ks).astype(BF16), old))
                for qs, qb, kb, vb, old in loaded:
                    acc, mx, den = _attend(qb, kb, vb, bias4, masks)
                    if step > 0:
                        yo, mo, zo = old
                        mn = jnp.maximum(mo, mx)
                        eo = jnp.exp(mo - mn)
                        en = jnp.exp(mx - mn)
                        acc, mx, den = yo * eo + acc * en, mn, zo * eo + den * en
                    _st2(ya, qs, acc)
                    _st2(ma, qs, mx)
                    _st2(za, qs, den)
                return carry

            lax.fori_loop(0, count // unroll, blk_body, 0)

        run_blocks(0, dil, True)
        if nb > 1:
            run_blocks(dil, dil * (nb - 1), False)

    for step, gi in enumerate(ATT_ORDER):
        @pl.when(g == step)
        def _(step=step, gi=gi):
            group(step, gi, ATT_DILATIONS[gi], min(ATT_WINDOWS[gi], L), (kv0_ref, kv1_ref, kv2_ref)[gi])

    @pl.when(g == 2)
    def _():
        def out_body(i, carry):
            r0 = pl.multiple_of(i * RB, RB)
            y_ref[pl.ds(r0, RB), :] = (_ld2(ya, pl.ds(r0, RB)) / _ld2(za, pl.ds(r0, RB))).astype(BF16)
            return carry
        lax.fori_loop(0, L // RB, out_body, 0)


def _attn_prompt(a16, gq, gk, bd, B, L, layer, depth, prev):
    keeps = [min(w, L) for w in ATT_WINDOWS]
    const = lambda b, g: (0, 0)
    first = prev is None
    slot = (lambda k: pl.BlockSpec((depth, 1, 2 * ATT_WIDTH, k), lambda b, g: (0, b, 0, 0))) if first else \
           (lambda k: pl.BlockSpec((1, 1, 2 * ATT_WIDTH, k), lambda b, g: (layer, b, 0, 0)))
    return pl.pallas_call(
        functools.partial(_attn_prompt_kernel, L=L),
        grid=(B, 3),
        in_specs=[
            pl.BlockSpec((L, 3 * ATT_WIDTH), lambda b, g: (b, A16_ATT // (3 * ATT_WIDTH) + 2 - g)),
            pl.BlockSpec((3, ATT_WIDTH), const),
            pl.BlockSpec((3, ATT_WIDTH), const),
            pl.BlockSpec((ATT_WIDTH, ATT_WIDTH), const),
        ] + ([] if first else [pl.BlockSpec(memory_space=pl.ANY)] * 3),
        out_specs=[pl.BlockSpec((L, ATT_WIDTH), lambda b, g: (b, 0))] + [slot(k) for k in keeps],
        out_shape=[jax.ShapeDtypeStruct((B * L, ATT_WIDTH), BF16)]
        + [jax.ShapeDtypeStruct((depth, B, 2 * ATT_WIDTH, k), F32) for k in keeps],
        input_output_aliases={} if first else {4: 1, 5: 2, 6: 3},
        scratch_shapes=[pltpu.VMEM((L, LANE), F32) for _ in range(12)],
        compiler_params=_cparams(("parallel", "arbitrary")),
        name="attn_prompt",
    )(a16, gq, gk, bd, *([] if first else prev))


def _attn_sample_kernel(a0_ref, a1_ref, a2_ref, b0_ref, b1_ref, b2_ref, gq_ref, gk_ref, bd_ref,
                        y_ref, k0_ref, k1_ref, k2_ref):
    R = SAMPLE_ROWS
    bd = bd_ref[...]
    masks = _head_masks(R)
    ya = ma = za = None
    for gi, (a_ref, b_ref, k_ref) in enumerate(((a0_ref, b0_ref, k0_ref), (a1_ref, b1_ref, k1_ref),
                                                (a2_ref, b2_ref, k2_ref))):
        dil = ATT_DILATIONS[gi]
        wb = b_ref.shape[3]
        x = a_ref[...].astype(F32)
        q = _head_norm(x[:, :ATT_WIDTH], gq_ref[gi:gi + 1, :] * ATT_SCALE, bd)
        k = _head_norm(x[:, ATT_WIDTH:2 * ATT_WIDTH], gk_ref[gi:gi + 1, :], bd)
        v = x[:, 2 * ATT_WIDTH:]
        k_ref[0] = jnp.concatenate([k[:8], v[:8]], axis=1)
        k_t = b_ref[0, 0, 0:ATT_WIDTH, :].astype(BF16)
        v_t = b_ref[0, 0, ATT_WIDTH:2 * ATT_WIDTH, :].astype(BF16)
        kb, vb = k.astype(BF16), v.astype(BF16)

        def bias(dist):
            return jnp.where((dist >= 0) & (dist <= ATT_KEYS * dil) & ((dist & (dil - 1)) == 0), 0.0, NEG)

        R4 = ATT_HEADS * R
        row_b = lax.broadcasted_iota(jnp.int32, (R4, wb), 0) & (R - 1)
        row_n = lax.broadcasted_iota(jnp.int32, (R4, R), 0) & (R - 1)
        qs = _stack_heads(q, masks)
        s_b = _dot(qs, k_t) + bias(wb + row_b - lax.broadcasted_iota(jnp.int32, (R4, wb), 1))
        s_n = _dot_nt(qs, kb) + bias(row_n - lax.broadcasted_iota(jnp.int32, (R4, R), 1))
        m = jnp.maximum(jnp.max(s_b, axis=-1, keepdims=True), jnp.max(s_n, axis=-1, keepdims=True))
        p_b = jnp.exp(s_b - m)
        p_n = jnp.exp(s_n - m)
        d = jnp.sum(p_b, axis=-1, keepdims=True) + jnp.sum(p_n, axis=-1, keepdims=True)
        pv = _dot_nt(p_b.astype(BF16), v_t) + _dot(p_n.astype(BF16), vb)
        acc, mx, den = _unstack_heads(pv, m, d, masks)
        if gi == 0:
            ya, ma, za = acc, mx, den
        else:
            mn = jnp.maximum(ma, mx)
            eo = jnp.exp(ma - mn)
            en = jnp.exp(mx - mn)
            ya, za, ma = ya * eo + acc * en, za * eo + den * en, mn
    y_ref[...] = (ya / za).astype(BF16)


def _attn_sample(a16, bufs, layer, gq, gk, bd, B):
    R = SAMPLE_ROWS
    const = lambda b: (0, 0)
    bat = lambda b: (b, 0, 0)
    base = A16_ATT // (3 * ATT_WIDTH)
    return pl.pallas_call(
        _attn_sample_kernel,
        grid=(B,),
        in_specs=[pl.BlockSpec((R, 3 * ATT_WIDTH), functools.partial(lambda b, j: (b, j), j=base + g))
                  for g in range(3)]
        + [pl.BlockSpec((1, 1, 2 * ATT_WIDTH, buf.shape[3]), lambda b: (layer, b, 0, 0)) for buf in bufs]
        + [pl.BlockSpec((3, ATT_WIDTH), const), pl.BlockSpec((3, ATT_WIDTH), const),
           pl.BlockSpec((ATT_WIDTH, ATT_WIDTH), const)],
        out_specs=[pl.BlockSpec((R, ATT_WIDTH), lambda b: (b, 0))]
        + [pl.BlockSpec((1, 8, 2 * ATT_WIDTH), bat) for _ in range(3)],
        out_shape=[jax.ShapeDtypeStruct((B * R, ATT_WIDTH), BF16)]
        + [jax.ShapeDtypeStruct((B, 8, 2 * ATT_WIDTH), F32) for _ in range(3)],
        compiler_params=_cparams(("parallel",)),
        name="attn_sample",
    )(a16, a16, a16, *bufs, gq, gk, bd)


def _merge_kernel(yh_ref, ys_ref, ya_ref, gt_ref, x_ref, wh_ref, ws_ref, wa_ref, wo_ref, o_ref):
    gt = _sigmoid(gt_ref[...].astype(F32))
    mixed = (gt[:, 0:D_MODEL] * _dot(yh_ref[...], wh_ref[...])
             + gt[:, D_MODEL:2 * D_MODEL] * _dot(ys_ref[...], ws_ref[...])
             + gt[:, 2 * D_MODEL:] * _dot(ya_ref[...], wa_ref[...]))
    o_ref[...] = x_ref[...] + _dot(mixed.astype(BF16), wo_ref[...])


def _merge(y_hg, y_ssm, y_att, a16, x, wh, ws, wa, wo, tm):
    T = x.shape[0]
    const = lambda i: (0, 0)
    rowb = lambda w: pl.BlockSpec((tm, w), lambda i: (i, 0))
    wsp = lambda w: pl.BlockSpec(w.shape, const, pipeline_mode=pl.Buffered(1))
    return pl.pallas_call(
        _merge_kernel,
        grid=(pl.cdiv(T, tm),),
        in_specs=[rowb(HG_WIDTH), rowb(SSM_WIDTH), rowb(ATT_WIDTH),
                  pl.BlockSpec((tm, 3 * D_MODEL), lambda i: (i, A16_GATE // (3 * D_MODEL))),
                  rowb(D_MODEL), wsp(wh), wsp(ws), wsp(wa), wsp(wo)],
        out_specs=rowb(D_MODEL),
        out_shape=jax.ShapeDtypeStruct((T, D_MODEL), F32),
        compiler_params=_cparams(("parallel",)),
        name="merge_out_proj",
    )(y_hg, y_ssm, y_att, a16, x, wh, ws, wa, wo)


def _ffn_kernel(x_ref, g_ref, wg_ref, wu_ref, wd_ref, *rest):
    n = (len(rest) - 1) // 2
    cast_in, o_ref, cast_out = rest[:n], rest[n], rest[n + 1:]
    x = x_ref[...]
    h = (x * lax.rsqrt(jnp.mean(x * x, axis=-1, keepdims=True) + EPS) * g_ref[...]).astype(BF16)
    a = _silu(_dot(h, wg_ref[...])) * _dot(h, wu_ref[...])
    o_ref[...] = x + _dot(a.astype(BF16), wd_ref[...])
    for src, dst in zip(cast_in, cast_out):
        dst[...] = src[...].astype(BF16)


def _ffn(x, gain, w_up, wd, tm, cast=()):
    T = x.shape[0]
    dff = wd.shape[0]
    steps = pl.cdiv(T, tm)
    const = lambda i: (0, 0)
    wsp = lambda w: pl.BlockSpec(w.shape, const, pipeline_mode=pl.Buffered(1))
    slab = [pl.BlockSpec((w.shape[0] // steps, w.shape[1]), lambda i: (i, 0)) for w in cast]
    for w in cast:
        assert w.shape[0] % (16 * steps) == 0
    outs = pl.pallas_call(
        _ffn_kernel,
        grid=(steps,),
        in_specs=[pl.BlockSpec((tm, D_MODEL), lambda i: (i, 0)), pl.BlockSpec((1, D_MODEL), const),
                  pl.BlockSpec((D_MODEL, dff), const, pipeline_mode=pl.Buffered(1)),
                  pl.BlockSpec((D_MODEL, dff), lambda i: (0, 1), pipeline_mode=pl.Buffered(1)),
                  wsp(wd)] + slab,
        out_specs=[pl.BlockSpec((tm, D_MODEL), lambda i: (i, 0))] + slab,
        out_shape=[jax.ShapeDtypeStruct((T, D_MODEL), F32)] + [jax.ShapeDtypeStruct(w.shape, BF16) for w in cast],
        compiler_params=_cparams(("parallel",)),
        name="ffn_dense",
    )(x, gain, w_up, w_up, wd, *cast)
    return outs[0], outs[1:]


def _router_kernel(x_ref, g_ref, wr_ref, h_ref, comb_ref, combt_ref):
    x = x_ref[...]
    h = x * lax.rsqrt(jnp.mean(x * x, axis=-1, keepdims=True) + EPS) * g_ref[...]
    hb = h.astype(BF16)
    h_ref[...] = hb
    h_lo = (h - hb.astype(F32)).astype(BF16)
    w = wr_ref[...]
    w_hi = w.astype(BF16)
    w_lo = (w - w_hi.astype(F32)).astype(BF16)
    logits = _dot(hb, w_hi) + _dot(hb, w_lo) + _dot(h_lo, w_hi)
    lane = lax.broadcasted_iota(jnp.int32, logits.shape, 1)
    logits = jnp.where(lane < N_EXPERTS, logits, NEG)
    v1 = jnp.max(logits, axis=-1, keepdims=True)
    i1 = jnp.min(jnp.where(logits == v1, lane, LANE), axis=-1, keepdims=True)
    rest = jnp.where(lane == i1, NEG, logits)
    v2 = jnp.max(rest, axis=-1, keepdims=True)
    i2 = jnp.min(jnp.where(rest == v2, lane, LANE), axis=-1, keepdims=True)
    e = jnp.exp(v2 - v1)
    w1 = 1.0 / (1.0 + e)
    w2 = e / (1.0 + e)
    comb = jnp.where(lane == i1, w1, 0.0) + jnp.where(lane == i2, w2, 0.0)
    comb_ref[...] = comb
    combt_ref[...] = comb.T[0:N_EXPERTS, :]


def _router(x, gain, wr, tm):
    T = x.shape[0]
    const = lambda i: (0, 0)
    return pl.pallas_call(
        _router_kernel,
        grid=(pl.cdiv(T, tm),),
        in_specs=[pl.BlockSpec((tm, D_MODEL), lambda i: (i, 0)), pl.BlockSpec((1, D_MODEL), const),
                  pl.BlockSpec((D_MODEL, LANE), const)],
        out_specs=[pl.BlockSpec((tm, D_MODEL), lambda i: (i, 0)), pl.BlockSpec((tm, LANE), lambda i: (i, 0)),
                   pl.BlockSpec((N_EXPERTS, tm), lambda i: (0, i))],
        out_shape=[jax.ShapeDtypeStruct((T, D_MODEL), BF16), jax.ShapeDtypeStruct((T, LANE), F32),
                   jax.ShapeDtypeStruct((N_EXPERTS, T), F32)],
        compiler_params=_cparams(("parallel",)),
        name="moe_router",
    )(x, gain, wr)


def _moe_kernel(h_ref, x_ref, comb_ref, combt_ref, tri_ref, wg_ref, wu_ref, wd_ref, o_ref,
                rank_scr, hs_scr, ys_scr, *, HB, SB, CH, nf):
    e = pl.program_id(1)
    f = pl.program_id(2)
    TB = h_ref.shape[0]
    nsub = HB // SB

    @pl.when((e == 0) & (f == 0))
    def _():
        o_ref[...] = x_ref[...]
        routed_all = jnp.where(combt_ref[...] > 0.0, 1.0, 0.0).astype(BF16)
        for t0 in range(0, TB, SB):
            rank_scr[:, t0:t0 + SB] = _dot(routed_all[:, t0:t0 + SB], tri_ref[...])

    def expert(hs):
        a = _silu(_dot(hs, wg_ref[0])) * _dot(hs, wu_ref[0])
        return _dot(a.astype(BF16), wd_ref[0])

    for hb in range(TB // HB):
        subs = [slice(hb * HB + j * SB, hb * HB + (j + 1) * SB) for j in range(nsub)]
        routed = [combt_ref[pl.ds(e, 1), sb] > 0.0 for sb in subs]
        rank = [rank_scr[pl.ds(e, 1), sb] for sb in subs]
        count = None
        for r in routed:
            cj = jnp.sum(jnp.where(r, 1.0, 0.0)).astype(jnp.int32)
            count = cj if count is None else jnp.maximum(count, cj)

        def onehots(base, ch):
            slot = lax.broadcasted_iota(jnp.int32, (ch, SB), 0).astype(F32)
            return [jnp.where((rk - base == slot) & rt, 1.0, 0.0).astype(BF16) for rk, rt in zip(rank, routed)]

        def gather(ps):
            return jnp.concatenate([_dot(p, h_ref[sb, :]) for p, sb in zip(ps, subs)], axis=0).astype(BF16)

        def scatter(ps, ys, ch):
            lane = lax.broadcasted_iota(jnp.int32, (HB, LANE), 1)
            gate = jnp.sum(jnp.where(lane == e, comb_ref[hb * HB:(hb + 1) * HB, :], 0.0), axis=-1, keepdims=True)
            for j, (p, sb) in enumerate(zip(ps, subs)):
                o_ref[sb, :] += gate[j * SB:(j + 1) * SB] * _dot_tn(p, ys[j * ch:(j + 1) * ch])

        def single_pass(ch):
            ps = onehots(0.0, ch)
            rows = pl.ds(hb * nsub * CH[-1], nsub * ch)

            @pl.when(f == 0)
            def _():
                hs = gather(ps)
                hs_scr[rows, :] = hs
                ys_scr[rows, :] = expert(hs)

            if nf > 2:
                @pl.when((f > 0) & (f < nf - 1))
                def _():
                    ys_scr[rows, :] += expert(hs_scr[rows, :])

            @pl.when(f == nf - 1)
            def _():
                scatter(ps, (ys_scr[rows, :] + expert(hs_scr[rows, :])).astype(BF16), ch)

        def overflow_pass(c, carry):
            ch = CH[-1]
            ps = onehots((c * ch).astype(F32), ch)
            scatter(ps, expert(gather(ps)).astype(BF16), ch)
            return carry

        lo = 0
        for ch in CH:
            @pl.when((count > lo) & (count <= ch))
            def _(ch=ch):
                single_pass(ch)
            lo = ch

        @pl.when(count > lo)
        def _():
            lax.fori_loop(0, (count + lo - 1) // lo, overflow_pass, 0)


def _moe(x, h, comb, combt, w_up, wd, TB, HB, SB, CH, nf):
    assert nf >= 2, "the first and last d_ff positions are separate code paths"
    T = x.shape[0]
    dff = wd.shape[1]
    tf = dff // nf
    tri = jnp.triu(jnp.ones((SB, SB), BF16), k=1)
    once = pl.Buffered(1)
    return pl.pallas_call(
        functools.partial(_moe_kernel, HB=HB, SB=SB, CH=CH, nf=nf),
        grid=(T // TB, N_EXPERTS, nf),
        in_specs=[pl.BlockSpec((TB, D_MODEL), lambda i, e, f: (i, 0), pipeline_mode=once),
                  pl.BlockSpec((TB, D_MODEL), lambda i, e, f: (i, 0), pipeline_mode=once),
                  pl.BlockSpec((TB, LANE), lambda i, e, f: (i, 0), pipeline_mode=once),
                  pl.BlockSpec((N_EXPERTS, TB), lambda i, e, f: (0, i)),
                  pl.BlockSpec((SB, SB), lambda i, e, f: (0, 0), pipeline_mode=once),
                  pl.BlockSpec((1, D_MODEL, tf), lambda i, e, f: (e, 0, f)),
                  pl.BlockSpec((1, D_MODEL, tf), lambda i, e, f: (e, 0, nf + f)),
                  pl.BlockSpec((1, tf, D_MODEL), lambda i, e, f: (e, f, 0))],
        out_specs=pl.BlockSpec((TB, D_MODEL), lambda i, e, f: (i, 0), pipeline_mode=once),
        out_shape=jax.ShapeDtypeStruct((T, D_MODEL), F32),
        scratch_shapes=[pltpu.VMEM((N_EXPERTS, TB), F32),
                        pltpu.VMEM((TB // SB * CH[-1], D_MODEL), BF16), pltpu.VMEM((TB // SB * CH[-1], D_MODEL), F32)],
        compiler_params=_cparams(("parallel", "arbitrary", "arbitrary")),
        name="moe_top2",
    )(h, x, comb, combt, tri, w_up, w_up, wd)


def _prep_w_in(w):
    wt = w.T
    hq, hf, hi, hg = (wt[j * HG_WIDTH:(j + 1) * HG_WIDTH] for j in range(4))
    z = wt[_OFF_SSM:_OFF_SSM + SSM_WIDTH]
    xbc = wt[_OFF_SSM + SSM_WIDTH:_OFF_SSM + SSM_WIDTH + SSM_CONV_DIM]
    dt = wt[_OFF_ATT - SSM_HEADS:_OFF_ATT]
    att = wt[_OFF_ATT:_OFF_GATE]
    gates = wt[_OFF_GATE:]
    w16 = jnp.concatenate([gates, xbc, hq, hi, hg, z, att], axis=0).astype(BF16)
    w32 = jnp.concatenate([hf, dt, jnp.zeros((LANE - SSM_HEADS, D_MODEL), w.dtype)], axis=0).astype(BF16)
    return w16, w32


def _pad_lanes(v, n, value=0.0):
    v = v.reshape(1, -1).astype(F32)
    return jnp.pad(v, ((0, 0), (0, n - v.shape[1])), constant_values=value)


def _run_group(x, B, L, C_h, C_s, valid, NB, tm, layer_params, states, kv_bufs):
    depth = len(layer_params)
    st_out, kv_out = None, (None if kv_bufs is None else [])
    for l, P in enumerate(layer_params):
        st = states[l]
        a16, a32 = _norm_proj(x, P["mix_norm"], P["w16"], P["w32"], tm)
        a16b, a32b = a16.reshape(B, L, W16), a32.reshape(B, L, W32)
        y_hg, s_hg, y_ssm, s_ssm, s_conv = _recurrent_mixers(
            a16b, a32b, P["lb"], P["hg_norm"], st["hgrn_t"], st["conv"], P["conv_w"], P["conv_b"], P["dt_bias"],
            P["a_log"], P["d_skip"], P["ssm_norm"], st["ssm"], B, L, C_h, HG_SUB, valid, NB, l, depth, st_out)
        st_out = (s_hg, s_ssm, s_conv)
        y_hg, y_ssm = y_hg.reshape(B * L, HG_WIDTH), y_ssm.reshape(B * L, SSM_WIDTH)
        if kv_bufs is None:
            y_att, *kv_out = _attn_prompt(a16, P["gq"], P["gk"], P["bd"], B, L, l, depth, kv_out)
        else:
            y_att, *kv = _attn_sample(a16, kv_bufs, l, P["gq"], P["gk"], P["bd"], B)
            kv_out.append(kv)
        x = _merge(y_hg, y_ssm, y_att, a16, x, P["w_out_hg"], P["w_out_ssm"], P["w_out_att"], P["w_o"], tm)
        if l % 2 == 0:
            F = P["ffn"]
            nxt = layer_params[l + 1]["moe"] if l + 1 < len(layer_params) else {}
            pending = [k for k in ("w_up", "wd") if k + "_f32" in nxt and k not in nxt]
            cast = [nxt[k + "_f32"].reshape(-1, nxt[k + "_f32"].shape[-1]) for k in pending]
            x, done = _ffn(x, P["ffn_norm"], F["w_up"], F["wd"], min(tm, 256), cast)
            for k, w in zip(pending, done):
                nxt[k] = w.reshape(nxt[k + "_f32"].shape)
        else:
            F = P["moe"]
            h, comb, combt = _router(x, P["ffn_norm"], F["wr"], tm)
            T = x.shape[0]
            x = _moe(x, h, comb, combt, F["w_up"], F["wd"], min(MOE_BLOCK, T), min(MOE_HALF, T), min(MOE_SUB, T),
                     MOE_SLOTS, 2)
    return x, st_out, kv_out


def kernel(x_prompt, x_sample, state_hgrn, state_ssm, state_conv, cache_kv_w128, cache_kv_w512, cache_kv_w2048, w_in, mix_norm, hgrn_lb_logits, hgrn_norm, ssm_conv_w, ssm_conv_b, ssm_dt_bias, ssm_a_log, ssm_d, ssm_norm, att_q_norm, att_k_norm, w_out_hgrn, w_out_ssm, w_out_att, w_o, ffn_norm, w_ffn_up, w_ffn_down, w_router, w_moe_up, w_moe_down):
    depth = w_in.shape[0]
    Bp, Lp, _ = x_prompt.shape
    Bs, Ls, _ = x_sample.shape
    R = SAMPLE_ROWS
    for buf, w in zip((cache_kv_w128, cache_kv_w512, cache_kv_w2048), ATT_WINDOWS):
        assert buf.shape[2] == w, "sample attention assumes full window buffers"

    lb_soft = jax.nn.softmax(hgrn_lb_logits.astype(F32), axis=0)
    lb_table = jnp.cumsum(lb_soft, axis=0) - lb_soft[0]
    bd = jnp.kron(jnp.eye(ATT_HEADS, dtype=F32), jnp.full((ATT_HEAD_DIM, ATT_HEAD_DIM), 1.0 / ATT_HEAD_DIM, F32)).astype(BF16)

    layer_params = []
    for l in range(depth):
        w16, w32 = _prep_w_in(w_in[l])
        P = dict(
            w16=w16, w32=w32,
            mix_norm=mix_norm[l].reshape(1, -1), ffn_norm=ffn_norm[l].reshape(1, -1),
            lb=lb_table[l].reshape(1, -1), hg_norm=hgrn_norm[l].reshape(1, -1),
            conv_w=jnp.pad(ssm_conv_w[l], ((0, 8 - SSM_CONV), (0, 0))), conv_b=ssm_conv_b[l].reshape(1, -1),
            dt_bias=_pad_lanes(ssm_dt_bias[l], LANE), a_log=_pad_lanes(ssm_a_log[l], LANE),
            d_skip=jnp.repeat(ssm_d[l].astype(F32), SSM_HEAD_DIM).reshape(1, -1),
            ssm_norm=ssm_norm[l].reshape(1, -1),
            gq=jnp.tile(att_q_norm[l], (1, ATT_HEADS)), gk=jnp.tile(att_k_norm[l], (1, ATT_HEADS)), bd=bd,
            w_out_hg=w_out_hgrn[l].astype(BF16), w_out_ssm=w_out_ssm[l].astype(BF16),
            w_out_att=w_out_att[l].astype(BF16), w_o=w_o[l].astype(BF16),
        )
        if l % 2 == 0:
            P["ffn"] = dict(w_up=w_ffn_up[l // 2].astype(BF16), wd=w_ffn_down[l // 2].astype(BF16))
        else:
            P["moe"] = dict(wr=jnp.pad(w_router[l // 2].astype(F32), ((0, 0), (0, LANE - N_EXPERTS))),
                            w_up_f32=w_moe_up[l // 2], wd_f32=w_moe_down[l // 2])
        layer_params.append(P)

    zero_p = dict(hgrn_t=jnp.zeros((Bp, HG_WIDTH, HG_DIM), F32), ssm=jnp.zeros((Bp, SSM_WIDTH, SSM_STATE), F32),
                  conv=jnp.zeros((Bp, 8, SSM_CONV_DIM), F32))
    xp, st_p, kv_p = _run_group(x_prompt.reshape(Bp * Lp, D_MODEL), Bp, Lp, 64, 64, 64, 4, 512,
                                layer_params, [zero_p] * depth, None)

    xs = jnp.pad(x_sample, ((0, 0), (0, R - Ls), (0, 0))).reshape(Bs * R, D_MODEL)
    st_s = []
    for l in range(depth):
        st_s.append(dict(
            hgrn_t=jnp.swapaxes(state_hgrn[l], -1, -2).reshape(Bs, HG_WIDTH, HG_DIM),
            ssm=state_ssm[l].reshape(Bs, SSM_WIDTH, SSM_STATE),
            conv=jnp.pad(state_conv[l], ((0, 0), (8 - (SSM_CONV - 1), 0), (0, 0)))))
    kv_s = [jnp.transpose(c, (0, 1, 3, 4, 5, 2)).reshape(depth, Bs, 2 * ATT_WIDTH, c.shape[2])
            for c in (cache_kv_w128, cache_kv_w512, cache_kv_w2048)]
    xs, st_s, kv_s = _run_group(xs, Bs, R, R, R, Ls, 4, Bs * R, layer_params, st_s, kv_s)

    def states(st, B):
        s_hg, s_ssm, s_conv = st
        return (jnp.swapaxes(s_hg.reshape(depth, B, HG_HEADS, HG_DIM, HG_DIM), -1, -2),
                s_ssm.reshape(depth, B, SSM_HEADS, SSM_HEAD_DIM, SSM_STATE),
                s_conv[:, :, 8 - (SSM_CONV - 1):])

    kv_prompt = [jnp.transpose(kv.reshape(depth, Bp, 2, ATT_HEADS, ATT_HEAD_DIM, kv.shape[3]), (0, 1, 5, 2, 3, 4))
                 for kv in kv_p]
    kv_sample = [jnp.stack([kv_s[l][j][:, :Ls].reshape(Bs, Ls, 2, ATT_HEADS, ATT_HEAD_DIM) for l in range(depth)])
                 for j in range(3)]
    y_prompt = xp.reshape(Bp, Lp, D_MODEL)
    y_sample = xs.reshape(Bs, R, D_MODEL)[:, :Ls]
    return (y_prompt, y_sample, *states(st_p, Bp), *kv_prompt, *states(st_s, Bs), *kv_sample)
```

```python
import functools

import numpy as np
import jax
import jax.numpy as jnp
from jax import lax
from jax.experimental import pallas as pl
from jax.experimental.pallas import tpu as pltpu

F32 = jnp.float32
BF16 = jnp.bfloat16

D_MODEL = 1024
HG_HEADS, HG_DIM = 4, 128
HG_WIDTH = HG_HEADS * HG_DIM
SSM_HEADS, SSM_HEAD_DIM, SSM_GROUPS, SSM_STATE, SSM_CONV = 8, 64, 2, 128, 4
SSM_WIDTH = SSM_HEADS * SSM_HEAD_DIM
SSM_CONV_DIM = SSM_WIDTH + 2 * SSM_GROUPS * SSM_STATE
ATT_WINDOWS = (128, 512, 2048)
ATT_DILATIONS = (1, 4, 16)
ATT_ORDER = (2, 1, 0)
ATT_HEADS, ATT_HEAD_DIM = 4, 64
ATT_WIDTH = ATT_HEADS * ATT_HEAD_DIM
ATT_KEYS = 128
ATT_SCALE = ATT_HEAD_DIM ** -0.5
N_EXPERTS = 8
EPS = 1e-6
NEG = -1e30

_OFF_SSM = 4 * HG_WIDTH
_OFF_ATT = _OFF_SSM + SSM_WIDTH + SSM_CONV_DIM + SSM_HEADS
_OFF_GATE = _OFF_ATT + 3 * 3 * ATT_WIDTH

A16_GATE, A16_XBC, A16_HQ, A16_HI, A16_HG, A16_Z, A16_ATT = 0, 3072, 4096, 4608, 5120, 5632, 6144
W16 = 8448
A32_HF, A32_DT = 0, 512
W32 = 640

LANE = 128
SAMPLE_ROWS = 16
HG_SUB = 8
MOE_BLOCK = 2048
MOE_HALF = 1024
MOE_SUB = 512
MOE_SLOTS = (144, 160)
VMEM_LIMIT = 56 * 2 ** 20


def _cparams(sem):
    return pltpu.CompilerParams(dimension_semantics=sem, vmem_limit_bytes=VMEM_LIMIT)


def _dot(a, b):
    return jnp.dot(a, b, preferred_element_type=F32)


def _dot_nt(a, b):
    return lax.dot_general(a, b, (((1,), (1,)), ((), ())), preferred_element_type=F32)


def _dot_tn(a, b):
    return lax.dot_general(a, b, (((0,), (0,)), ((), ())), preferred_element_type=F32)


def _split3(x):
    hi = x.astype(BF16)
    r = x - hi.astype(F32)
    mid = r.astype(BF16)
    lo = (r - mid.astype(F32)).astype(BF16)
    return hi, mid, lo


def _cumsum_rows(x, tril_bf16):
    hi, mid, lo = _split3(x)
    return _dot(tril_bf16, hi) + _dot(tril_bf16, mid) + _dot(tril_bf16, lo)


def _sigmoid(x):
    return 0.5 * jnp.tanh(0.5 * x) + 0.5


def _silu(x):
    return x * _sigmoid(x)


def _tril_bf16(n):
    r = lax.broadcasted_iota(jnp.int32, (n, n), 0)
    c = lax.broadcasted_iota(jnp.int32, (n, n), 1)
    return jnp.where(c <= r, 1.0, 0.0).astype(BF16)


def _norm_proj_kernel(x_ref, g_ref, w16_ref, w32_ref, o16_ref, o32_ref, *, chunk):
    x = x_ref[...]
    h = (x * lax.rsqrt(jnp.mean(x * x, axis=-1, keepdims=True) + EPS) * g_ref[...]).astype(BF16)
    for c0 in range(0, W16, chunk):
        o16_ref[:, c0:c0 + chunk] = _dot_nt(h, w16_ref[c0:c0 + chunk, :]).astype(BF16)
    o32_ref[...] = _dot_nt(h, w32_ref[...])


def _norm_proj(x, gain, w16, w32, tm):
    T = x.shape[0]
    const = lambda i: (0, 0)
    return pl.pallas_call(
        functools.partial(_norm_proj_kernel, chunk=768),
        grid=(pl.cdiv(T, tm),),
        in_specs=[
            pl.BlockSpec((tm, D_MODEL), lambda i: (i, 0)),
            pl.BlockSpec((1, D_MODEL), const),
            pl.BlockSpec((W16, D_MODEL), const, pipeline_mode=pl.Buffered(1)),
            pl.BlockSpec((W32, D_MODEL), const, pipeline_mode=pl.Buffered(1)),
        ],
        out_specs=[
            pl.BlockSpec((tm, W16), lambda i: (i, 0)),
            pl.BlockSpec((tm, W32), lambda i: (i, 0)),
        ],
        out_shape=[jax.ShapeDtypeStruct((T, W16), BF16), jax.ShapeDtypeStruct((T, W32), F32)],
        compiler_params=_cparams(("parallel",)),
        name="norm_proj",
    )(x, gain, w16, w32)


def _hgrn_tables(C, sub):
    t = np.arange(C)[:, None]
    u = np.arange(C)[None, :]
    blocks = [u <= t]
    levels = []
    l = sub
    while l < C:
        upper = (t % (2 * l)) >= l
        b = (t // (2 * l)) * (2 * l) + l - 1
        blocks.append(np.where(upper, (u > b) & (u <= t), (u > t) & (u <= b)))
        levels.append(l)
        l *= 2
    for s in range(sub):
        r = (t // sub) * sub + s
        blocks.append((u > r) & (u <= t))
    sel = [u == (t // sub) * sub + s for s in range(sub)]
    d = np.concatenate(blocks, 0).astype(np.float32)
    sl = np.concatenate(sel, 0).astype(np.float32)
    table = np.block([[d, d, d, np.zeros_like(d)], [np.zeros((sl.shape[0], 3 * C), np.float32), sl]])
    return jnp.asarray(table, BF16), tuple(levels)


def _hgrn_chunk(q_ref, f_ref, i_ref, g_ref, lb_ref, gn_ref, tab_ref, y_ref, st_scr, *, C, sub, levels, valid):
    lb = lb_ref[...]
    fgate = lb + (1.0 - lb) * jax.nn.sigmoid(f_ref[...])
    logf = jnp.log(fgate)
    kk = 1.0 - fgate
    rows = lax.broadcasted_iota(jnp.int32, (C, HG_WIDTH), 0)
    if valid < C:
        logf = jnp.where(rows < valid, logf, 0.0)
        kk = jnp.where(rows < valid, kk, 0.0)
    qa = _silu(q_ref[...].astype(F32))
    va = i_ref[...].astype(F32)
    ga = g_ref[...].astype(F32)

    ed = _dot(tab_ref[...], jnp.concatenate([*_split3(logf), kk.astype(BF16)], axis=0))
    nblk = 1 + len(levels) + sub
    blk = lambda j: ed[j * C:(j + 1) * C]
    G = blk(0)
    ksel = ed[nblk * C:]

    row_c = lax.broadcasted_iota(jnp.int32, (C, C), 0)
    lane_c = lax.broadcasted_iota(jnp.int32, (C, C), 1)
    off_c = lane_c - (row_c // sub) * sub

    A = [jnp.zeros((C, C), F32) for _ in range(HG_HEADS)]
    for s in range(sub):
        prod = qa * ksel[s * C:(s + 1) * C] * jnp.exp(blk(1 + len(levels) + s))
        for h in range(HG_HEADS):
            col = jnp.sum(prod[:, h * HG_DIM:(h + 1) * HG_DIM], axis=-1, keepdims=True)
            A[h] = jnp.where(off_c == s, col, A[h])
    A = [jnp.where(lane_c <= row_c, a, 0.0) for a in A]

    for li, l in enumerate(levels):
        upper = (rows % (2 * l)) >= l
        scaled = jnp.where(upper, qa, kk) * jnp.exp(blk(1 + li))
        qt = jnp.where(upper, scaled, 0.0).astype(BF16)
        kt = jnp.where(upper, 0.0, scaled).astype(BF16)
        same = (row_c // (2 * l)) == (lane_c // (2 * l))
        for h in range(HG_HEADS):
            sl = slice(h * HG_DIM, (h + 1) * HG_DIM)
            a_l = _dot_nt(qt[:, sl], kt[:, sl])
            A[h] = A[h] + (jnp.where(same, a_l, 0.0) if 2 * l < C else a_l)

    outs = []
    for h in range(HG_HEADS):
        sl = slice(h * HG_DIM, (h + 1) * HG_DIM)
        Gh, qh, kh = G[:, sl], qa[:, sl], kk[:, sl]
        vb = va[:, sl].astype(BF16)
        st = st_scr[sl, :]
        o = _dot(A[h].astype(BF16), vb) + _dot_nt((qh * jnp.exp(Gh)).astype(BF16), st.astype(BF16))
        g_last = Gh[C - 1:C]
        kdec = kh * jnp.exp(g_last - Gh)
        st_scr[sl, :] = jnp.exp(g_last) * st + _dot_tn(vb, kdec.astype(BF16))
        on = o * lax.rsqrt(jnp.mean(o * o, axis=-1, keepdims=True) + EPS) * gn_ref[...]
        outs.append(on * _silu(ga[:, sl]))
    y_ref[...] = jnp.concatenate(outs, axis=1).astype(BF16)


def _ssd_chunk(xbc_ref, z_ref, dt_ref, cw_ref, cb_ref, dtb_ref, alog_ref, dsk_ref, sn_ref,
               y_ref, ubuf, hs, *, C, valid):
    ubuf[8:8 + C, :] = xbc_ref[...].astype(F32)
    acc = cb_ref[...] + ubuf[5:5 + C, :] * cw_ref[0:1, :]
    for j in range(1, SSM_CONV):
        acc = acc + ubuf[5 + j:5 + j + C, :] * cw_ref[j:j + 1, :]
    xc = _silu(acc)
    ubuf[0:8, :] = ubuf[valid:valid + 8, :]

    x = dt_ref[...] + dtb_ref[...]
    dt = jnp.maximum(x, 0.0) + jnp.log1p(jnp.exp(-jnp.abs(x)))
    if valid < C:
        rows = lax.broadcasted_iota(jnp.int32, (C, LANE), 0)
        dt = jnp.where(rows < valid, dt, 0.0)
    a = -jnp.exp(alog_ref[...])
    cum = _cumsum_rows(dt * a, _tril_bf16(C))
    cum_t = cum.T
    dt_t = dt.T

    ri = lax.broadcasted_iota(jnp.int32, (C, C), 0)
    cj = lax.broadcasted_iota(jnp.int32, (C, C), 1)
    tril = cj <= ri
    lane = lax.broadcasted_iota(jnp.int32, (C, LANE), 1)
    srow = lax.broadcasted_iota(jnp.int32, (LANE, LANE), 0)
    bm = [xc[:, SSM_WIDTH + g * SSM_STATE:SSM_WIDTH + (g + 1) * SSM_STATE].astype(BF16) for g in range(SSM_GROUPS)]
    off_c = SSM_WIDTH + SSM_GROUPS * SSM_STATE
    cm = [xc[:, off_c + g * SSM_STATE:off_c + (g + 1) * SSM_STATE].astype(BF16) for g in range(SSM_GROUPS)]
    cb = [_dot_nt(cm[g], bm[g]) for g in range(SSM_GROUPS)]

    ys = []
    for p in range(SSM_HEADS // 2):
        g = (2 * p) // (SSM_HEADS // SSM_GROUPS)
        xp = xc[:, p * LANE:(p + 1) * LANE]
        yp = None
        colw = []
        for k, hd in enumerate((2 * p, 2 * p + 1)):
            colb = jnp.broadcast_to(cum[:, hd:hd + 1], (C, LANE))
            rowb = jnp.broadcast_to(cum_t[hd:hd + 1, :], (C, C))
            dtr = jnp.broadcast_to(dt_t[hd:hd + 1, :], (C, C))
            w = cb[g] * jnp.exp(jnp.where(tril, colb[:, :C] - rowb, NEG)) * dtr
            half = (lane < SSM_HEAD_DIM) if k == 0 else (lane >= SSM_HEAD_DIM)
            t = _dot(w.astype(BF16), jnp.where(half, xp, 0.0).astype(BF16))
            yp = t if yp is None else yp + t
            colw.append((colb, jnp.broadcast_to(dt[:, hd:hd + 1], (C, LANE))))
        first = lane < SSM_HEAD_DIM
        cum_e = jnp.where(first, colw[0][0], colw[1][0])
        dt_e = jnp.where(first, colw[0][1], colw[1][1])
        hp = hs[p * LANE:(p + 1) * LANE, :]
        yp = yp + jnp.exp(cum_e) * _dot_nt(cm[g], hp.astype(BF16))
        c_last = cum_e[C - 1:C, :]
        w_in = jnp.exp(c_last - cum_e) * dt_e
        dec = jnp.where(srow < SSM_HEAD_DIM,
                        jnp.broadcast_to(cum[C - 1:C, 2 * p:2 * p + 1], (LANE, LANE)),
                        jnp.broadcast_to(cum[C - 1:C, 2 * p + 1:2 * p + 2], (LANE, LANE)))
        hs[p * LANE:(p + 1) * LANE, :] = jnp.exp(dec) * hp + _dot_tn((xp * w_in).astype(BF16), bm[g])
        ys.append(yp)
    y = jnp.concatenate(ys, axis=1) + dsk_ref[...] * xc[:, :SSM_WIDTH]
    yz = y * _silu(z_ref[...].astype(F32))
    y_ref[...] = (yz * lax.rsqrt(jnp.mean(yz * yz, axis=-1, keepdims=True) + EPS) * sn_ref[...]).astype(BF16)


def _rec_kernel(q_ref, f_ref, i_ref, g_ref, lb_ref, gn_ref, s0_ref, tab_ref,
                xbc_ref, z_ref, dt_ref, pre_ref, cw_ref, cb_ref, dtb_ref, alog_ref, dsk_ref, sn_ref, h0_ref,
                *rest, NB, nchunks, first, hg_kw, ssd_kw):
    yh_ref, sout_ref, ys_ref, hout_ref, cout_ref, st_scr, ubuf, hs = rest[-8:]
    ci = pl.program_id(1)

    @pl.when(ci == 0)
    def _():
        st_scr[...] = s0_ref[...]
        ubuf[:, 0:8, :] = pre_ref[...]
        hs[...] = h0_ref[...]

    for j in range(NB):
        _hgrn_chunk(q_ref.at[j], f_ref.at[j], i_ref.at[j], g_ref.at[j], lb_ref, gn_ref, tab_ref,
                    yh_ref.at[j], st_scr.at[j], **hg_kw)
        _ssd_chunk(xbc_ref.at[j], z_ref.at[j], dt_ref.at[j], cw_ref, cb_ref, dtb_ref, alog_ref, dsk_ref, sn_ref,
                   ys_ref.at[j], ubuf.at[j], hs.at[j], **ssd_kw)

    @pl.when(ci == nchunks - 1)
    def _():
        for d in range(sout_ref.shape[0]):
            sout_ref[d] = st_scr[...]
            hout_ref[d] = hs[...]
            cout_ref[d] = ubuf[:, 0:8, :]


def _recurrent_mixers(a16, a32, lb, gn, s0t, prefix, cw, cb, dtb, alog, dsk, sn, h0, B, L, C, sub, valid, NB,
                      layer, depth, prev):
    nch = L // C
    table, levels = _hgrn_tables(C, sub)
    const = lambda b, c: (0, 0)
    bat = lambda b, c: (b, 0, 0)
    blk = lambda w, col: pl.BlockSpec((NB, C, w), lambda b, c: (b, c, col))
    first = prev is None
    slot = (lambda rows, w: pl.BlockSpec((depth, NB, rows, w), lambda b, c: (0, b, 0, 0))) if first else \
           (lambda rows, w: pl.BlockSpec((1, NB, rows, w), lambda b, c: (layer, b, 0, 0)))
    return pl.pallas_call(
        functools.partial(_rec_kernel, NB=NB, nchunks=nch, first=first,
                          hg_kw=dict(C=C, sub=sub, levels=levels, valid=valid), ssd_kw=dict(C=C, valid=valid)),
        grid=(B // NB, nch),
        in_specs=[
            blk(HG_WIDTH, A16_HQ // HG_WIDTH), blk(HG_WIDTH, A32_HF // HG_WIDTH),
            blk(HG_WIDTH, A16_HI // HG_WIDTH), blk(HG_WIDTH, A16_HG // HG_WIDTH),
            pl.BlockSpec((1, HG_WIDTH), const),
            pl.BlockSpec((1, HG_DIM), const),
            pl.BlockSpec((NB, HG_WIDTH, HG_DIM), bat),
            pl.BlockSpec(table.shape, const),
            blk(SSM_CONV_DIM, A16_XBC // SSM_CONV_DIM), blk(SSM_WIDTH, A16_Z // SSM_WIDTH), blk(LANE, A32_DT // LANE),
            pl.BlockSpec((NB, 8, SSM_CONV_DIM), bat),
            pl.BlockSpec((8, SSM_CONV_DIM), const),
            pl.BlockSpec((1, SSM_CONV_DIM), const),
            pl.BlockSpec((1, LANE), const),
            pl.BlockSpec((1, LANE), const),
            pl.BlockSpec((1, SSM_WIDTH), const),
            pl.BlockSpec((1, SSM_WIDTH), const),
            pl.BlockSpec((NB, SSM_WIDTH, SSM_STATE), bat),
        ] + ([] if first else [pl.BlockSpec(memory_space=pl.ANY)] * 3),
        out_specs=[
            blk(HG_WIDTH, 0), slot(HG_WIDTH, HG_DIM),
            blk(SSM_WIDTH, 0), slot(SSM_WIDTH, SSM_STATE), slot(8, SSM_CONV_DIM),
        ],
        out_shape=[jax.ShapeDtypeStruct((B, L, HG_WIDTH), BF16),
                   jax.ShapeDtypeStruct((depth, B, HG_WIDTH, HG_DIM), F32),
                   jax.ShapeDtypeStruct((B, L, SSM_WIDTH), BF16),
                   jax.ShapeDtypeStruct((depth, B, SSM_WIDTH, SSM_STATE), F32),
                   jax.ShapeDtypeStruct((depth, B, 8, SSM_CONV_DIM), F32)],
        input_output_aliases={} if first else {19: 1, 20: 3, 21: 4},
        scratch_shapes=[pltpu.VMEM((NB, HG_WIDTH, HG_DIM), F32), pltpu.VMEM((NB, C + 8, SSM_CONV_DIM), F32),
                        pltpu.VMEM((NB, SSM_WIDTH, SSM_STATE), F32)],
        compiler_params=_cparams(("parallel", "arbitrary")),
        name="hgrn2_ssd",
    )(a16, a32, a16, a16, lb, gn, s0t, table, a16, a16, a32, prefix, cw, cb, dtb, alog, dsk, sn, h0,
      *([] if first else prev))


def _head_norm(x, gain, bd):
    ms = _dot((x * x).astype(BF16), bd)
    return x * (lax.rsqrt(ms + EPS) * gain)


def _head_masks(rows):
    lane = lax.broadcasted_iota(jnp.int32, (rows, ATT_WIDTH), 1)
    return [(lane >= h * ATT_HEAD_DIM) & (lane < (h + 1) * ATT_HEAD_DIM) for h in range(ATT_HEADS)]


def _stack_heads(q, masks):
    return jnp.concatenate([jnp.where(m, q, 0.0) for m in masks], axis=0).astype(BF16)


def _unstack_heads(pv, m, d, masks):
    R = pv.shape[0] // ATT_HEADS
    acc, mx, den = pv[0:R], jnp.broadcast_to(m[0:R], (R, ATT_WIDTH)), jnp.broadcast_to(d[0:R], (R, ATT_WIDTH))
    for h in range(1, ATT_HEADS):
        sl = slice(h * R, (h + 1) * R)
        acc = jnp.where(masks[h], pv[sl], acc)
        mx = jnp.where(masks[h], m[sl], mx)
        den = jnp.where(masks[h], d[sl], den)
    return acc, mx, den


def _attend(qb, kb, vb, bias4, masks):
    s = _dot_nt(_stack_heads(qb, masks), kb) + bias4
    m = jnp.max(s, axis=-1, keepdims=True)
    p = jnp.exp(s - m)
    d = jnp.sum(p, axis=-1, keepdims=True)
    return _unstack_heads(_dot(p.astype(BF16), vb), m, d, masks)


def _ld2(pair, s):
    return jnp.concatenate([pair[0][s, :], pair[1][s, :]], axis=1)


def _st2(pair, s, val):
    pair[0][s, :] = val[:, :LANE]
    pair[1][s, :] = val[:, LANE:]


def _attn_prompt_kernel(att_ref, gq_ref, gk_ref, bd_ref, *rest, L):
    y_ref, kv0_ref, kv1_ref, kv2_ref = rest[-16:-12]
    scr = rest[-12:]
    qn, kn, vv, ya, ma, za = (scr[2 * j:2 * j + 2] for j in range(6))
    g = pl.program_id(1)
    RB = 256
    QB = ATT_KEYS

    def group(step, gi, dil, keep, kv_ref):
        M = L // dil
        nb = M // QB
        KW = min(2 * QB, M)
        bd = bd_ref[...]

        def norm_body(i, carry):
            r0 = pl.multiple_of(i * RB, RB)
            x = att_ref[pl.ds(r0, RB), :].astype(F32)
            _st2(qn, pl.ds(r0, RB), _head_norm(x[:, :ATT_WIDTH], gq_ref[gi:gi + 1, :] * ATT_SCALE, bd))
            _st2(kn, pl.ds(r0, RB), _head_norm(x[:, ATT_WIDTH:2 * ATT_WIDTH], gk_ref[gi:gi + 1, :], bd))
            _st2(vv, pl.ds(r0, RB), x[:, 2 * ATT_WIDTH:])
            return carry

        lax.fori_loop(0, L // RB, norm_body, 0)
        cw = min(keep, RB)
        for c0 in range(0, keep, cw):
            rows = slice(L - keep + c0, L - keep + c0 + cw)
            for p, half in enumerate((kn[0], kn[1], vv[0], vv[1])):
                piece = half[rows, :].T
                for d in range(kv_ref.shape[0]):
                    kv_ref[d, 0, p * LANE:(p + 1) * LANE, c0:c0 + cw] = piece

        masks = _head_masks(QB)
        iq = lax.broadcasted_iota(jnp.int32, (ATT_HEADS * QB, KW), 0) & (QB - 1)
        ik = lax.broadcasted_iota(jnp.int32, (ATT_HEADS * QB, KW), 1)

        def run_blocks(first, count, lead):
            dist = (iq - ik) if lead else (QB + iq - ik)
            bias4 = jnp.where((dist >= 0) & (dist <= ATT_KEYS), 0.0, NEG)
            unroll = max(u for u in (4, 3, 2, 1) if count % u == 0)

            def blk_body(j, carry):
                loaded = []
                for u in range(unroll):
                    it = first + j * unroll + u
                    n = it // dil
                    r = it - n * dil
                    k0 = 0 if lead else (n - 1) * QB
                    if dil == 1:
                        qs = pl.ds(pl.multiple_of(n * QB, QB), QB)
                        ks = pl.ds(pl.multiple_of(k0, QB), KW)
                    else:
                        qs = pl.ds(n * QB * dil + r, QB, stride=dil)
                        ks = pl.ds(k0 * dil + r, KW, stride=dil)
                    old = None if step == 0 else (_ld2(ya, qs), _ld2(ma, qs), _ld2(za, qs))
                    loaded.append((qs, _ld2(qn, qs), _ld2(kn, ks).astype(BF16), _ld2(vv, ks).astype(BF16), old))
                for qs, qb, kb, vb, old in loaded:
                    acc, mx, den = _attend(qb, kb, vb, bias4, masks)
                    if step > 0:
                        yo, mo, zo = old
                        mn = jnp.maximum(mo, mx)
                        eo = jnp.exp(mo - mn)
                        en = jnp.exp(mx - mn)
                        acc, mx, den = yo * eo + acc * en, mn, zo * eo + den * en
                    _st2(ya, qs, acc)
                    _st2(ma, qs, mx)
                    _st2(za, qs, den)
                return carry

            lax.fori_loop(0, count // unroll, blk_body, 0)

        run_blocks(0, dil, True)
        if nb > 1:
            run_blocks(dil, dil * (nb - 1), False)

    for step, gi in enumerate(ATT_ORDER):
        @pl.when(g == step)
        def _(step=step, gi=gi):
            group(step, gi, ATT_DILATIONS[gi], min(ATT_WINDOWS[gi], L), (kv0_ref, kv1_ref, kv2_ref)[gi])

    @pl.when(g == 2)
    def _():
        def out_body(i, carry):
            r0 = pl.multiple_of(i * RB, RB)
            y_ref[pl.ds(r0, RB), :] = (_ld2(ya, pl.ds(r0, RB)) / _ld2(za, pl.ds(r0, RB))).astype(BF16)
            return carry
        lax.fori_loop(0, L // RB, out_body, 0)


def _attn_prompt(a16, gq, gk, bd, B, L, layer, depth, prev):
    keeps = [min(w, L) for w in ATT_WINDOWS]
    const = lambda b, g: (0, 0)
    first = prev is None
    slot = (lambda k: pl.BlockSpec((depth, 1, 2 * ATT_WIDTH, k), lambda b, g: (0, b, 0, 0))) if first else \
           (lambda k: pl.BlockSpec((1, 1, 2 * ATT_WIDTH, k), lambda b, g: (layer, b, 0, 0)))
    return pl.pallas_call(
        functools.partial(_attn_prompt_kernel, L=L),
        grid=(B, 3),
        in_specs=[
            pl.BlockSpec((L, 3 * ATT_WIDTH), lambda b, g: (b, A16_ATT // (3 * ATT_WIDTH) + 2 - g)),
            pl.BlockSpec((3, ATT_WIDTH), const),
            pl.BlockSpec((3, ATT_WIDTH), const),
            pl.BlockSpec((ATT_WIDTH, ATT_WIDTH), const),
        ] + ([] if first else [pl.BlockSpec(memory_space=pl.ANY)] * 3),
        out_specs=[pl.BlockSpec((L, ATT_WIDTH), lambda b, g: (b, 0))] + [slot(k) for k in keeps],
        out_shape=[jax.ShapeDtypeStruct((B * L, ATT_WIDTH), BF16)]
        + [jax.ShapeDtypeStruct((depth, B, 2 * ATT_WIDTH, k), F32) for k in keeps],
        input_output_aliases={} if first else {4: 1, 5: 2, 6: 3},
        scratch_shapes=[pltpu.VMEM((L, LANE), F32) for _ in range(12)],
        compiler_params=_cparams(("parallel", "arbitrary")),
        name="attn_prompt",
    )(a16, gq, gk, bd, *([] if first else prev))


def _attn_sample_kernel(a0_ref, a1_ref, a2_ref, b0_ref, b1_ref, b2_ref, gq_ref, gk_ref, bd_ref,
                        y_ref, k0_ref, k1_ref, k2_ref):
    R = SAMPLE_ROWS
    bd = bd_ref[...]
    masks = _head_masks(R)
    ya = ma = za = None
    for gi, (a_ref, b_ref, k_ref) in enumerate(((a0_ref, b0_ref, k0_ref), (a1_ref, b1_ref, k1_ref),
                                                (a2_ref, b2_ref, k2_ref))):
        dil = ATT_DILATIONS[gi]
        wb = b_ref.shape[3]
        x = a_ref[...].astype(F32)
        q = _head_norm(x[:, :ATT_WIDTH], gq_ref[gi:gi + 1, :] * ATT_SCALE, bd)
        k = _head_norm(x[:, ATT_WIDTH:2 * ATT_WIDTH], gk_ref[gi:gi + 1, :], bd)
        v = x[:, 2 * ATT_WIDTH:]
        k_ref[0] = jnp.concatenate([k[:8], v[:8]], axis=1)
        k_t = b_ref[0, 0, 0:ATT_WIDTH, :].astype(BF16)
        v_t = b_ref[0, 0, ATT_WIDTH:2 * ATT_WIDTH, :].astype(BF16)
        kb, vb = k.astype(BF16), v.astype(BF16)

        def bias(dist):
            return jnp.where((dist >= 0) & (dist <= ATT_KEYS * dil) & ((dist & (dil - 1)) == 0), 0.0, NEG)

        R4 = ATT_HEADS * R
        row_b = lax.broadcasted_iota(jnp.int32, (R4, wb), 0) & (R - 1)
        row_n = lax.broadcasted_iota(jnp.int32, (R4, R), 0) & (R - 1)
        qs = _stack_heads(q, masks)
        s_b = _dot(qs, k_t) + bias(wb + row_b - lax.broadcasted_iota(jnp.int32, (R4, wb), 1))
        s_n = _dot_nt(qs, kb) + bias(row_n - lax.broadcasted_iota(jnp.int32, (R4, R), 1))
        m = jnp.maximum(jnp.max(s_b, axis=-1, keepdims=True), jnp.max(s_n, axis=-1, keepdims=True))
        p_b = jnp.exp(s_b - m)
        p_n = jnp.exp(s_n - m)
        d = jnp.sum(p_b, axis=-1, keepdims=True) + jnp.sum(p_n, axis=-1, keepdims=True)
        pv = _dot_nt(p_b.astype(BF16), v_t) + _dot(p_n.astype(BF16), vb)
        acc, mx, den = _unstack_heads(pv, m, d, masks)
        if gi == 0:
            ya, ma, za = acc, mx, den
        else:
            mn = jnp.maximum(ma, mx)
            eo = jnp.exp(ma - mn)
            en = jnp.exp(mx - mn)
            ya, za, ma = ya * eo + acc * en, za * eo + den * en, mn
    y_ref[...] = (ya / za).astype(BF16)


def _attn_sample(a16, bufs, layer, gq, gk, bd, B):
    R = SAMPLE_ROWS
    const = lambda b: (0, 0)
    bat = lambda b: (b, 0, 0)
    base = A16_ATT // (3 * ATT_WIDTH)
    return pl.pallas_call(
        _attn_sample_kernel,
        grid=(B,),
        in_specs=[pl.BlockSpec((R, 3 * ATT_WIDTH), functools.partial(lambda b, j: (b, j), j=base + g))
                  for g in range(3)]
        + [pl.BlockSpec((1, 1, 2 * ATT_WIDTH, buf.shape[3]), lambda b: (layer, b, 0, 0)) for buf in bufs]
        + [pl.BlockSpec((3, ATT_WIDTH), const), pl.BlockSpec((3, ATT_WIDTH), const),
           pl.BlockSpec((ATT_WIDTH, ATT_WIDTH), const)],
        out_specs=[pl.BlockSpec((R, ATT_WIDTH), lambda b: (b, 0))]
        + [pl.BlockSpec((1, 8, 2 * ATT_WIDTH), bat) for _ in range(3)],
        out_shape=[jax.ShapeDtypeStruct((B * R, ATT_WIDTH), BF16)]
        + [jax.ShapeDtypeStruct((B, 8, 2 * ATT_WIDTH), F32) for _ in range(3)],
        compiler_params=_cparams(("parallel",)),
        name="attn_sample",
    )(a16, a16, a16, *bufs, gq, gk, bd)


def _merge_kernel(yh_ref, ys_ref, ya_ref, gt_ref, x_ref, wh_ref, ws_ref, wa_ref, wo_ref, o_ref):
    gt = _sigmoid(gt_ref[...].astype(F32))
    mixed = (gt[:, 0:D_MODEL] * _dot(yh_ref[...], wh_ref[...])
             + gt[:, D_MODEL:2 * D_MODEL] * _dot(ys_ref[...], ws_ref[...])
             + gt[:, 2 * D_MODEL:] * _dot(ya_ref[...], wa_ref[...]))
    o_ref[...] = x_ref[...] + _dot(mixed.astype(BF16), wo_ref[...])


def _merge(y_hg, y_ssm, y_att, a16, x, wh, ws, wa, wo, tm):
    T = x.shape[0]
    const = lambda i: (0, 0)
    rowb = lambda w: pl.BlockSpec((tm, w), lambda i: (i, 0))
    wsp = lambda w: pl.BlockSpec(w.shape, const, pipeline_mode=pl.Buffered(1))
    return pl.pallas_call(
        _merge_kernel,
        grid=(pl.cdiv(T, tm),),
        in_specs=[rowb(HG_WIDTH), rowb(SSM_WIDTH), rowb(ATT_WIDTH),
                  pl.BlockSpec((tm, 3 * D_MODEL), lambda i: (i, A16_GATE // (3 * D_MODEL))),
                  rowb(D_MODEL), wsp(wh), wsp(ws), wsp(wa), wsp(wo)],
        out_specs=rowb(D_MODEL),
        out_shape=jax.ShapeDtypeStruct((T, D_MODEL), F32),
        compiler_params=_cparams(("parallel",)),
        name="merge_out_proj",
    )(y_hg, y_ssm, y_att, a16, x, wh, ws, wa, wo)


def _ffn_kernel(x_ref, g_ref, wg_ref, wu_ref, wd_ref, *rest):
    n = (len(rest) - 1) // 2
    cast_in, o_ref, cast_out = rest[:n], rest[n], rest[n + 1:]
    x = x_ref[...]
    h = (x * lax.rsqrt(jnp.mean(x * x, axis=-1, keepdims=True) + EPS) * g_ref[...]).astype(BF16)
    a = _silu(_dot(h, wg_ref[...])) * _dot(h, wu_ref[...])
    o_ref[...] = x + _dot(a.astype(BF16), wd_ref[...])
    for src, dst in zip(cast_in, cast_out):
        dst[...] = src[...].astype(BF16)


def _ffn(x, gain, w_up, wd, tm, cast=()):
    T = x.shape[0]
    dff = wd.shape[0]
    steps = pl.cdiv(T, tm)
    const = lambda i: (0, 0)
    wsp = lambda w: pl.BlockSpec(w.shape, const, pipeline_mode=pl.Buffered(1))
    slab = [pl.BlockSpec((w.shape[0] // steps, w.shape[1]), lambda i: (i, 0)) for w in cast]
    for w in cast:
        assert w.shape[0] % (16 * steps) == 0
    outs = pl.pallas_call(
        _ffn_kernel,
        grid=(steps,),
        in_specs=[pl.BlockSpec((tm, D_MODEL), lambda i: (i, 0)), pl.BlockSpec((1, D_MODEL), const),
                  pl.BlockSpec((D_MODEL, dff), const, pipeline_mode=pl.Buffered(1)),
                  pl.BlockSpec((D_MODEL, dff), lambda i: (0, 1), pipeline_mode=pl.Buffered(1)),
                  wsp(wd)] + slab,
        out_specs=[pl.BlockSpec((tm, D_MODEL), lambda i: (i, 0))] + slab,
        out_shape=[jax.ShapeDtypeStruct((T, D_MODEL), F32)] + [jax.ShapeDtypeStruct(w.shape, BF16) for w in cast],
        compiler_params=_cparams(("parallel",)),
        name="ffn_dense",
    )(x, gain, w_up, w_up, wd, *cast)
    return outs[0], outs[1:]


def _router_kernel(x_ref, g_ref, wr_ref, h_ref, comb_ref, combt_ref):
    x = x_ref[...]
    h = x * lax.rsqrt(jnp.mean(x * x, axis=-1, keepdims=True) + EPS) * g_ref[...]
    hb = h.astype(BF16)
    h_ref[...] = hb
    h_lo = (h - hb.astype(F32)).astype(BF16)
    w = wr_ref[...]
    w_hi = w.astype(BF16)
    w_lo = (w - w_hi.astype(F32)).astype(BF16)
    logits = _dot(hb, w_hi) + _dot(hb, w_lo) + _dot(h_lo, w_hi)
    lane = lax.broadcasted_iota(jnp.int32, logits.shape, 1)
    logits = jnp.where(lane < N_EXPERTS, logits, NEG)
    v1 = jnp.max(logits, axis=-1, keepdims=True)
    i1 = jnp.min(jnp.where(logits == v1, lane, LANE), axis=-1, keepdims=True)
    rest = jnp.where(lane == i1, NEG, logits)
    v2 = jnp.max(rest, axis=-1, keepdims=True)
    i2 = jnp.min(jnp.where(rest == v2, lane, LANE), axis=-1, keepdims=True)
    e = jnp.exp(v2 - v1)
    w1 = 1.0 / (1.0 + e)
    w2 = e / (1.0 + e)
    comb = jnp.where(lane == i1, w1, 0.0) + jnp.where(lane == i2, w2, 0.0)
    comb_ref[...] = comb
    sb = combt_ref.shape[2]
    comb_t = comb.T[0:N_EXPERTS, :]
    for j in range(combt_ref.shape[0]):
        combt_ref[j] = comb_t[:, j * sb:(j + 1) * sb]


def _router(x, gain, wr, tm, SB):
    T = x.shape[0]
    assert tm % SB == 0
    const = lambda i: (0, 0)
    return pl.pallas_call(
        _router_kernel,
        grid=(pl.cdiv(T, tm),),
        in_specs=[pl.BlockSpec((tm, D_MODEL), lambda i: (i, 0)), pl.BlockSpec((1, D_MODEL), const),
                  pl.BlockSpec((D_MODEL, LANE), const)],
        out_specs=[pl.BlockSpec((tm, D_MODEL), lambda i: (i, 0)), pl.BlockSpec((tm, LANE), lambda i: (i, 0)),
                   pl.BlockSpec((tm // SB, N_EXPERTS, SB), lambda i: (i, 0, 0))],
        out_shape=[jax.ShapeDtypeStruct((T, D_MODEL), BF16), jax.ShapeDtypeStruct((T, LANE), F32),
                   jax.ShapeDtypeStruct((T // SB, N_EXPERTS, SB), F32)],
        compiler_params=_cparams(("parallel",)),
        name="moe_router",
    )(x, gain, wr)


def _moe_kernel(h_ref, x_ref, comb_ref, combt_ref, tri_ref, wg_ref, wu_ref, wd_ref, o_ref,
                rank_scr, hs_scr, ys_scr, *, HB, SB, CH, nf):
    e = pl.program_id(1)
    f = pl.program_id(2)
    TB = h_ref.shape[0]
    nsub = HB // SB

    @pl.when((e == 0) & (f == 0))
    def _():
        o_ref[...] = x_ref[...]
        for j in range(TB // SB):
            rank_scr[j] = _dot(jnp.where(combt_ref[j] > 0.0, 1.0, 0.0).astype(BF16), tri_ref[...])

    def expert(hs):
        a = _silu(_dot(hs, wg_ref[0])) * _dot(hs, wu_ref[0])
        return _dot(a.astype(BF16), wd_ref[0])

    def half_block(hb, carry):
        subs = [pl.ds(pl.multiple_of(hb * HB + j * SB, SB), SB) for j in range(nsub)]
        routed = [combt_ref[hb * nsub + j, pl.ds(e, 1), :] > 0.0 for j in range(nsub)]
        rank = [rank_scr[hb * nsub + j, pl.ds(e, 1), :] for j in range(nsub)]
        count = None
        for r in routed:
            cj = jnp.sum(jnp.where(r, 1.0, 0.0)).astype(jnp.int32)
            count = cj if count is None else jnp.maximum(count, cj)

        def onehots(base, ch):
            slot = lax.broadcasted_iota(jnp.int32, (ch, SB), 0).astype(F32)
            return [jnp.where((rk - base == slot) & rt, 1.0, 0.0).astype(BF16) for rk, rt in zip(rank, routed)]

        def gather(ps):
            return jnp.concatenate([_dot(p, h_ref[sb, :]) for p, sb in zip(ps, subs)], axis=0).astype(BF16)

        def scatter(ps, ys, ch):
            lane = lax.broadcasted_iota(jnp.int32, (HB, LANE), 1)
            comb = comb_ref[pl.ds(pl.multiple_of(hb * HB, HB), HB), :]
            gate = jnp.sum(jnp.where(lane == e, comb, 0.0), axis=-1, keepdims=True)
            for j, (p, sb) in enumerate(zip(ps, subs)):
                o_ref[sb, :] += gate[j * SB:(j + 1) * SB] * _dot_tn(p, ys[j * ch:(j + 1) * ch])

        def single_pass(ch):
            ps = onehots(0.0, ch)
            rows = pl.ds(pl.multiple_of(hb * (nsub * CH[-1]), 16), nsub * ch)

            @pl.when(f == 0)
            def _():
                hs = gather(ps)
                hs_scr[rows, :] = hs
                ys_scr[rows, :] = expert(hs)

            if nf > 2:
                @pl.when((f > 0) & (f < nf - 1))
                def _():
                    ys_scr[rows, :] += expert(hs_scr[rows, :])

            @pl.when(f == nf - 1)
            def _():
                scatter(ps, (ys_scr[rows, :] + expert(hs_scr[rows, :])).astype(BF16), ch)

        def overflow_pass(c, carry):
            ch = CH[-1]
            ps = onehots((c * ch).astype(F32), ch)
            scatter(ps, expert(gather(ps)).astype(BF16), ch)
            return carry

        lo = 0
        for ch in CH:
            @pl.when((count > lo) & (count <= ch))
            def _(ch=ch):
                single_pass(ch)
            lo = ch

        @pl.when(count > lo)
        def _():
            lax.fori_loop(0, (count + lo - 1) // lo, overflow_pass, 0)

        return carry

    lax.fori_loop(0, TB // HB, half_block, 0)


def _moe(x, h, comb, combt, w_up, wd, TB, HB, SB, CH, nf):
    assert nf >= 2, "the first and last d_ff positions are separate code paths"
    T = x.shape[0]
    dff = wd.shape[1]
    tf = dff // nf
    tri = jnp.triu(jnp.ones((SB, SB), BF16), k=1)
    once = pl.Buffered(1)
    return pl.pallas_call(
        functools.partial(_moe_kernel, HB=HB, SB=SB, CH=CH, nf=nf),
        grid=(T // TB, N_EXPERTS, nf),
        in_specs=[pl.BlockSpec((TB, D_MODEL), lambda i, e, f: (i, 0), pipeline_mode=once),
                  pl.BlockSpec((TB, D_MODEL), lambda i, e, f: (i, 0), pipeline_mode=once),
                  pl.BlockSpec((TB, LANE), lambda i, e, f: (i, 0), pipeline_mode=once),
                  pl.BlockSpec((TB // SB, N_EXPERTS, SB), lambda i, e, f: (i, 0, 0)),
                  pl.BlockSpec((SB, SB), lambda i, e, f: (0, 0), pipeline_mode=once),
                  pl.BlockSpec((1, D_MODEL, tf), lambda i, e, f: (e, 0, f)),
                  pl.BlockSpec((1, D_MODEL, tf), lambda i, e, f: (e, 0, nf + f)),
                  pl.BlockSpec((1, tf, D_MODEL), lambda i, e, f: (e, f, 0))],
        out_specs=pl.BlockSpec((TB, D_MODEL), lambda i, e, f: (i, 0), pipeline_mode=once),
        out_shape=jax.ShapeDtypeStruct((T, D_MODEL), F32),
        scratch_shapes=[pltpu.VMEM((TB // SB, N_EXPERTS, SB), F32),
                        pltpu.VMEM((TB // SB * CH[-1], D_MODEL), BF16), pltpu.VMEM((TB // SB * CH[-1], D_MODEL), F32)],
        compiler_params=_cparams(("parallel", "arbitrary", "arbitrary")),
        name="moe_top2",
    )(h, x, comb, combt, tri, w_up, w_up, wd)


def _prep_w_in(w):
    wt = w.T
    hq, hf, hi, hg = (wt[j * HG_WIDTH:(j + 1) * HG_WIDTH] for j in range(4))
    z = wt[_OFF_SSM:_OFF_SSM + SSM_WIDTH]
    xbc = wt[_OFF_SSM + SSM_WIDTH:_OFF_SSM + SSM_WIDTH + SSM_CONV_DIM]
    dt = wt[_OFF_ATT - SSM_HEADS:_OFF_ATT]
    att = wt[_OFF_ATT:_OFF_GATE]
    gates = wt[_OFF_GATE:]
    w16 = jnp.concatenate([gates, xbc, hq, hi, hg, z, att], axis=0).astype(BF16)
    w32 = jnp.concatenate([hf, dt, jnp.zeros((LANE - SSM_HEADS, D_MODEL), w.dtype)], axis=0).astype(BF16)
    return w16, w32


def _pad_lanes(v, n, value=0.0):
    v = v.reshape(1, -1).astype(F32)
    return jnp.pad(v, ((0, 0), (0, n - v.shape[1])), constant_values=value)


def _run_group(x, B, L, C_h, C_s, valid, NB, tm, layer_params, states, kv_bufs):
    depth = len(layer_params)
    st_out, kv_out = None, (None if kv_bufs is None else [])
    for l, P in enumerate(layer_params):
        st = states[l]
        a16, a32 = _norm_proj(x, P["mix_norm"], P["w16"], P["w32"], tm)
        a16b, a32b = a16.reshape(B, L, W16), a32.reshape(B, L, W32)
        y_hg, s_hg, y_ssm, s_ssm, s_conv = _recurrent_mixers(
            a16b, a32b, P["lb"], P["hg_norm"], st["hgrn_t"], st["conv"], P["conv_w"], P["conv_b"], P["dt_bias"],
            P["a_log"], P["d_skip"], P["ssm_norm"], st["ssm"], B, L, C_h, HG_SUB, valid, NB, l, depth, st_out)
        st_out = (s_hg, s_ssm, s_conv)
        y_hg, y_ssm = y_hg.reshape(B * L, HG_WIDTH), y_ssm.reshape(B * L, SSM_WIDTH)
        if kv_bufs is None:
            y_att, *kv_out = _attn_prompt(a16, P["gq"], P["gk"], P["bd"], B, L, l, depth, kv_out)
        else:
            y_att, *kv = _attn_sample(a16, kv_bufs, l, P["gq"], P["gk"], P["bd"], B)
            kv_out.append(kv)
        x = _merge(y_hg, y_ssm, y_att, a16, x, P["w_out_hg"], P["w_out_ssm"], P["w_out_att"], P["w_o"], tm)
        if l % 2 == 0:
            F = P["ffn"]
            nxt = layer_params[l + 1]["moe"] if l + 1 < len(layer_params) else {}
            pending = [k for k in ("w_up", "wd") if k + "_f32" in nxt and k not in nxt]
            cast = [nxt[k + "_f32"].reshape(-1, nxt[k + "_f32"].shape[-1]) for k in pending]
            x, done = _ffn(x, P["ffn_norm"], F["w_up"], F["wd"], min(tm, 256), cast)
            for k, w in zip(pending, done):
                nxt[k] = w.reshape(nxt[k + "_f32"].shape)
        else:
            F = P["moe"]
            T = x.shape[0]
            h, comb, combt = _router(x, P["ffn_norm"], F["wr"], tm, min(MOE_SUB, T))
            x = _moe(x, h, comb, combt, F["w_up"], F["wd"], min(MOE_BLOCK, T), min(MOE_HALF, T), min(MOE_SUB, T),
                     MOE_SLOTS, 2)
    return x, st_out, kv_out


def kernel(x_prompt, x_sample, state_hgrn, state_ssm, state_conv, cache_kv_w128, cache_kv_w512, cache_kv_w2048, w_in, mix_norm, hgrn_lb_logits, hgrn_norm, ssm_conv_w, ssm_conv_b, ssm_dt_bias, ssm_a_log, ssm_d, ssm_norm, att_q_norm, att_k_norm, w_out_hgrn, w_out_ssm, w_out_att, w_o, ffn_norm, w_ffn_up, w_ffn_down, w_router, w_moe_up, w_moe_down):
    depth = w_in.shape[0]
    Bp, Lp, _ = x_prompt.shape
    Bs, Ls, _ = x_sample.shape
    R = SAMPLE_ROWS
    for buf, w in zip((cache_kv_w128, cache_kv_w512, cache_kv_w2048), ATT_WINDOWS):
        assert buf.shape[2] == w, "sample attention assumes full window buffers"

    lb_soft = jax.nn.softmax(hgrn_lb_logits.astype(F32), axis=0)
    lb_table = jnp.cumsum(lb_soft, axis=0) - lb_soft[0]
    bd = jnp.kron(jnp.eye(ATT_HEADS, dtype=F32), jnp.full((ATT_HEAD_DIM, ATT_HEAD_DIM), 1.0 / ATT_HEAD_DIM, F32)).astype(BF16)

    layer_params = []
    for l in range(depth):
        w16, w32 = _prep_w_in(w_in[l])
        P = dict(
            w16=w16, w32=w32,
            mix_norm=mix_norm[l].reshape(1, -1), ffn_norm=ffn_norm[l].reshape(1, -1),
            lb=lb_table[l].reshape(1, -1), hg_norm=hgrn_norm[l].reshape(1, -1),
            conv_w=jnp.pad(ssm_conv_w[l], ((0, 8 - SSM_CONV), (0, 0))), conv_b=ssm_conv_b[l].reshape(1, -1),
            dt_bias=_pad_lanes(ssm_dt_bias[l], LANE), a_log=_pad_lanes(ssm_a_log[l], LANE),
            d_skip=jnp.repeat(ssm_d[l].astype(F32), SSM_HEAD_DIM).reshape(1, -1),
            ssm_norm=ssm_norm[l].reshape(1, -1),
            gq=jnp.tile(att_q_norm[l], (1, ATT_HEADS)), gk=jnp.tile(att_k_norm[l], (1, ATT_HEADS)), bd=bd,
            w_out_hg=w_out_hgrn[l].astype(BF16), w_out_ssm=w_out_ssm[l].astype(BF16),
            w_out_att=w_out_att[l].astype(BF16), w_o=w_o[l].astype(BF16),
        )
        if l % 2 == 0:
            P["ffn"] = dict(w_up=w_ffn_up[l // 2].astype(BF16), wd=w_ffn_down[l // 2].astype(BF16))
        else:
            P["moe"] = dict(wr=jnp.pad(w_router[l // 2].astype(F32), ((0, 0), (0, LANE - N_EXPERTS))),
                            w_up_f32=w_moe_up[l // 2], wd_f32=w_moe_down[l // 2])
        layer_params.append(P)

    zero_p = dict(hgrn_t=jnp.zeros((Bp, HG_WIDTH, HG_DIM), F32), ssm=jnp.zeros((Bp, SSM_WIDTH, SSM_STATE), F32),
                  conv=jnp.zeros((Bp, 8, SSM_CONV_DIM), F32))
    xp, st_p, kv_p = _run_group(x_prompt.reshape(Bp * Lp, D_MODEL), Bp, Lp, 64, 64, 64, 4, 512,
                                layer_params, [zero_p] * depth, None)

    xs = jnp.pad(x_sample, ((0, 0), (0, R - Ls), (0, 0))).reshape(Bs * R, D_MODEL)
    st_s = []
    for l in range(depth):
        st_s.append(dict(
            hgrn_t=jnp.swapaxes(state_hgrn[l], -1, -2).reshape(Bs, HG_WIDTH, HG_DIM),
            ssm=state_ssm[l].reshape(Bs, SSM_WIDTH, SSM_STATE),
            conv=jnp.pad(state_conv[l], ((0, 0), (8 - (SSM_CONV - 1), 0), (0, 0)))))
    kv_s = [jnp.transpose(c, (0, 1, 3, 4, 5, 2)).reshape(depth, Bs, 2 * ATT_WIDTH, c.shape[2])
            for c in (cache_kv_w128, cache_kv_w512, cache_kv_w2048)]
    xs, st_s, kv_s = _run_group(xs, Bs, R, R, R, Ls, 4, Bs * R, layer_params, st_s, kv_s)

    def states(st, B):
        s_hg, s_ssm, s_conv = st
        return (jnp.swapaxes(s_hg.reshape(depth, B, HG_HEADS, HG_DIM, HG_DIM), -1, -2),
                s_ssm.reshape(depth, B, SSM_HEADS, SSM_HEAD_DIM, SSM_STATE),
                s_conv[:, :, 8 - (SSM_CONV - 1):])

    kv_prompt = [jnp.transpose(kv.reshape(depth, Bp, 2, ATT_HEADS, ATT_HEAD_DIM, kv.shape[3]), (0, 1, 5, 2, 3, 4))
                 for kv in kv_p]
    kv_sample = [jnp.stack([kv_s[l][j][:, :Ls].reshape(Bs, Ls, 2, ATT_HEADS, ATT_HEAD_DIM) for l in range(depth)])
                 for j in range(3)]
    y_prompt = xp.reshape(Bp, Lp, D_MODEL)
    y_sample = xs.reshape(Bs, R, D_MODEL)[:, :Ls]
    return (y_prompt, y_sample, *states(st_p, Bp), *kv_prompt, *states(st_s, Bs), *kv_sample)
```

```python
import functools

import numpy as np
import jax
import jax.numpy as jnp
from jax import lax
from jax.experimental import pallas as pl
from jax.experimental.pallas import tpu as pltpu

F32 = jnp.float32
BF16 = jnp.bfloat16

D_MODEL = 1024
HG_HEADS, HG_DIM = 4, 128
HG_WIDTH = HG_HEADS * HG_DIM
SSM_HEADS, SSM_HEAD_DIM, SSM_GROUPS, SSM_STATE, SSM_CONV = 8, 64, 2, 128, 4
SSM_WIDTH = SSM_HEADS * SSM_HEAD_DIM
SSM_CONV_DIM = SSM_WIDTH + 2 * SSM_GROUPS * SSM_STATE
ATT_WINDOWS = (128, 512, 2048)
ATT_DILATIONS = (1, 4, 16)
ATT_ORDER = (2, 1, 0)
ATT_HEADS, ATT_HEAD_DIM = 4, 64
ATT_WIDTH = ATT_HEADS * ATT_HEAD_DIM
ATT_KEYS = 128
ATT_SCALE = ATT_HEAD_DIM ** -0.5
N_EXPERTS = 8
EPS = 1e-6
NEG = -1e30

_OFF_SSM = 4 * HG_WIDTH
_OFF_ATT = _OFF_SSM + SSM_WIDTH + SSM_CONV_DIM + SSM_HEADS
_OFF_GATE = _OFF_ATT + 3 * 3 * ATT_WIDTH

A16_GATE, A16_XBC, A16_HQ, A16_HI, A16_HG, A16_Z, A16_ATT = 0, 3072, 4096, 4608, 5120, 5632, 6144
W16 = 8448
A32_HF, A32_DT = 0, 512
W32 = 640

LANE = 128
SAMPLE_ROWS = 16
HG_SUB = 8
MOE_BLOCK = 2048
MOE_HALF = 1024
MOE_SUB = 512
MOE_SLOTS = (144, 160)
VMEM_LIMIT = 56 * 2 ** 20


def _cparams(sem):
    return pltpu.CompilerParams(dimension_semantics=sem, vmem_limit_bytes=VMEM_LIMIT)


def _dot(a, b):
    return jnp.dot(a, b, preferred_element_type=F32)


def _dot_nt(a, b):
    return lax.dot_general(a, b, (((1,), (1,)), ((), ())), preferred_element_type=F32)


def _dot_tn(a, b):
    return lax.dot_general(a, b, (((0,), (0,)), ((), ())), preferred_element_type=F32)


def _split3(x):
    hi = x.astype(BF16)
    r = x - hi.astype(F32)
    mid = r.astype(BF16)
    lo = (r - mid.astype(F32)).astype(BF16)
    return hi, mid, lo


def _cumsum_rows(x, tril_bf16):
    hi, mid, lo = _split3(x)
    return _dot(tril_bf16, hi) + _dot(tril_bf16, mid) + _dot(tril_bf16, lo)


def _sigmoid(x):
    return 0.5 * jnp.tanh(0.5 * x) + 0.5


def _silu(x):
    return x * _sigmoid(x)


def _tril_bf16(n):
    r = lax.broadcasted_iota(jnp.int32, (n, n), 0)
    c = lax.broadcasted_iota(jnp.int32, (n, n), 1)
    return jnp.where(c <= r, 1.0, 0.0).astype(BF16)


def _norm_proj_kernel(x_ref, g_ref, w16_ref, w32_ref, o16_ref, o32_ref, *, chunk):
    x = x_ref[...]
    h = (x * lax.rsqrt(jnp.mean(x * x, axis=-1, keepdims=True) + EPS) * g_ref[...]).astype(BF16)
    for c0 in range(0, W16, chunk):
        o16_ref[:, c0:c0 + chunk] = _dot_nt(h, w16_ref[c0:c0 + chunk, :]).astype(BF16)
    o32_ref[...] = _dot_nt(h, w32_ref[...])


def _norm_proj(x, gain, w16, w32, tm):
    T = x.shape[0]
    const = lambda i: (0, 0)
    return pl.pallas_call(
        functools.partial(_norm_proj_kernel, chunk=768),
        grid=(pl.cdiv(T, tm),),
        in_specs=[
            pl.BlockSpec((tm, D_MODEL), lambda i: (i, 0)),
            pl.BlockSpec((1, D_MODEL), const),
            pl.BlockSpec((W16, D_MODEL), const, pipeline_mode=pl.Buffered(1)),
            pl.BlockSpec((W32, D_MODEL), const, pipeline_mode=pl.Buffered(1)),
        ],
        out_specs=[
            pl.BlockSpec((tm, W16), lambda i: (i, 0)),
            pl.BlockSpec((tm, W32), lambda i: (i, 0)),
        ],
        out_shape=[jax.ShapeDtypeStruct((T, W16), BF16), jax.ShapeDtypeStruct((T, W32), F32)],
        compiler_params=_cparams(("parallel",)),
        name="norm_proj",
    )(x, gain, w16, w32)


def _hgrn_tables(C, sub):
    t = np.arange(C)[:, None]
    u = np.arange(C)[None, :]
    blocks = [u <= t]
    levels = []
    l = sub
    while l < C:
        upper = (t % (2 * l)) >= l
        b = (t // (2 * l)) * (2 * l) + l - 1
        blocks.append(np.where(upper, (u > b) & (u <= t), (u > t) & (u <= b)))
        levels.append(l)
        l *= 2
    for s in range(sub):
        r = (t // sub) * sub + s
        blocks.append((u > r) & (u <= t))
    sel = [u == (t // sub) * sub + s for s in range(sub)]
    d = np.concatenate(blocks, 0).astype(np.float32)
    sl = np.concatenate(sel, 0).astype(np.float32)
    table = np.block([[d, d, d, np.zeros_like(d)], [np.zeros((sl.shape[0], 3 * C), np.float32), sl]])
    return jnp.asarray(table, BF16), tuple(levels)


def _hgrn_chunk(q_ref, f_ref, i_ref, g_ref, lb_ref, gn_ref, tab_ref, y_ref, st_scr, *, C, sub, levels, valid):
    lb = lb_ref[...]
    fgate = lb + (1.0 - lb) * jax.nn.sigmoid(f_ref[...])
    logf = jnp.log(fgate)
    kk = 1.0 - fgate
    rows = lax.broadcasted_iota(jnp.int32, (C, HG_WIDTH), 0)
    if valid < C:
        logf = jnp.where(rows < valid, logf, 0.0)
        kk = jnp.where(rows < valid, kk, 0.0)
    qa = _silu(q_ref[...].astype(F32))
    va = i_ref[...].astype(F32)
    ga = g_ref[...].astype(F32)

    ed = _dot(tab_ref[...], jnp.concatenate([*_split3(logf), kk.astype(BF16)], axis=0))
    nblk = 1 + len(levels) + sub
    blk = lambda j: ed[j * C:(j + 1) * C]
    G = blk(0)
    ksel = ed[nblk * C:]

    row_c = lax.broadcasted_iota(jnp.int32, (C, C), 0)
    lane_c = lax.broadcasted_iota(jnp.int32, (C, C), 1)
    off_c = lane_c - (row_c // sub) * sub

    A = [jnp.zeros((C, C), F32) for _ in range(HG_HEADS)]
    for s in range(sub):
        prod = qa * ksel[s * C:(s + 1) * C] * jnp.exp(blk(1 + len(levels) + s))
        for h in range(HG_HEADS):
            col = jnp.sum(prod[:, h * HG_DIM:(h + 1) * HG_DIM], axis=-1, keepdims=True)
            A[h] = jnp.where(off_c == s, col, A[h])
    A = [jnp.where(lane_c <= row_c, a, 0.0) for a in A]

    for li, l in enumerate(levels):
        upper = (rows % (2 * l)) >= l
        scaled = jnp.where(upper, qa, kk) * jnp.exp(blk(1 + li))
        qt = jnp.where(upper, scaled, 0.0).astype(BF16)
        kt = jnp.where(upper, 0.0, scaled).astype(BF16)
        same = (row_c // (2 * l)) == (lane_c // (2 * l))
        for h in range(HG_HEADS):
            sl = slice(h * HG_DIM, (h + 1) * HG_DIM)
            a_l = _dot_nt(qt[:, sl], kt[:, sl])
            A[h] = A[h] + (jnp.where(same, a_l, 0.0) if 2 * l < C else a_l)

    outs = []
    for h in range(HG_HEADS):
        sl = slice(h * HG_DIM, (h + 1) * HG_DIM)
        Gh, qh, kh = G[:, sl], qa[:, sl], kk[:, sl]
        vb = va[:, sl].astype(BF16)
        st = st_scr[sl, :]
        o = _dot(A[h].astype(BF16), vb) + _dot_nt((qh * jnp.exp(Gh)).astype(BF16), st.astype(BF16))
        g_last = Gh[C - 1:C]
        kdec = kh * jnp.exp(g_last - Gh)
        st_scr[sl, :] = jnp.exp(g_last) * st + _dot_tn(vb, kdec.astype(BF16))
        on = o * lax.rsqrt(jnp.mean(o * o, axis=-1, keepdims=True) + EPS) * gn_ref[...]
        outs.append(on * _silu(ga[:, sl]))
    y_ref[...] = jnp.concatenate(outs, axis=1).astype(BF16)


def _ssd_chunk(xbc_ref, z_ref, dt_ref, cw_ref, cb_ref, dtb_ref, alog_ref, dsk_ref, sn_ref,
               y_ref, ubuf, hs, *, C, valid):
    ubuf[8:8 + C, :] = xbc_ref[...].astype(F32)
    acc = cb_ref[...] + ubuf[5:5 + C, :] * cw_ref[0:1, :]
    for j in range(1, SSM_CONV):
        acc = acc + ubuf[5 + j:5 + j + C, :] * cw_ref[j:j + 1, :]
    xc = _silu(acc)
    ubuf[0:8, :] = ubuf[valid:valid + 8, :]

    x = dt_ref[...] + dtb_ref[...]
    dt = jnp.maximum(x, 0.0) + jnp.log1p(jnp.exp(-jnp.abs(x)))
    if valid < C:
        rows = lax.broadcasted_iota(jnp.int32, (C, LANE), 0)
        dt = jnp.where(rows < valid, dt, 0.0)
    a = -jnp.exp(alog_ref[...])
    cum = _cumsum_rows(dt * a, _tril_bf16(C))
    cum_t = cum.T
    dt_t = dt.T

    ri = lax.broadcasted_iota(jnp.int32, (C, C), 0)
    cj = lax.broadcasted_iota(jnp.int32, (C, C), 1)
    tril = cj <= ri
    lane = lax.broadcasted_iota(jnp.int32, (C, LANE), 1)
    srow = lax.broadcasted_iota(jnp.int32, (LANE, LANE), 0)
    bm = [xc[:, SSM_WIDTH + g * SSM_STATE:SSM_WIDTH + (g + 1) * SSM_STATE].astype(BF16) for g in range(SSM_GROUPS)]
    off_c = SSM_WIDTH + SSM_GROUPS * SSM_STATE
    cm = [xc[:, off_c + g * SSM_STATE:off_c + (g + 1) * SSM_STATE].astype(BF16) for g in range(SSM_GROUPS)]
    cb = [_dot_nt(cm[g], bm[g]) for g in range(SSM_GROUPS)]

    ys = []
    for p in range(SSM_HEADS // 2):
        g = (2 * p) // (SSM_HEADS // SSM_GROUPS)
        xp = xc[:, p * LANE:(p + 1) * LANE]
        yp = None
        colw = []
        for k, hd in enumerate((2 * p, 2 * p + 1)):
            colb = jnp.broadcast_to(cum[:, hd:hd + 1], (C, LANE))
            rowb = jnp.broadcast_to(cum_t[hd:hd + 1, :], (C, C))
            dtr = jnp.broadcast_to(dt_t[hd:hd + 1, :], (C, C))
            w = cb[g] * jnp.exp(jnp.where(tril, colb[:, :C] - rowb, NEG)) * dtr
            half = (lane < SSM_HEAD_DIM) if k == 0 else (lane >= SSM_HEAD_DIM)
            t = _dot(w.astype(BF16), jnp.where(half, xp, 0.0).astype(BF16))
            yp = t if yp is None else yp + t
            colw.append((colb, jnp.broadcast_to(dt[:, hd:hd + 1], (C, LANE))))
        first = lane < SSM_HEAD_DIM
        cum_e = jnp.where(first, colw[0][0], colw[1][0])
        dt_e = jnp.where(first, colw[0][1], colw[1][1])
        hp = hs[p * LANE:(p + 1) * LANE, :]
        yp = yp + jnp.exp(cum_e) * _dot_nt(cm[g], hp.astype(BF16))
        c_last = cum_e[C - 1:C, :]
        w_in = jnp.exp(c_last - cum_e) * dt_e
        dec = jnp.where(srow < SSM_HEAD_DIM,
                        jnp.broadcast_to(cum[C - 1:C, 2 * p:2 * p + 1], (LANE, LANE)),
                        jnp.broadcast_to(cum[C - 1:C, 2 * p + 1:2 * p + 2], (LANE, LANE)))
        hs[p * LANE:(p + 1) * LANE, :] = jnp.exp(dec) * hp + _dot_tn((xp * w_in).astype(BF16), bm[g])
        ys.append(yp)
    y = jnp.concatenate(ys, axis=1) + dsk_ref[...] * xc[:, :SSM_WIDTH]
    yz = y * _silu(z_ref[...].astype(F32))
    y_ref[...] = (yz * lax.rsqrt(jnp.mean(yz * yz, axis=-1, keepdims=True) + EPS) * sn_ref[...]).astype(BF16)


def _rec_kernel(q_ref, f_ref, i_ref, g_ref, lb_ref, gn_ref, s0_ref, tab_ref,
                xbc_ref, z_ref, dt_ref, pre_ref, cw_ref, cb_ref, dtb_ref, alog_ref, dsk_ref, sn_ref, h0_ref,
                *rest, NB, nchunks, first, hg_kw, ssd_kw):
    yh_ref, sout_ref, ys_ref, hout_ref, cout_ref, st_scr, ubuf, hs = rest[-8:]
    ci = pl.program_id(1)

    @pl.when(ci == 0)
    def _():
        st_scr[...] = s0_ref[...]
        ubuf[:, 0:8, :] = pre_ref[...]
        hs[...] = h0_ref[...]

    for j in range(NB):
        _hgrn_chunk(q_ref.at[j], f_ref.at[j], i_ref.at[j], g_ref.at[j], lb_ref, gn_ref, tab_ref,
                    yh_ref.at[j], st_scr.at[j], **hg_kw)
        _ssd_chunk(xbc_ref.at[j], z_ref.at[j], dt_ref.at[j], cw_ref, cb_ref, dtb_ref, alog_ref, dsk_ref, sn_ref,
                   ys_ref.at[j], ubuf.at[j], hs.at[j], **ssd_kw)

    @pl.when(ci == nchunks - 1)
    def _():
        for d in range(sout_ref.shape[0]):
            sout_ref[d] = st_scr[...]
            hout_ref[d] = hs[...]
            cout_ref[d] = ubuf[:, 0:8, :]


def _recurrent_mixers(a16, a32, lb, gn, s0t, prefix, cw, cb, dtb, alog, dsk, sn, h0, B, L, C, sub, valid, NB,
                      layer, depth, prev):
    nch = L // C
    table, levels = _hgrn_tables(C, sub)
    const = lambda b, c: (0, 0)
    bat = lambda b, c: (b, 0, 0)
    blk = lambda w, col: pl.BlockSpec((NB, C, w), lambda b, c: (b, c, col))
    first = prev is None
    slot = (lambda rows, w: pl.BlockSpec((depth, NB, rows, w), lambda b, c: (0, b, 0, 0))) if first else \
           (lambda rows, w: pl.BlockSpec((1, NB, rows, w), lambda b, c: (layer, b, 0, 0)))
    return pl.pallas_call(
        functools.partial(_rec_kernel, NB=NB, nchunks=nch, first=first,
                          hg_kw=dict(C=C, sub=sub, levels=levels, valid=valid), ssd_kw=dict(C=C, valid=valid)),
        grid=(B // NB, nch),
        in_specs=[
            blk(HG_WIDTH, A16_HQ // HG_WIDTH), blk(HG_WIDTH, A32_HF // HG_WIDTH),
            blk(HG_WIDTH, A16_HI // HG_WIDTH), blk(HG_WIDTH, A16_HG // HG_WIDTH),
            pl.BlockSpec((1, HG_WIDTH), const),
            pl.BlockSpec((1, HG_DIM), const),
            pl.BlockSpec((NB, HG_WIDTH, HG_DIM), bat),
            pl.BlockSpec(table.shape, const),
            blk(SSM_CONV_DIM, A16_XBC // SSM_CONV_DIM), blk(SSM_WIDTH, A16_Z // SSM_WIDTH), blk(LANE, A32_DT // LANE),
            pl.BlockSpec((NB, 8, SSM_CONV_DIM), bat),
            pl.BlockSpec((8, SSM_CONV_DIM), const),
            pl.BlockSpec((1, SSM_CONV_DIM), const),
            pl.BlockSpec((1, LANE), const),
            pl.BlockSpec((1, LANE), const),
            pl.BlockSpec((1, SSM_WIDTH), const),
            pl.BlockSpec((1, SSM_WIDTH), const),
            pl.BlockSpec((NB, SSM_WIDTH, SSM_STATE), bat),
        ] + ([] if first else [pl.BlockSpec(memory_space=pl.ANY)] * 3),
        out_specs=[
            blk(HG_WIDTH, 0), slot(HG_WIDTH, HG_DIM),
            blk(SSM_WIDTH, 0), slot(SSM_WIDTH, SSM_STATE), slot(8, SSM_CONV_DIM),
        ],
        out_shape=[jax.ShapeDtypeStruct((B, L, HG_WIDTH), BF16),
                   jax.ShapeDtypeStruct((depth, B, HG_WIDTH, HG_DIM), F32),
                   jax.ShapeDtypeStruct((B, L, SSM_WIDTH), BF16),
                   jax.ShapeDtypeStruct((depth, B, SSM_WIDTH, SSM_STATE), F32),
                   jax.ShapeDtypeStruct((depth, B, 8, SSM_CONV_DIM), F32)],
        input_output_aliases={} if first else {19: 1, 20: 3, 21: 4},
        scratch_shapes=[pltpu.VMEM((NB, HG_WIDTH, HG_DIM), F32), pltpu.VMEM((NB, C + 8, SSM_CONV_DIM), F32),
                        pltpu.VMEM((NB, SSM_WIDTH, SSM_STATE), F32)],
        compiler_params=_cparams(("parallel", "arbitrary")),
        name="hgrn2_ssd",
    )(a16, a32, a16, a16, lb, gn, s0t, table, a16, a16, a32, prefix, cw, cb, dtb, alog, dsk, sn, h0,
      *([] if first else prev))


def _head_norm(x, gain, bd):
    ms = _dot((x * x).astype(BF16), bd)
    return x * (lax.rsqrt(ms + EPS) * gain)


def _head_masks(rows):
    lane = lax.broadcasted_iota(jnp.int32, (rows, ATT_WIDTH), 1)
    return [(lane >= h * ATT_HEAD_DIM) & (lane < (h + 1) * ATT_HEAD_DIM) for h in range(ATT_HEADS)]


def _stack_heads(q, masks):
    return jnp.concatenate([jnp.where(m, q, 0.0) for m in masks], axis=0).astype(BF16)


def _unstack_heads(pv, m, d, masks):
    R = pv.shape[0] // ATT_HEADS
    acc, mx, den = pv[0:R], jnp.broadcast_to(m[0:R], (R, ATT_WIDTH)), jnp.broadcast_to(d[0:R], (R, ATT_WIDTH))
    for h in range(1, ATT_HEADS):
        sl = slice(h * R, (h + 1) * R)
        acc = jnp.where(masks[h], pv[sl], acc)
        mx = jnp.where(masks[h], m[sl], mx)
        den = jnp.where(masks[h], d[sl], den)
    return acc, mx, den


def _attend(qb, kb, vb, bias4, masks):
    s = _dot_nt(_stack_heads(qb, masks), kb) + bias4
    m = jnp.max(s, axis=-1, keepdims=True)
    p = jnp.exp(s - m)
    d = jnp.sum(p, axis=-1, keepdims=True)
    return _unstack_heads(_dot(p.astype(BF16), vb), m, d, masks)


def _ld2(pair, s):
    return jnp.concatenate([pair[0][s, :], pair[1][s, :]], axis=1)


def _st2(pair, s, val):
    pair[0][s, :] = val[:, :LANE]
    pair[1][s, :] = val[:, LANE:]


def _attn_prompt_kernel(att_ref, gq_ref, gk_ref, bd_ref, *rest, L):
    y_ref, kv0_ref, kv1_ref, kv2_ref = rest[-16:-12]
    scr = rest[-12:]
    qn, kn, vv, ya, ma, za = (scr[2 * j:2 * j + 2] for j in range(6))
    g = pl.program_id(1)
    RB = 256
    QB = ATT_KEYS

    def group(step, gi, dil, keep, kv_ref):
        M = L // dil
        nb = M // QB
        KW = min(2 * QB, M)
        bd = bd_ref[...]

        def norm_body(i, carry):
            r0 = pl.multiple_of(i * RB, RB)
            x = att_ref[pl.ds(r0, RB), :].astype(F32)
            _st2(qn, pl.ds(r0, RB), _head_norm(x[:, :ATT_WIDTH], gq_ref[gi:gi + 1, :] * ATT_SCALE, bd))
            _st2(kn, pl.ds(r0, RB), _head_norm(x[:, ATT_WIDTH:2 * ATT_WIDTH], gk_ref[gi:gi + 1, :], bd))
            _st2(vv, pl.ds(r0, RB), x[:, 2 * ATT_WIDTH:])
            return carry

        lax.fori_loop(0, L // RB, norm_body, 0)
        cw = min(keep, RB)
        for c0 in range(0, keep, cw):
            rows = slice(L - keep + c0, L - keep + c0 + cw)
            for p, half in enumerate((kn[0], kn[1], vv[0], vv[1])):
                piece = half[rows, :].T
                for d in range(kv_ref.shape[0]):
                    kv_ref[d, 0, p * LANE:(p + 1) * LANE, c0:c0 + cw] = piece

        masks = _head_masks(QB)
        iq = lax.broadcasted_iota(jnp.int32, (ATT_HEADS * QB, KW), 0) & (QB - 1)
        ik = lax.broadcasted_iota(jnp.int32, (ATT_HEADS * QB, KW), 1)

        def run_blocks(first, count, lead):
            dist = (iq - ik) if lead else (QB + iq - ik)
            bias4 = jnp.where((dist >= 0) & (dist <= ATT_KEYS), 0.0, NEG)
            unroll = max(u for u in (4, 3, 2, 1) if count % u == 0)

            def blk_body(j, carry):
                loaded = []
                for u in range(unroll):
                    it = first + j * unroll + u
                    n = it // dil
                    r = it - n * dil
                    k0 = 0 if lead else (n - 1) * QB
                    if dil == 1:
                        qs = pl.ds(pl.multiple_of(n * QB, QB), QB)
                        ks = pl.ds(pl.multiple_of(k0, QB), KW)
                    else:
                        qs = pl.ds(n * QB * dil + r, QB, stride=dil)
                        ks = pl.ds(k0 * dil + r, KW, stride=dil)
                    old = None if step == 0 else (_ld2(ya, qs), _ld2(ma, qs), _ld2(za, qs))
                    loaded.append((qs, _ld2(qn, qs), _ld2(kn, ks).astype(BF16), _ld2(vv, ks).astype(BF16), old))
                for qs, qb, kb, vb, old in loaded:
                    acc, mx, den = _attend(qb, kb, vb, bias4, masks)
                    if step > 0:
                        yo, mo, zo = old
                        mn = jnp.maximum(mo, mx)
                        eo = jnp.exp(mo - mn)
                        en = jnp.exp(mx - mn)
                        acc, mx, den = yo * eo + acc * en, mn, zo * eo + den * en
                    _st2(ya, qs, acc)
                    _st2(ma, qs, mx)
                    _st2(za, qs, den)
                return carry

            lax.fori_loop(0, count // unroll, blk_body, 0)

        run_blocks(0, dil, True)
        if nb > 1:
            run_blocks(dil, dil * (nb - 1), False)

    for step, gi in enumerate(ATT_ORDER):
        @pl.when(g == step)
        def _(step=step, gi=gi):
            group(step, gi, ATT_DILATIONS[gi], min(ATT_WINDOWS[gi], L), (kv0_ref, kv1_ref, kv2_ref)[gi])

    @pl.when(g == 2)
    def _():
        def out_body(i, carry):
            r0 = pl.multiple_of(i * RB, RB)
            y_ref[pl.ds(r0, RB), :] = (_ld2(ya, pl.ds(r0, RB)) / _ld2(za, pl.ds(r0, RB))).astype(BF16)
            return carry
        lax.fori_loop(0, L // RB, out_body, 0)


def _attn_prompt(a16, gq, gk, bd, B, L, layer, depth, prev):
    keeps = [min(w, L) for w in ATT_WINDOWS]
    const = lambda b, g: (0, 0)
    first = prev is None
    slot = (lambda k: pl.BlockSpec((depth, 1, 2 * ATT_WIDTH, k), lambda b, g: (0, b, 0, 0))) if first else \
           (lambda k: pl.BlockSpec((1, 1, 2 * ATT_WIDTH, k), lambda b, g: (layer, b, 0, 0)))
    return pl.pallas_call(
        functools.partial(_attn_prompt_kernel, L=L),
        grid=(B, 3),
        in_specs=[
            pl.BlockSpec((L, 3 * ATT_WIDTH), lambda b, g: (b, A16_ATT // (3 * ATT_WIDTH) + 2 - g)),
            pl.BlockSpec((3, ATT_WIDTH), const),
            pl.BlockSpec((3, ATT_WIDTH), const),
            pl.BlockSpec((ATT_WIDTH, ATT_WIDTH), const),
        ] + ([] if first else [pl.BlockSpec(memory_space=pl.ANY)] * 3),
        out_specs=[pl.BlockSpec((L, ATT_WIDTH), lambda b, g: (b, 0))] + [slot(k) for k in keeps],
        out_shape=[jax.ShapeDtypeStruct((B * L, ATT_WIDTH), BF16)]
        + [jax.ShapeDtypeStruct((depth, B, 2 * ATT_WIDTH, k), F32) for k in keeps],
        input_output_aliases={} if first else {4: 1, 5: 2, 6: 3},
        scratch_shapes=[pltpu.VMEM((L, LANE), F32) for _ in range(12)],
        compiler_params=_cparams(("parallel", "arbitrary")),
        name="attn_prompt",
    )(a16, gq, gk, bd, *([] if first else prev))


def _attn_sample_kernel(a0_ref, a1_ref, a2_ref, b0_ref, b1_ref, b2_ref, gq_ref, gk_ref, bd_ref,
                        y_ref, k0_ref, k1_ref, k2_ref):
    R = SAMPLE_ROWS
    bd = bd_ref[...]
    masks = _head_masks(R)
    ya = ma = za = None
    for gi, (a_ref, b_ref, k_ref) in enumerate(((a0_ref, b0_ref, k0_ref), (a1_ref, b1_ref, k1_ref),
                                                (a2_ref, b2_ref, k2_ref))):
        dil = ATT_DILATIONS[gi]
        wb = b_ref.shape[3]
        x = a_ref[...].astype(F32)
        q = _head_norm(x[:, :ATT_WIDTH], gq_ref[gi:gi + 1, :] * ATT_SCALE, bd)
        k = _head_norm(x[:, ATT_WIDTH:2 * ATT_WIDTH], gk_ref[gi:gi + 1, :], bd)
        v = x[:, 2 * ATT_WIDTH:]
        k_ref[0] = jnp.concatenate([k[:8], v[:8]], axis=1)
        k_t = b_ref[0, 0, 0:ATT_WIDTH, :].astype(BF16)
        v_t = b_ref[0, 0, ATT_WIDTH:2 * ATT_WIDTH, :].astype(BF16)
        kb, vb = k.astype(BF16), v.astype(BF16)

        def bias(dist):
            return jnp.where((dist >= 0) & (dist <= ATT_KEYS * dil) & ((dist & (dil - 1)) == 0), 0.0, NEG)

        R4 = ATT_HEADS * R
        row_b = lax.broadcasted_iota(jnp.int32, (R4, wb), 0) & (R - 1)
        row_n = lax.broadcasted_iota(jnp.int32, (R4, R), 0) & (R - 1)
        qs = _stack_heads(q, masks)
        s_b = _dot(qs, k_t) + bias(wb + row_b - lax.broadcasted_iota(jnp.int32, (R4, wb), 1))
        s_n = _dot_nt(qs, kb) + bias(row_n - lax.broadcasted_iota(jnp.int32, (R4, R), 1))
        m = jnp.maximum(jnp.max(s_b, axis=-1, keepdims=True), jnp.max(s_n, axis=-1, keepdims=True))
        p_b = jnp.exp(s_b - m)
        p_n = jnp.exp(s_n - m)
        d = jnp.sum(p_b, axis=-1, keepdims=True) + jnp.sum(p_n, axis=-1, keepdims=True)
        pv = _dot_nt(p_b.astype(BF16), v_t) + _dot(p_n.astype(BF16), vb)
        acc, mx, den = _unstack_heads(pv, m, d, masks)
        if gi == 0:
            ya, ma, za = acc, mx, den
        else:
            mn = jnp.maximum(ma, mx)
            eo = jnp.exp(ma - mn)
            en = jnp.exp(mx - mn)
            ya, za, ma = ya * eo + acc * en, za * eo + den * en, mn
    y_ref[...] = (ya / za).astype(BF16)


def _attn_sample(a16, bufs, layer, gq, gk, bd, B):
    R = SAMPLE_ROWS
    const = lambda b: (0, 0)
    bat = lambda b: (b, 0, 0)
    base = A16_ATT // (3 * ATT_WIDTH)
    return pl.pallas_call(
        _attn_sample_kernel,
        grid=(B,),
        in_specs=[pl.BlockSpec((R, 3 * ATT_WIDTH), functools.partial(lambda b, j: (b, j), j=base + g))
                  for g in range(3)]
        + [pl.BlockSpec((1, 1, 2 * ATT_WIDTH, buf.shape[3]), lambda b: (layer, b, 0, 0)) for buf in bufs]
        + [pl.BlockSpec((3, ATT_WIDTH), const), pl.BlockSpec((3, ATT_WIDTH), const),
           pl.BlockSpec((ATT_WIDTH, ATT_WIDTH), const)],
        out_specs=[pl.BlockSpec((R, ATT_WIDTH), lambda b: (b, 0))]
        + [pl.BlockSpec((1, 8, 2 * ATT_WIDTH), bat) for _ in range(3)],
        out_shape=[jax.ShapeDtypeStruct((B * R, ATT_WIDTH), BF16)]
        + [jax.ShapeDtypeStruct((B, 8, 2 * ATT_WIDTH), F32) for _ in range(3)],
        compiler_params=_cparams(("parallel",)),
        name="attn_sample",
    )(a16, a16, a16, *bufs, gq, gk, bd)


def _merge_kernel(yh_ref, ys_ref, ya_ref, gt_ref, x_ref, wh_ref, ws_ref, wa_ref, wo_ref, o_ref):
    gt = _sigmoid(gt_ref[...].astype(F32))
    mixed = (gt[:, 0:D_MODEL] * _dot(yh_ref[...], wh_ref[...])
             + gt[:, D_MODEL:2 * D_MODEL] * _dot(ys_ref[...], ws_ref[...])
             + gt[:, 2 * D_MODEL:] * _dot(ya_ref[...], wa_ref[...]))
    o_ref[...] = x_ref[...] + _dot(mixed.astype(BF16), wo_ref[...])


def _merge(y_hg, y_ssm, y_att, a16, x, wh, ws, wa, wo, tm):
    T = x.shape[0]
    const = lambda i: (0, 0)
    rowb = lambda w: pl.BlockSpec((tm, w), lambda i: (i, 0))
    wsp = lambda w: pl.BlockSpec(w.shape, const, pipeline_mode=pl.Buffered(1))
    return pl.pallas_call(
        _merge_kernel,
        grid=(pl.cdiv(T, tm),),
        in_specs=[rowb(HG_WIDTH), rowb(SSM_WIDTH), rowb(ATT_WIDTH),
                  pl.BlockSpec((tm, 3 * D_MODEL), lambda i: (i, A16_GATE // (3 * D_MODEL))),
                  rowb(D_MODEL), wsp(wh), wsp(ws), wsp(wa), wsp(wo)],
        out_specs=rowb(D_MODEL),
        out_shape=jax.ShapeDtypeStruct((T, D_MODEL), F32),
        compiler_params=_cparams(("parallel",)),
        name="merge_out_proj",
    )(y_hg, y_ssm, y_att, a16, x, wh, ws, wa, wo)


def _ffn_kernel(x_ref, g_ref, wg_ref, wu_ref, wd_ref, *rest):
    n = (len(rest) - 1) // 2
    cast_in, o_ref, cast_out = rest[:n], rest[n], rest[n + 1:]
    x = x_ref[...]
    h = (x * lax.rsqrt(jnp.mean(x * x, axis=-1, keepdims=True) + EPS) * g_ref[...]).astype(BF16)
    a = _silu(_dot(h, wg_ref[...])) * _dot(h, wu_ref[...])
    o_ref[...] = x + _dot(a.astype(BF16), wd_ref[...])
    for src, dst in zip(cast_in, cast_out):
        dst[...] = src[...].astype(BF16)


def _ffn(x, gain, w_up, wd, tm, cast=()):
    T = x.shape[0]
    dff = wd.shape[0]
    steps = pl.cdiv(T, tm)
    const = lambda i: (0, 0)
    wsp = lambda w: pl.BlockSpec(w.shape, const, pipeline_mode=pl.Buffered(1))
    slab = [pl.BlockSpec((w.shape[0] // steps, w.shape[1]), lambda i: (i, 0)) for w in cast]
    for w in cast:
        assert w.shape[0] % (16 * steps) == 0
    outs = pl.pallas_call(
        _ffn_kernel,
        grid=(steps,),
        in_specs=[pl.BlockSpec((tm, D_MODEL), lambda i: (i, 0)), pl.BlockSpec((1, D_MODEL), const),
                  pl.BlockSpec((D_MODEL, dff), const, pipeline_mode=pl.Buffered(1)),
                  pl.BlockSpec((D_MODEL, dff), lambda i: (0, 1), pipeline_mode=pl.Buffered(1)),
                  wsp(wd)] + slab,
        out_specs=[pl.BlockSpec((tm, D_MODEL), lambda i: (i, 0))] + slab,
        out_shape=[jax.ShapeDtypeStruct((T, D_MODEL), F32)] + [jax.ShapeDtypeStruct(w.shape, BF16) for w in cast],
        compiler_params=_cparams(("parallel",)),
        name="ffn_dense",
    )(x, gain, w_up, w_up, wd, *cast)
    return outs[0], outs[1:]


def _router_kernel(x_ref, g_ref, wr_ref, h_ref, comb_ref, combt_ref):
    x = x_ref[...]
    h = x * lax.rsqrt(jnp.mean(x * x, axis=-1, keepdims=True) + EPS) * g_ref[...]
    hb = h.astype(BF16)
    h_ref[...] = hb
    h_lo = (h - hb.astype(F32)).astype(BF16)
    w = wr_ref[...]
    w_hi = w.astype(BF16)
    w_lo = (w - w_hi.astype(F32)).astype(BF16)
    logits = _dot(hb, w_hi) + _dot(hb, w_lo) + _dot(h_lo, w_hi)
    lane = lax.broadcasted_iota(jnp.int32, logits.shape, 1)
    logits = jnp.where(lane < N_EXPERTS, logits, NEG)
    v1 = jnp.max(logits, axis=-1, keepdims=True)
    i1 = jnp.min(jnp.where(logits == v1, lane, LANE), axis=-1, keepdims=True)
    rest = jnp.where(lane == i1, NEG, logits)
    v2 = jnp.max(rest, axis=-1, keepdims=True)
    i2 = jnp.min(jnp.where(rest == v2, lane, LANE), axis=-1, keepdims=True)
    e = jnp.exp(v2 - v1)
    w1 = 1.0 / (1.0 + e)
    w2 = e / (1.0 + e)
    comb = jnp.where(lane == i1, w1, 0.0) + jnp.where(lane == i2, w2, 0.0)
    comb_ref[...] = comb
    sb = combt_ref.shape[2]
    comb_t = comb.T[0:N_EXPERTS, :]
    for j in range(combt_ref.shape[0]):
        combt_ref[j] = comb_t[:, j * sb:(j + 1) * sb]


def _router(x, gain, wr, tm, SB):
    T = x.shape[0]
    assert tm % SB == 0
    const = lambda i: (0, 0)
    return pl.pallas_call(
        _router_kernel,
        grid=(pl.cdiv(T, tm),),
        in_specs=[pl.BlockSpec((tm, D_MODEL), lambda i: (i, 0)), pl.BlockSpec((1, D_MODEL), const),
                  pl.BlockSpec((D_MODEL, LANE), const)],
        out_specs=[pl.BlockSpec((tm, D_MODEL), lambda i: (i, 0)), pl.BlockSpec((tm, LANE), lambda i: (i, 0)),
                   pl.BlockSpec((tm // SB, N_EXPERTS, SB), lambda i: (i, 0, 0))],
        out_shape=[jax.ShapeDtypeStruct((T, D_MODEL), BF16), jax.ShapeDtypeStruct((T, LANE), F32),
                   jax.ShapeDtypeStruct((T // SB, N_EXPERTS, SB), F32)],
        compiler_params=_cparams(("parallel",)),
        name="moe_router",
    )(x, gain, wr)


def _moe_kernel(count_ref, h_ref, x_ref, comb_ref, combt_ref, tri_ref, wg_ref, wu_ref, wd_ref, o_ref,
                rank_scr, hs_scr, ys_scr, *, HB, SB, CH, nf):
    e = pl.program_id(1)
    f = pl.program_id(2)
    TB = h_ref.shape[0]
    nsub = HB // SB

    @pl.when((e == 0) & (f == 0))
    def _():
        o_ref[...] = x_ref[...]
        for j in range(TB // SB):
            rank_scr[j] = _dot(jnp.where(combt_ref[j] > 0.0, 1.0, 0.0).astype(BF16), tri_ref[...])

    def expert(hs):
        a = _silu(_dot(hs, wg_ref[0])) * _dot(hs, wu_ref[0])
        return _dot(a.astype(BF16), wd_ref[0])

    def half_block(hb, carry):
        subs = [pl.ds(pl.multiple_of(hb * HB + j * SB, SB), SB) for j in range(nsub)]
        routed = [combt_ref[hb * nsub + j, pl.ds(e, 1), :] > 0.0 for j in range(nsub)]
        rank = [rank_scr[hb * nsub + j, pl.ds(e, 1), :] for j in range(nsub)]
        count = count_ref[pl.program_id(0), hb * N_EXPERTS + e]

        def onehots(base, ch):
            slot = lax.broadcasted_iota(jnp.int32, (ch, SB), 0).astype(F32)
            return [jnp.where((rk - base == slot) & rt, 1.0, 0.0).astype(BF16) for rk, rt in zip(rank, routed)]

        def gather(ps):
            return jnp.concatenate([_dot(p, h_ref[sb, :]) for p, sb in zip(ps, subs)], axis=0).astype(BF16)

        def scatter(ps, ys, ch):
            lane = lax.broadcasted_iota(jnp.int32, (HB, LANE), 1)
            comb = comb_ref[pl.ds(pl.multiple_of(hb * HB, HB), HB), :]
            gate = jnp.sum(jnp.where(lane == e, comb, 0.0), axis=-1, keepdims=True)
            for j, (p, sb) in enumerate(zip(ps, subs)):
                o_ref[sb, :] += gate[j * SB:(j + 1) * SB] * _dot_tn(p, ys[j * ch:(j + 1) * ch])

        def single_pass(ch):
            ps = onehots(0.0, ch)
            rows = pl.ds(pl.multiple_of(hb * (nsub * CH[-1]), 16), nsub * ch)

            @pl.when(f == 0)
            def _():
                hs = gather(ps)
                hs_scr[rows, :] = hs
                ys_scr[rows, :] = expert(hs)

            if nf > 2:
                @pl.when((f > 0) & (f < nf - 1))
                def _():
                    ys_scr[rows, :] += expert(hs_scr[rows, :])

            @pl.when(f == nf - 1)
            def _():
                scatter(ps, (ys_scr[rows, :] + expert(hs_scr[rows, :])).astype(BF16), ch)

        def overflow_pass(c, carry):
            ch = CH[-1]
            ps = onehots((c * ch).astype(F32), ch)
            scatter(ps, expert(gather(ps)).astype(BF16), ch)
            return carry

        lo = 0
        for ch in CH:
            @pl.when((count > lo) & (count <= ch))
            def _(ch=ch):
                single_pass(ch)
            lo = ch

        @pl.when(count > lo)
        def _():
            lax.fori_loop(0, (count + lo - 1) // lo, overflow_pass, 0)

        return carry

    lax.fori_loop(0, TB // HB, half_block, 0)


def _moe(x, h, comb, combt, w_up, wd, TB, HB, SB, CH, nf):
    assert nf >= 2, "the first and last d_ff positions are separate code paths"
    T = x.shape[0]
    dff = wd.shape[1]
    tf = dff // nf
    tri = jnp.triu(jnp.ones((SB, SB), BF16), k=1)
    counts = jnp.sum(combt > 0.0, axis=-1, dtype=jnp.int32).reshape(T // TB, TB // HB, HB // SB, N_EXPERTS)
    counts = jnp.max(counts, axis=2).reshape(T // TB, TB // HB * N_EXPERTS)
    once = pl.Buffered(1)
    return pl.pallas_call(
        functools.partial(_moe_kernel, HB=HB, SB=SB, CH=CH, nf=nf),
        grid_spec=pltpu.PrefetchScalarGridSpec(
            num_scalar_prefetch=1,
            grid=(T // TB, N_EXPERTS, nf),
            in_specs=[pl.BlockSpec((TB, D_MODEL), lambda i, e, f, c: (i, 0), pipeline_mode=once),
                      pl.BlockSpec((TB, D_MODEL), lambda i, e, f, c: (i, 0), pipeline_mode=once),
                      pl.BlockSpec((TB, LANE), lambda i, e, f, c: (i, 0), pipeline_mode=once),
                      pl.BlockSpec((TB // SB, N_EXPERTS, SB), lambda i, e, f, c: (i, 0, 0)),
                      pl.BlockSpec((SB, SB), lambda i, e, f, c: (0, 0), pipeline_mode=once),
                      pl.BlockSpec((1, D_MODEL, tf), lambda i, e, f, c: (e, 0, f)),
                      pl.BlockSpec((1, D_MODEL, tf), lambda i, e, f, c: (e, 0, nf + f)),
                      pl.BlockSpec((1, tf, D_MODEL), lambda i, e, f, c: (e, f, 0))],
            out_specs=pl.BlockSpec((TB, D_MODEL), lambda i, e, f, c: (i, 0), pipeline_mode=once),
            scratch_shapes=[pltpu.VMEM((TB // SB, N_EXPERTS, SB), F32),
                            pltpu.VMEM((TB // SB * CH[-1], D_MODEL), BF16),
                            pltpu.VMEM((TB // SB * CH[-1], D_MODEL), F32)]),
        out_shape=jax.ShapeDtypeStruct((T, D_MODEL), F32),
        compiler_params=_cparams(("parallel", "arbitrary", "arbitrary")),
        name="moe_top2",
    )(counts, h, x, comb, combt, tri, w_up, w_up, wd)


def _prep_w_in(w):
    wt = w.T
    hq, hf, hi, hg = (wt[j * HG_WIDTH:(j + 1) * HG_WIDTH] for j in range(4))
    z = wt[_OFF_SSM:_OFF_SSM + SSM_WIDTH]
    xbc = wt[_OFF_SSM + SSM_WIDTH:_OFF_SSM + SSM_WIDTH + SSM_CONV_DIM]
    dt = wt[_OFF_ATT - SSM_HEADS:_OFF_ATT]
    att = wt[_OFF_ATT:_OFF_GATE]
    gates = wt[_OFF_GATE:]
    w16 = jnp.concatenate([gates, xbc, hq, hi, hg, z, att], axis=0).astype(BF16)
    w32 = jnp.concatenate([hf, dt, jnp.zeros((LANE - SSM_HEADS, D_MODEL), w.dtype)], axis=0).astype(BF16)
    return w16, w32


def _pad_lanes(v, n, value=0.0):
    v = v.reshape(1, -1).astype(F32)
    return jnp.pad(v, ((0, 0), (0, n - v.shape[1])), constant_values=value)


def _run_group(x, B, L, C_h, C_s, valid, NB, tm, layer_params, states, kv_bufs):
    depth = len(layer_params)
    st_out, kv_out = None, (None if kv_bufs is None else [])
    for l, P in enumerate(layer_params):
        st = states[l]
        a16, a32 = _norm_proj(x, P["mix_norm"], P["w16"], P["w32"], tm)
        a16b, a32b = a16.reshape(B, L, W16), a32.reshape(B, L, W32)
        y_hg, s_hg, y_ssm, s_ssm, s_conv = _recurrent_mixers(
            a16b, a32b, P["lb"], P["hg_norm"], st["hgrn_t"], st["conv"], P["conv_w"], P["conv_b"], P["dt_bias"],
            P["a_log"], P["d_skip"], P["ssm_norm"], st["ssm"], B, L, C_h, HG_SUB, valid, NB, l, depth, st_out)
        st_out = (s_hg, s_ssm, s_conv)
        y_hg, y_ssm = y_hg.reshape(B * L, HG_WIDTH), y_ssm.reshape(B * L, SSM_WIDTH)
        if kv_bufs is None:
            y_att, *kv_out = _attn_prompt(a16, P["gq"], P["gk"], P["bd"], B, L, l, depth, kv_out)
        else:
            y_att, *kv = _attn_sample(a16, kv_bufs, l, P["gq"], P["gk"], P["bd"], B)
            kv_out.append(kv)
        x = _merge(y_hg, y_ssm, y_att, a16, x, P["w_out_hg"], P["w_out_ssm"], P["w_out_att"], P["w_o"], tm)
        if l % 2 == 0:
            F = P["ffn"]
            nxt = layer_params[l + 1]["moe"] if l + 1 < len(layer_params) else {}
            pending = [k for k in ("w_up", "wd") if k + "_f32" in nxt and k not in nxt]
            cast = [nxt[k + "_f32"].reshape(-1, nxt[k + "_f32"].shape[-1]) for k in pending]
            x, done = _ffn(x, P["ffn_norm"], F["w_up"], F["wd"], min(tm, 256), cast)
            for k, w in zip(pending, done):
                nxt[k] = w.reshape(nxt[k + "_f32"].shape)
        else:
            F = P["moe"]
            T = x.shape[0]
            h, comb, combt = _router(x, P["ffn_norm"], F["wr"], tm, min(MOE_SUB, T))
            x = _moe(x, h, comb, combt, F["w_up"], F["wd"], min(MOE_BLOCK, T), min(MOE_HALF, T), min(MOE_SUB, T),
                     MOE_SLOTS, 2)
    return x, st_out, kv_out


def kernel(x_prompt, x_sample, state_hgrn, state_ssm, state_conv, cache_kv_w128, cache_kv_w512, cache_kv_w2048, w_in, mix_norm, hgrn_lb_logits, hgrn_norm, ssm_conv_w, ssm_conv_b, ssm_dt_bias, ssm_a_log, ssm_d, ssm_norm, att_q_norm, att_k_norm, w_out_hgrn, w_out_ssm, w_out_att, w_o, ffn_norm, w_ffn_up, w_ffn_down, w_router, w_moe_up, w_moe_down):
    depth = w_in.shape[0]
    Bp, Lp, _ = x_prompt.shape
    Bs, Ls, _ = x_sample.shape
    R = SAMPLE_ROWS
    for buf, w in zip((cache_kv_w128, cache_kv_w512, cache_kv_w2048), ATT_WINDOWS):
        assert buf.shape[2] == w, "sample attention assumes full window buffers"

    lb_soft = jax.nn.softmax(hgrn_lb_logits.astype(F32), axis=0)
    lb_table = jnp.cumsum(lb_soft, axis=0) - lb_soft[0]
    bd = jnp.kron(jnp.eye(ATT_HEADS, dtype=F32), jnp.full((ATT_HEAD_DIM, ATT_HEAD_DIM), 1.0 / ATT_HEAD_DIM, F32)).astype(BF16)

    layer_params = []
    for l in range(depth):
        w16, w32 = _prep_w_in(w_in[l])
        P = dict(
            w16=w16, w32=w32,
            mix_norm=mix_norm[l].reshape(1, -1), ffn_norm=ffn_norm[l].reshape(1, -1),
            lb=lb_table[l].reshape(1, -1), hg_norm=hgrn_norm[l].reshape(1, -1),
            conv_w=jnp.pad(ssm_conv_w[l], ((0, 8 - SSM_CONV), (0, 0))), conv_b=ssm_conv_b[l].reshape(1, -1),
            dt_bias=_pad_lanes(ssm_dt_bias[l], LANE), a_log=_pad_lanes(ssm_a_log[l], LANE),
            d_skip=jnp.repeat(ssm_d[l].astype(F32), SSM_HEAD_DIM).reshape(1, -1),
            ssm_norm=ssm_norm[l].reshape(1, -1),
            gq=jnp.tile(att_q_norm[l], (1, ATT_HEADS)), gk=jnp.tile(att_k_norm[l], (1, ATT_HEADS)), bd=bd,
            w_out_hg=w_out_hgrn[l].astype(BF16), w_out_ssm=w_out_ssm[l].astype(BF16),
            w_out_att=w_out_att[l].astype(BF16), w_o=w_o[l].astype(BF16),
        )
        if l % 2 == 0:
            P["ffn"] = dict(w_up=w_ffn_up[l // 2].astype(BF16), wd=w_ffn_down[l // 2].astype(BF16))
        else:
            P["moe"] = dict(wr=jnp.pad(w_router[l // 2].astype(F32), ((0, 0), (0, LANE - N_EXPERTS))),
                            w_up_f32=w_moe_up[l // 2], wd_f32=w_moe_down[l // 2])
        layer_params.append(P)

    zero_p = dict(hgrn_t=jnp.zeros((Bp, HG_WIDTH, HG_DIM), F32), ssm=jnp.zeros((Bp, SSM_WIDTH, SSM_STATE), F32),
                  conv=jnp.zeros((Bp, 8, SSM_CONV_DIM), F32))
    xp, st_p, kv_p = _run_group(x_prompt.reshape(Bp * Lp, D_MODEL), Bp, Lp, 64, 64, 64, 4, 512,
                                layer_params, [zero_p] * depth, None)

    xs = jnp.pad(x_sample, ((0, 0), (0, R - Ls), (0, 0))).reshape(Bs * R, D_MODEL)
    st_s = []
    for l in range(depth):
        st_s.append(dict(
            hgrn_t=jnp.swapaxes(state_hgrn[l], -1, -2).reshape(Bs, HG_WIDTH, HG_DIM),
            ssm=state_ssm[l].reshape(Bs, SSM_WIDTH, SSM_STATE),
            conv=jnp.pad(state_conv[l], ((0, 0), (8 - (SSM_CONV - 1), 0), (0, 0)))))
    kv_s = [jnp.transpose(c, (0, 1, 3, 4, 5, 2)).reshape(depth, Bs, 2 * ATT_WIDTH, c.shape[2])
            for c in (cache_kv_w128, cache_kv_w512, cache_kv_w2048)]
    xs, st_s, kv_s = _run_group(xs, Bs, R, R, R, Ls, 4, Bs * R, layer_params, st_s, kv_s)

    def states(st, B):
        s_hg, s_ssm, s_conv = st
        return (jnp.swapaxes(s_hg.reshape(depth, B, HG_HEADS, HG_DIM, HG_DIM), -1, -2),
                s_ssm.reshape(depth, B, SSM_HEADS, SSM_HEAD_DIM, SSM_STATE),
                s_conv[:, :, 8 - (SSM_CONV - 1):])

    kv_prompt = [jnp.transpose(kv.reshape(depth, Bp, 2, ATT_HEADS, ATT_HEAD_DIM, kv.shape[3]), (0, 1, 5, 2, 3, 4))
                 for kv in kv_p]
    kv_sample = [jnp.stack([kv_s[l][j][:, :Ls].reshape(Bs, Ls, 2, ATT_HEADS, ATT_HEAD_DIM) for l in range(depth)])
                 for j in range(3)]
    y_prompt = xp.reshape(Bp, Lp, D_MODEL)
    y_sample = xs.reshape(Bs, R, D_MODEL)[:, :Ls]
    return (y_prompt, y_sample, *states(st_p, Bp), *kv_prompt, *states(st_s, Bs), *kv_sample)
```

```python
import functools

import numpy as np
import jax
import jax.numpy as jnp
from jax import lax
from jax.experimental import pallas as pl
from jax.experimental.pallas import tpu as pltpu

F32 = jnp.float32
BF16 = jnp.bfloat16

D_MODEL = 1024
HG_HEADS, HG_DIM = 4, 128
HG_WIDTH = HG_HEADS * HG_DIM
SSM_HEADS, SSM_HEAD_DIM, SSM_GROUPS, SSM_STATE, SSM_CONV = 8, 64, 2, 128, 4
SSM_WIDTH = SSM_HEADS * SSM_HEAD_DIM
SSM_CONV_DIM = SSM_WIDTH + 2 * SSM_GROUPS * SSM_STATE
ATT_WINDOWS = (128, 512, 2048)
ATT_DILATIONS = (1, 4, 16)
ATT_ORDER = (2, 1, 0)
ATT_HEADS, ATT_HEAD_DIM = 4, 64
ATT_WIDTH = ATT_HEADS * ATT_HEAD_DIM
ATT_KEYS = 128
ATT_SCALE = ATT_HEAD_DIM ** -0.5
N_EXPERTS = 8
EPS = 1e-6
NEG = -1e30

_OFF_SSM = 4 * HG_WIDTH
_OFF_ATT = _OFF_SSM + SSM_WIDTH + SSM_CONV_DIM + SSM_HEADS
_OFF_GATE = _OFF_ATT + 3 * 3 * ATT_WIDTH

A16_GATE, A16_XBC, A16_HQ, A16_HI, A16_HG, A16_Z, A16_ATT = 0, 3072, 4096, 4608, 5120, 5632, 6144
W16 = 8448
A32_HF, A32_DT = 0, 512
W32 = 640

LANE = 128
SAMPLE_ROWS = 16
HG_SUB = 8
MOE_BLOCK = 1024
MOE_SUB = 512
MOE_SLOTS = (128, 144, 160)
VMEM_LIMIT = 56 * 2 ** 20


def _cparams(sem):
    return pltpu.CompilerParams(dimension_semantics=sem, vmem_limit_bytes=VMEM_LIMIT)


def _dot(a, b):
    return jnp.dot(a, b, preferred_element_type=F32)


def _dot_nt(a, b):
    return lax.dot_general(a, b, (((1,), (1,)), ((), ())), preferred_element_type=F32)


def _dot_tn(a, b):
    return lax.dot_general(a, b, (((0,), (0,)), ((), ())), preferred_element_type=F32)


def _split3(x):
    hi = x.astype(BF16)
    r = x - hi.astype(F32)
    mid = r.astype(BF16)
    lo = (r - mid.astype(F32)).astype(BF16)
    return hi, mid, lo


def _cumsum_rows(x, tril_bf16):
    hi, mid, lo = _split3(x)
    return _dot(tril_bf16, hi) + _dot(tril_bf16, mid) + _dot(tril_bf16, lo)


def _sigmoid(x):
    return 0.5 * jnp.tanh(0.5 * x) + 0.5


def _silu(x):
    return x * _sigmoid(x)


def _tril_bf16(n):
    r = lax.broadcasted_iota(jnp.int32, (n, n), 0)
    c = lax.broadcasted_iota(jnp.int32, (n, n), 1)
    return jnp.where(c <= r, 1.0, 0.0).astype(BF16)


def _norm_proj_kernel(x_ref, g_ref, w16_ref, w32_ref, o16_ref, o32_ref, *, chunk):
    x = x_ref[...]
    h = (x * lax.rsqrt(jnp.mean(x * x, axis=-1, keepdims=True) + EPS) * g_ref[...]).astype(BF16)
    for c0 in range(0, W16, chunk):
        o16_ref[:, c0:c0 + chunk] = _dot_nt(h, w16_ref[c0:c0 + chunk, :]).astype(BF16)
    o32_ref[...] = _dot_nt(h, w32_ref[...])


def _norm_proj(x, gain, w16, w32, tm):
    T = x.shape[0]
    const = lambda i: (0, 0)
    return pl.pallas_call(
        functools.partial(_norm_proj_kernel, chunk=768),
        grid=(pl.cdiv(T, tm),),
        in_specs=[
            pl.BlockSpec((tm, D_MODEL), lambda i: (i, 0)),
            pl.BlockSpec((1, D_MODEL), const),
            pl.BlockSpec((W16, D_MODEL), const, pipeline_mode=pl.Buffered(1)),
            pl.BlockSpec((W32, D_MODEL), const, pipeline_mode=pl.Buffered(1)),
        ],
        out_specs=[
            pl.BlockSpec((tm, W16), lambda i: (i, 0)),
            pl.BlockSpec((tm, W32), lambda i: (i, 0)),
        ],
        out_shape=[jax.ShapeDtypeStruct((T, W16), BF16), jax.ShapeDtypeStruct((T, W32), F32)],
        compiler_params=_cparams(("parallel",)),
        name="norm_proj",
    )(x, gain, w16, w32)


def _hgrn_tables(C, sub):
    t = np.arange(C)[:, None]
    u = np.arange(C)[None, :]
    blocks = [u <= t]
    levels = []
    l = sub
    while l < C:
        upper = (t % (2 * l)) >= l
        b = (t // (2 * l)) * (2 * l) + l - 1
        blocks.append(np.where(upper, (u > b) & (u <= t), (u > t) & (u <= b)))
        levels.append(l)
        l *= 2
    for s in range(sub):
        r = (t // sub) * sub + s
        blocks.append((u > r) & (u <= t))
    sel = [u == (t // sub) * sub + s for s in range(sub)]
    d = np.concatenate(blocks, 0).astype(np.float32)
    sl = np.concatenate(sel, 0).astype(np.float32)
    table = np.block([[d, d, d, np.zeros_like(d)], [np.zeros((sl.shape[0], 3 * C), np.float32), sl]])
    return jnp.asarray(table, BF16), tuple(levels)


def _hgrn_chunk(q_ref, f_ref, i_ref, g_ref, lb_ref, gn_ref, tab_ref, y_ref, st_scr, *, C, sub, levels, valid):
    lb = lb_ref[...]
    fgate = lb + (1.0 - lb) * jax.nn.sigmoid(f_ref[...])
    logf = jnp.log(fgate)
    kk = 1.0 - fgate
    rows = lax.broadcasted_iota(jnp.int32, (C, HG_WIDTH), 0)
    if valid < C:
        logf = jnp.where(rows < valid, logf, 0.0)
        kk = jnp.where(rows < valid, kk, 0.0)
    qa = _silu(q_ref[...].astype(F32))
    va = i_ref[...].astype(F32)
    ga = g_ref[...].astype(F32)

    ed = _dot(tab_ref[...], jnp.concatenate([*_split3(logf), kk.astype(BF16)], axis=0))
    nblk = 1 + len(levels) + sub
    blk = lambda j: ed[j * C:(j + 1) * C]
    G = blk(0)
    ksel = ed[nblk * C:]

    row_c = lax.broadcasted_iota(jnp.int32, (C, C), 0)
    lane_c = lax.broadcasted_iota(jnp.int32, (C, C), 1)
    off_c = lane_c - (row_c // sub) * sub

    A = [jnp.zeros((C, C), F32) for _ in range(HG_HEADS)]
    for s in range(sub):
        prod = qa * ksel[s * C:(s + 1) * C] * jnp.exp(blk(1 + len(levels) + s))
        for h in range(HG_HEADS):
            col = jnp.sum(prod[:, h * HG_DIM:(h + 1) * HG_DIM], axis=-1, keepdims=True)
            A[h] = jnp.where(off_c == s, col, A[h])
    A = [jnp.where(lane_c <= row_c, a, 0.0) for a in A]

    for li, l in enumerate(levels):
        upper = (rows % (2 * l)) >= l
        scaled = jnp.where(upper, qa, kk) * jnp.exp(blk(1 + li))
        qt = jnp.where(upper, scaled, 0.0).astype(BF16)
        kt = jnp.where(upper, 0.0, scaled).astype(BF16)
        same = (row_c // (2 * l)) == (lane_c // (2 * l))
        for h in range(HG_HEADS):
            sl = slice(h * HG_DIM, (h + 1) * HG_DIM)
            a_l = _dot_nt(qt[:, sl], kt[:, sl])
            A[h] = A[h] + (jnp.where(same, a_l, 0.0) if 2 * l < C else a_l)

    outs = []
    for h in range(HG_HEADS):
        sl = slice(h * HG_DIM, (h + 1) * HG_DIM)
        Gh, qh, kh = G[:, sl], qa[:, sl], kk[:, sl]
        vb = va[:, sl].astype(BF16)
        st = st_scr[sl, :]
        o = _dot(A[h].astype(BF16), vb) + _dot_nt((qh * jnp.exp(Gh)).astype(BF16), st.astype(BF16))
        g_last = Gh[C - 1:C]
        kdec = kh * jnp.exp(g_last - Gh)
        st_scr[sl, :] = jnp.exp(g_last) * st + _dot_tn(vb, kdec.astype(BF16))
        on = o * lax.rsqrt(jnp.mean(o * o, axis=-1, keepdims=True) + EPS) * gn_ref[...]
        outs.append(on * _silu(ga[:, sl]))
    y_ref[...] = jnp.concatenate(outs, axis=1).astype(BF16)


def _ssd_chunk(xbc_ref, z_ref, dt_ref, cw_ref, cb_ref, dtb_ref, alog_ref, dsk_ref, sn_ref,
               y_ref, ubuf, hs, *, C, valid):
    ubuf[8:8 + C, :] = xbc_ref[...].astype(F32)
    acc = cb_ref[...] + ubuf[5:5 + C, :] * cw_ref[0:1, :]
    for j in range(1, SSM_CONV):
        acc = acc + ubuf[5 + j:5 + j + C, :] * cw_ref[j:j + 1, :]
    xc = _silu(acc)
    ubuf[0:8, :] = ubuf[valid:valid + 8, :]

    x = dt_ref[...] + dtb_ref[...]
    dt = jnp.maximum(x, 0.0) + jnp.log1p(jnp.exp(-jnp.abs(x)))
    if valid < C:
        rows = lax.broadcasted_iota(jnp.int32, (C, LANE), 0)
        dt = jnp.where(rows < valid, dt, 0.0)
    a = -jnp.exp(alog_ref[...])
    cum = _cumsum_rows(dt * a, _tril_bf16(C))
    cum_t = cum.T
    dt_t = dt.T

    ri = lax.broadcasted_iota(jnp.int32, (C, C), 0)
    cj = lax.broadcasted_iota(jnp.int32, (C, C), 1)
    tril = cj <= ri
    lane = lax.broadcasted_iota(jnp.int32, (C, LANE), 1)
    srow = lax.broadcasted_iota(jnp.int32, (LANE, LANE), 0)
    bm = [xc[:, SSM_WIDTH + g * SSM_STATE:SSM_WIDTH + (g + 1) * SSM_STATE].astype(BF16) for g in range(SSM_GROUPS)]
    off_c = SSM_WIDTH + SSM_GROUPS * SSM_STATE
    cm = [xc[:, off_c + g * SSM_STATE:off_c + (g + 1) * SSM_STATE].astype(BF16) for g in range(SSM_GROUPS)]
    cb = [_dot_nt(cm[g], bm[g]) for g in range(SSM_GROUPS)]

    ys = []
    for p in range(SSM_HEADS // 2):
        g = (2 * p) // (SSM_HEADS // SSM_GROUPS)
        xp = xc[:, p * LANE:(p + 1) * LANE]
        yp = None
        colw = []
        for k, hd in enumerate((2 * p, 2 * p + 1)):
            colb = jnp.broadcast_to(cum[:, hd:hd + 1], (C, LANE))
            rowb = jnp.broadcast_to(cum_t[hd:hd + 1, :], (C, C))
            dtr = jnp.broadcast_to(dt_t[hd:hd + 1, :], (C, C))
            w = cb[g] * jnp.exp(jnp.where(tril, colb[:, :C] - rowb, NEG)) * dtr
            half = (lane < SSM_HEAD_DIM) if k == 0 else (lane >= SSM_HEAD_DIM)
            t = _dot(w.astype(BF16), jnp.where(half, xp, 0.0).astype(BF16))
            yp = t if yp is None else yp + t
            colw.append((colb, jnp.broadcast_to(dt[:, hd:hd + 1], (C, LANE))))
        first = lane < SSM_HEAD_DIM
        cum_e = jnp.where(first, colw[0][0], colw[1][0])
        dt_e = jnp.where(first, colw[0][1], colw[1][1])
        hp = hs[p * LANE:(p + 1) * LANE, :]
        yp = yp + jnp.exp(cum_e) * _dot_nt(cm[g], hp.astype(BF16))
        c_last = cum_e[C - 1:C, :]
        w_in = jnp.exp(c_last - cum_e) * dt_e
        dec = jnp.where(srow < SSM_HEAD_DIM,
                        jnp.broadcast_to(cum[C - 1:C, 2 * p:2 * p + 1], (LANE, LANE)),
                        jnp.broadcast_to(cum[C - 1:C, 2 * p + 1:2 * p + 2], (LANE, LANE)))
        hs[p * LANE:(p + 1) * LANE, :] = jnp.exp(dec) * hp + _dot_tn((xp * w_in).astype(BF16), bm[g])
        ys.append(yp)
    y = jnp.concatenate(ys, axis=1) + dsk_ref[...] * xc[:, :SSM_WIDTH]
    yz = y * _silu(z_ref[...].astype(F32))
    y_ref[...] = (yz * lax.rsqrt(jnp.mean(yz * yz, axis=-1, keepdims=True) + EPS) * sn_ref[...]).astype(BF16)


def _rec_kernel(q_ref, f_ref, i_ref, g_ref, lb_ref, gn_ref, s0_ref, tab_ref,
                xbc_ref, z_ref, dt_ref, pre_ref, cw_ref, cb_ref, dtb_ref, alog_ref, dsk_ref, sn_ref, h0_ref,
                *rest, NB, nchunks, first, hg_kw, ssd_kw):
    yh_ref, sout_ref, ys_ref, hout_ref, cout_ref, st_scr, ubuf, hs = rest[-8:]
    ci = pl.program_id(1)

    @pl.when(ci == 0)
    def _():
        st_scr[...] = s0_ref[...]
        ubuf[:, 0:8, :] = pre_ref[...]
        hs[...] = h0_ref[...]

    for j in range(NB):
        _hgrn_chunk(q_ref.at[j], f_ref.at[j], i_ref.at[j], g_ref.at[j], lb_ref, gn_ref, tab_ref,
                    yh_ref.at[j], st_scr.at[j], **hg_kw)
        _ssd_chunk(xbc_ref.at[j], z_ref.at[j], dt_ref.at[j], cw_ref, cb_ref, dtb_ref, alog_ref, dsk_ref, sn_ref,
                   ys_ref.at[j], ubuf.at[j], hs.at[j], **ssd_kw)

    @pl.when(ci == nchunks - 1)
    def _():
        for d in range(sout_ref.shape[0]):
            sout_ref[d] = st_scr[...]
            hout_ref[d] = hs[...]
            cout_ref[d] = ubuf[:, 0:8, :]


def _recurrent_mixers(a16, a32, lb, gn, s0t, prefix, cw, cb, dtb, alog, dsk, sn, h0, B, L, C, sub, valid, NB,
                      layer, depth, prev):
    nch = L // C
    table, levels = _hgrn_tables(C, sub)
    const = lambda b, c: (0, 0)
    bat = lambda b, c: (b, 0, 0)
    blk = lambda w, col: pl.BlockSpec((NB, C, w), lambda b, c: (b, c, col))
    first = prev is None
    slot = (lambda rows, w: pl.BlockSpec((depth, NB, rows, w), lambda b, c: (0, b, 0, 0))) if first else \
           (lambda rows, w: pl.BlockSpec((1, NB, rows, w), lambda b, c: (layer, b, 0, 0)))
    return pl.pallas_call(
        functools.partial(_rec_kernel, NB=NB, nchunks=nch, first=first,
                          hg_kw=dict(C=C, sub=sub, levels=levels, valid=valid), ssd_kw=dict(C=C, valid=valid)),
        grid=(B // NB, nch),
        in_specs=[
            blk(HG_WIDTH, A16_HQ // HG_WIDTH), blk(HG_WIDTH, A32_HF // HG_WIDTH),
            blk(HG_WIDTH, A16_HI // HG_WIDTH), blk(HG_WIDTH, A16_HG // HG_WIDTH),
            pl.BlockSpec((1, HG_WIDTH), const),
            pl.BlockSpec((1, HG_DIM), const),
            pl.BlockSpec((NB, HG_WIDTH, HG_DIM), bat),
            pl.BlockSpec(table.shape, const),
            blk(SSM_CONV_DIM, A16_XBC // SSM_CONV_DIM), blk(SSM_WIDTH, A16_Z // SSM_WIDTH), blk(LANE, A32_DT // LANE),
            pl.BlockSpec((NB, 8, SSM_CONV_DIM), bat),
            pl.BlockSpec((8, SSM_CONV_DIM), const),
            pl.BlockSpec((1, SSM_CONV_DIM), const),
            pl.BlockSpec((1, LANE), const),
            pl.BlockSpec((1, LANE), const),
            pl.BlockSpec((1, SSM_WIDTH), const),
            pl.BlockSpec((1, SSM_WIDTH), const),
            pl.BlockSpec((NB, SSM_WIDTH, SSM_STATE), bat),
        ] + ([] if first else [pl.BlockSpec(memory_space=pl.ANY)] * 3),
        out_specs=[
            blk(HG_WIDTH, 0), slot(HG_WIDTH, HG_DIM),
            blk(SSM_WIDTH, 0), slot(SSM_WIDTH, SSM_STATE), slot(8, SSM_CONV_DIM),
        ],
        out_shape=[jax.ShapeDtypeStruct((B, L, HG_WIDTH), BF16),
                   jax.ShapeDtypeStruct((depth, B, HG_WIDTH, HG_DIM), F32),
                   jax.ShapeDtypeStruct((B, L, SSM_WIDTH), BF16),
                   jax.ShapeDtypeStruct((depth, B, SSM_WIDTH, SSM_STATE), F32),
                   jax.ShapeDtypeStruct((depth, B, 8, SSM_CONV_DIM), F32)],
        input_output_aliases={} if first else {19: 1, 20: 3, 21: 4},
        scratch_shapes=[pltpu.VMEM((NB, HG_WIDTH, HG_DIM), F32), pltpu.VMEM((NB, C + 8, SSM_CONV_DIM), F32),
                        pltpu.VMEM((NB, SSM_WIDTH, SSM_STATE), F32)],
        compiler_params=_cparams(("parallel", "arbitrary")),
        name="hgrn2_ssd",
    )(a16, a32, a16, a16, lb, gn, s0t, table, a16, a16, a32, prefix, cw, cb, dtb, alog, dsk, sn, h0,
      *([] if first else prev))


def _head_norm(x, gain, bd):
    ms = _dot((x * x).astype(BF16), bd)
    return x * (lax.rsqrt(ms + EPS) * gain)


def _head_masks(rows):
    lane = lax.broadcasted_iota(jnp.int32, (rows, ATT_WIDTH), 1)
    return [(lane >= h * ATT_HEAD_DIM) & (lane < (h + 1) * ATT_HEAD_DIM) for h in range(ATT_HEADS)]


def _stack_heads(q, masks):
    return jnp.concatenate([jnp.where(m, q, 0.0) for m in masks], axis=0).astype(BF16)


def _unstack_heads(pv, m, d, masks):
    R = pv.shape[0] // ATT_HEADS
    acc, mx, den = pv[0:R], jnp.broadcast_to(m[0:R], (R, ATT_WIDTH)), jnp.broadcast_to(d[0:R], (R, ATT_WIDTH))
    for h in range(1, ATT_HEADS):
        sl = slice(h * R, (h + 1) * R)
        acc = jnp.where(masks[h], pv[sl], acc)
        mx = jnp.where(masks[h], m[sl], mx)
        den = jnp.where(masks[h], d[sl], den)
    return acc, mx, den


def _attend(qb, kb, vb, bias4, masks):
    s = _dot_nt(_stack_heads(qb, masks), kb) + bias4
    m = jnp.max(s, axis=-1, keepdims=True)
    p = jnp.exp(s - m)
    d = jnp.sum(p, axis=-1, keepdims=True)
    return _unstack_heads(_dot(p.astype(BF16), vb), m, d, masks)


def _ld2(pair, s):
    return jnp.concatenate([pair[0][s, :], pair[1][s, :]], axis=1)


def _st2(pair, s, val):
    pair[0][s, :] = val[:, :LANE]
    pair[1][s, :] = val[:, LANE:]


def _attn_prompt_kernel(att_ref, gq_ref, gk_ref, bd_ref, *rest, L):
    y_ref, kv0_ref, kv1_ref, kv2_ref = rest[-16:-12]
    scr = rest[-12:]
    qn, kn, vv, ya, ma, za = (scr[2 * j:2 * j + 2] for j in range(6))
    g = pl.program_id(1)
    RB = 256
    QB = ATT_KEYS

    def group(step, gi, dil, keep, kv_ref):
        M = L // dil
        nb = M // QB
        KW = min(2 * QB, M)
        bd = bd_ref[...]

        def norm_body(i, carry):
            r0 = pl.multiple_of(i * RB, RB)
            x = att_ref[pl.ds(r0, RB), :].astype(F32)
            _st2(qn, pl.ds(r0, RB), _head_norm(x[:, :ATT_WIDTH], gq_ref[gi:gi + 1, :] * ATT_SCALE, bd))
            _st2(kn, pl.ds(r0, RB), _head_norm(x[:, ATT_WIDTH:2 * ATT_WIDTH], gk_ref[gi:gi + 1, :], bd))
            _st2(vv, pl.ds(r0, RB), x[:, 2 * ATT_WIDTH:])
            return carry

        lax.fori_loop(0, L // RB, norm_body, 0)
        cw = min(keep, RB)
        for c0 in range(0, keep, cw):
            rows = slice(L - keep + c0, L - keep + c0 + cw)
            for p, half in enumerate((kn[0], kn[1], vv[0], vv[1])):
                piece = half[rows, :].T
                for d in range(kv_ref.shape[0]):
                    kv_ref[d, 0, p * LANE:(p + 1) * LANE, c0:c0 + cw] = piece

        masks = _head_masks(QB)
        iq = lax.broadcasted_iota(jnp.int32, (ATT_HEADS * QB, KW), 0) & (QB - 1)
        ik = lax.broadcasted_iota(jnp.int32, (ATT_HEADS * QB, KW), 1)

        def run_blocks(first, count, lead):
            dist = (iq - ik) if lead else (QB + iq - ik)
            bias4 = jnp.where((dist >= 0) & (dist <= ATT_KEYS), 0.0, NEG)
            unroll = max(u for u in (4, 3, 2, 1) if count % u == 0)

            def blk_body(j, carry):
                loaded = []
                for u in range(unroll):
                    it = first + j * unroll + u
                    n = it // dil
                    r = it - n * dil
                    k0 = 0 if lead else (n - 1) * QB
                    if dil == 1:
                        qs = pl.ds(pl.multiple_of(n * QB, QB), QB)
                        ks = pl.ds(pl.multiple_of(k0, QB), KW)
                    else:
                        qs = pl.ds(n * QB * dil + r, QB, stride=dil)
                        ks = pl.ds(k0 * dil + r, KW, stride=dil)
                    old = None if step == 0 else (_ld2(ya, qs), _ld2(ma, qs), _ld2(za, qs))
                    loaded.append((qs, _ld2(qn, qs), _ld2(kn, ks).astype(BF16), _ld2(vv, ks).astype(BF16), old))
                for qs, qb, kb, vb, old in loaded:
                    acc, mx, den = _attend(qb, kb, vb, bias4, masks)
                    if step > 0:
                        yo, mo, zo = old
                        mn = jnp.maximum(mo, mx)
                        eo = jnp.exp(mo - mn)
                        en = jnp.exp(mx - mn)
                        acc, mx, den = yo * eo + acc * en, mn, zo * eo + den * en
                    _st2(ya, qs, acc)
                    _st2(ma, qs, mx)
                    _st2(za, qs, den)
                return carry

            lax.fori_loop(0, count // unroll, blk_body, 0)

        run_blocks(0, dil, True)
        if nb > 1:
            run_blocks(dil, dil * (nb - 1), False)

    for step, gi in enumerate(ATT_ORDER):
        @pl.when(g == step)
        def _(step=step, gi=gi):
            group(step, gi, ATT_DILATIONS[gi], min(ATT_WINDOWS[gi], L), (kv0_ref, kv1_ref, kv2_ref)[gi])

    @pl.when(g == 2)
    def _():
        def out_body(i, carry):
            r0 = pl.multiple_of(i * RB, RB)
            y_ref[pl.ds(r0, RB), :] = (_ld2(ya, pl.ds(r0, RB)) / _ld2(za, pl.ds(r0, RB))).astype(BF16)
            return carry
        lax.fori_loop(0, L // RB, out_body, 0)


def _attn_prompt(a16, gq, gk, bd, B, L, layer, depth, prev):
    keeps = [min(w, L) for w in ATT_WINDOWS]
    const = lambda b, g: (0, 0)
    first = prev is None
    slot = (lambda k: pl.BlockSpec((depth, 1, 2 * ATT_WIDTH, k), lambda b, g: (0, b, 0, 0))) if first else \
           (lambda k: pl.BlockSpec((1, 1, 2 * ATT_WIDTH, k), lambda b, g: (layer, b, 0, 0)))
    return pl.pallas_call(
        functools.partial(_attn_prompt_kernel, L=L),
        grid=(B, 3),
        in_specs=[
            pl.BlockSpec((L, 3 * ATT_WIDTH), lambda b, g: (b, A16_ATT // (3 * ATT_WIDTH) + 2 - g)),
            pl.BlockSpec((3, ATT_WIDTH), const),
            pl.BlockSpec((3, ATT_WIDTH), const),
            pl.BlockSpec((ATT_WIDTH, ATT_WIDTH), const),
        ] + ([] if first else [pl.BlockSpec(memory_space=pl.ANY)] * 3),
        out_specs=[pl.BlockSpec((L, ATT_WIDTH), lambda b, g: (b, 0))] + [slot(k) for k in keeps],
        out_shape=[jax.ShapeDtypeStruct((B * L, ATT_WIDTH), BF16)]
        + [jax.ShapeDtypeStruct((depth, B, 2 * ATT_WIDTH, k), F32) for k in keeps],
        input_output_aliases={} if first else {4: 1, 5: 2, 6: 3},
        scratch_shapes=[pltpu.VMEM((L, LANE), F32) for _ in range(12)],
        compiler_params=_cparams(("parallel", "arbitrary")),
        name="attn_prompt",
    )(a16, gq, gk, bd, *([] if first else prev))


def _attn_sample_kernel(a0_ref, a1_ref, a2_ref, b0_ref, b1_ref, b2_ref, gq_ref, gk_ref, bd_ref,
                        y_ref, k0_ref, k1_ref, k2_ref):
    R = SAMPLE_ROWS
    bd = bd_ref[...]
    masks = _head_masks(R)
    ya = ma = za = None
    for gi, (a_ref, b_ref, k_ref) in enumerate(((a0_ref, b0_ref, k0_ref), (a1_ref, b1_ref, k1_ref),
                                                (a2_ref, b2_ref, k2_ref))):
        dil = ATT_DILATIONS[gi]
        wb = b_ref.shape[3]
        x = a_ref[...].astype(F32)
        q = _head_norm(x[:, :ATT_WIDTH], gq_ref[gi:gi + 1, :] * ATT_SCALE, bd)
        k = _head_norm(x[:, ATT_WIDTH:2 * ATT_WIDTH], gk_ref[gi:gi + 1, :], bd)
        v = x[:, 2 * ATT_WIDTH:]
        k_ref[0] = jnp.concatenate([k[:8], v[:8]], axis=1)
        k_t = b_ref[0, 0, 0:ATT_WIDTH, :].astype(BF16)
        v_t = b_ref[0, 0, ATT_WIDTH:2 * ATT_WIDTH, :].astype(BF16)
        kb, vb = k.astype(BF16), v.astype(BF16)

        def bias(dist):
            return jnp.where((dist >= 0) & (dist <= ATT_KEYS * dil) & ((dist & (dil - 1)) == 0), 0.0, NEG)

        R4 = ATT_HEADS * R
        row_b = lax.broadcasted_iota(jnp.int32, (R4, wb), 0) & (R - 1)
        row_n = lax.broadcasted_iota(jnp.int32, (R4, R), 0) & (R - 1)
        qs = _stack_heads(q, masks)
        s_b = _dot(qs, k_t) + bias(wb + row_b - lax.broadcasted_iota(jnp.int32, (R4, wb), 1))
        s_n = _dot_nt(qs, kb) + bias(row_n - lax.broadcasted_iota(jnp.int32, (R4, R), 1))
        m = jnp.maximum(jnp.max(s_b, axis=-1, keepdims=True), jnp.max(s_n, axis=-1, keepdims=True))
        p_b = jnp.exp(s_b - m)
        p_n = jnp.exp(s_n - m)
        d = jnp.sum(p_b, axis=-1, keepdims=True) + jnp.sum(p_n, axis=-1, keepdims=True)
        pv = _dot_nt(p_b.astype(BF16), v_t) + _dot(p_n.astype(BF16), vb)
        acc, mx, den = _unstack_heads(pv, m, d, masks)
        if gi == 0:
            ya, ma, za = acc, mx, den
        else:
            mn = jnp.maximum(ma, mx)
            eo = jnp.exp(ma - mn)
            en = jnp.exp(mx - mn)
            ya, za, ma = ya * eo + acc * en, za * eo + den * en, mn
    y_ref[...] = (ya / za).astype(BF16)


def _attn_sample(a16, bufs, layer, gq, gk, bd, B):
    R = SAMPLE_ROWS
    const = lambda b: (0, 0)
    bat = lambda b: (b, 0, 0)
    base = A16_ATT // (3 * ATT_WIDTH)
    return pl.pallas_call(
        _attn_sample_kernel,
        grid=(B,),
        in_specs=[pl.BlockSpec((R, 3 * ATT_WIDTH), functools.partial(lambda b, j: (b, j), j=base + g))
                  for g in range(3)]
        + [pl.BlockSpec((1, 1, 2 * ATT_WIDTH, buf.shape[3]), lambda b: (layer, b, 0, 0)) for buf in bufs]
        + [pl.BlockSpec((3, ATT_WIDTH), const), pl.BlockSpec((3, ATT_WIDTH), const),
           pl.BlockSpec((ATT_WIDTH, ATT_WIDTH), const)],
        out_specs=[pl.BlockSpec((R, ATT_WIDTH), lambda b: (b, 0))]
        + [pl.BlockSpec((1, 8, 2 * ATT_WIDTH), bat) for _ in range(3)],
        out_shape=[jax.ShapeDtypeStruct((B * R, ATT_WIDTH), BF16)]
        + [jax.ShapeDtypeStruct((B, 8, 2 * ATT_WIDTH), F32) for _ in range(3)],
        compiler_params=_cparams(("parallel",)),
        name="attn_sample",
    )(a16, a16, a16, *bufs, gq, gk, bd)


def _merge_kernel(yh_ref, ys_ref, ya_ref, gt_ref, x_ref, wh_ref, ws_ref, wa_ref, wo_ref, o_ref):
    gt = _sigmoid(gt_ref[...].astype(F32))
    mixed = (gt[:, 0:D_MODEL] * _dot(yh_ref[...], wh_ref[...])
             + gt[:, D_MODEL:2 * D_MODEL] * _dot(ys_ref[...], ws_ref[...])
             + gt[:, 2 * D_MODEL:] * _dot(ya_ref[...], wa_ref[...]))
    o_ref[...] = x_ref[...] + _dot(mixed.astype(BF16), wo_ref[...])


def _merge(y_hg, y_ssm, y_att, a16, x, wh, ws, wa, wo, tm):
    T = x.shape[0]
    const = lambda i: (0, 0)
    rowb = lambda w: pl.BlockSpec((tm, w), lambda i: (i, 0))
    wsp = lambda w: pl.BlockSpec(w.shape, const, pipeline_mode=pl.Buffered(1))
    return pl.pallas_call(
        _merge_kernel,
        grid=(pl.cdiv(T, tm),),
        in_specs=[rowb(HG_WIDTH), rowb(SSM_WIDTH), rowb(ATT_WIDTH),
                  pl.BlockSpec((tm, 3 * D_MODEL), lambda i: (i, A16_GATE // (3 * D_MODEL))),
                  rowb(D_MODEL), wsp(wh), wsp(ws), wsp(wa), wsp(wo)],
        out_specs=rowb(D_MODEL),
        out_shape=jax.ShapeDtypeStruct((T, D_MODEL), F32),
        compiler_params=_cparams(("parallel",)),
        name="merge_out_proj",
    )(y_hg, y_ssm, y_att, a16, x, wh, ws, wa, wo)


def _ffn_kernel(x_ref, g_ref, wg_ref, wu_ref, wd_ref, *rest):
    n = (len(rest) - 1) // 2
    cast_in, o_ref, cast_out = rest[:n], rest[n], rest[n + 1:]
    x = x_ref[...]
    h = (x * lax.rsqrt(jnp.mean(x * x, axis=-1, keepdims=True) + EPS) * g_ref[...]).astype(BF16)
    a = _silu(_dot(h, wg_ref[...])) * _dot(h, wu_ref[...])
    o_ref[...] = x + _dot(a.astype(BF16), wd_ref[...])
    for src, dst in zip(cast_in, cast_out):
        dst[...] = src[...].astype(BF16)


def _ffn(x, gain, w_up, wd, tm, cast=()):
    T = x.shape[0]
    dff = wd.shape[0]
    steps = pl.cdiv(T, tm)
    const = lambda i: (0, 0)
    wsp = lambda w: pl.BlockSpec(w.shape, const, pipeline_mode=pl.Buffered(1))
    slab = [pl.BlockSpec((w.shape[0] // steps, w.shape[1]), lambda i: (i, 0)) for w in cast]
    for w in cast:
        assert w.shape[0] % (16 * steps) == 0
    outs = pl.pallas_call(
        _ffn_kernel,
        grid=(steps,),
        in_specs=[pl.BlockSpec((tm, D_MODEL), lambda i: (i, 0)), pl.BlockSpec((1, D_MODEL), const),
                  pl.BlockSpec((D_MODEL, dff), const, pipeline_mode=pl.Buffered(1)),
                  pl.BlockSpec((D_MODEL, dff), lambda i: (0, 1), pipeline_mode=pl.Buffered(1)),
                  wsp(wd)] + slab,
        out_specs=[pl.BlockSpec((tm, D_MODEL), lambda i: (i, 0))] + slab,
        out_shape=[jax.ShapeDtypeStruct((T, D_MODEL), F32)] + [jax.ShapeDtypeStruct(w.shape, BF16) for w in cast],
        compiler_params=_cparams(("parallel",)),
        name="ffn_dense",
    )(x, gain, w_up, w_up, wd, *cast)
    return outs[0], outs[1:]


def _router_kernel(x_ref, g_ref, wr_ref, h_ref, comb_ref, combt_ref):
    x = x_ref[...]
    h = x * lax.rsqrt(jnp.mean(x * x, axis=-1, keepdims=True) + EPS) * g_ref[...]
    hb = h.astype(BF16)
    h_ref[...] = hb
    h_lo = (h - hb.astype(F32)).astype(BF16)
    w = wr_ref[...]
    w_hi = w.astype(BF16)
    w_lo = (w - w_hi.astype(F32)).astype(BF16)
    logits = _dot(hb, w_hi) + _dot(hb, w_lo) + _dot(h_lo, w_hi)
    lane = lax.broadcasted_iota(jnp.int32, logits.shape, 1)
    logits = jnp.where(lane < N_EXPERTS, logits, NEG)
    v1 = jnp.max(logits, axis=-1, keepdims=True)
    i1 = jnp.min(jnp.where(logits == v1, lane, LANE), axis=-1, keepdims=True)
    rest = jnp.where(lane == i1, NEG, logits)
    v2 = jnp.max(rest, axis=-1, keepdims=True)
    i2 = jnp.min(jnp.where(rest == v2, lane, LANE), axis=-1, keepdims=True)
    e = jnp.exp(v2 - v1)
    w1 = 1.0 / (1.0 + e)
    w2 = e / (1.0 + e)
    comb = jnp.where(lane == i1, w1, 0.0) + jnp.where(lane == i2, w2, 0.0)
    comb_ref[...] = comb
    combt_ref[...] = comb.T[0:N_EXPERTS, :]


def _router(x, gain, wr, tm):
    T = x.shape[0]
    const = lambda i: (0, 0)
    return pl.pallas_call(
        _router_kernel,
        grid=(pl.cdiv(T, tm),),
        in_specs=[pl.BlockSpec((tm, D_MODEL), lambda i: (i, 0)), pl.BlockSpec((1, D_MODEL), const),
                  pl.BlockSpec((D_MODEL, LANE), const)],
        out_specs=[pl.BlockSpec((tm, D_MODEL), lambda i: (i, 0)), pl.BlockSpec((tm, LANE), lambda i: (i, 0)),
                   pl.BlockSpec((N_EXPERTS, tm), lambda i: (0, i))],
        out_shape=[jax.ShapeDtypeStruct((T, D_MODEL), BF16), jax.ShapeDtypeStruct((T, LANE), F32),
                   jax.ShapeDtypeStruct((N_EXPERTS, T), F32)],
        compiler_params=_cparams(("parallel",)),
        name="moe_router",
    )(x, gain, wr)


def _moe_kernel(count_ref, h_ref, x_ref, comb_ref, combt_ref, tri_ref, wg_ref, wu_ref, wd_ref, o_ref,
                rank_scr, hs_scr, ys_scr, *, TB, SB, CH, nf):
    e = pl.program_id(1)
    f = pl.program_id(2)

    nsub = TB // SB
    subs = [slice(j * SB, (j + 1) * SB) for j in range(nsub)]

    @pl.when((e == 0) & (f == 0))
    def _():
        o_ref[...] = x_ref[...]
        routed_all = jnp.where(combt_ref[...] > 0.0, 1.0, 0.0).astype(BF16)
        for sb in subs:
            rank_scr[:, sb] = _dot(routed_all[:, sb], tri_ref[...])

    routed = combt_ref[pl.ds(e, 1), :] > 0.0
    rank = rank_scr[pl.ds(e, 1), :]
    count = count_ref[pl.program_id(0), e]

    def body(c, carry, CH):
        slot = lax.broadcasted_iota(jnp.int32, (CH, SB), 0).astype(F32)
        if isinstance(c, int):
            rows, base = pl.ds(c * nsub * CH, nsub * CH), float(c * CH)
        else:
            rows, base = pl.ds(pl.multiple_of(c * (nsub * CH), 16), nsub * CH), (c * CH).astype(F32)
        onehots = [jnp.where((rank[:, sb] - base == slot) & routed[:, sb], 1.0, 0.0).astype(BF16) for sb in subs]

        def expert(hs):
            a = _silu(_dot(hs, wg_ref[0])) * _dot(hs, wu_ref[0])
            return _dot(a.astype(BF16), wd_ref[0])

        @pl.when(f == 0)
        def _():
            hs = jnp.concatenate([_dot(p, h_ref[sb, :]) for p, sb in zip(onehots, subs)], axis=0).astype(BF16)
            hs_scr[rows, :] = hs
            ys_scr[rows, :] = expert(hs)

        if nf > 2:
            @pl.when((f > 0) & (f < nf - 1))
            def _():
                ys_scr[rows, :] += expert(hs_scr[rows, :])

        @pl.when(f == nf - 1)
        def _():
            ys = (ys_scr[rows, :] + expert(hs_scr[rows, :])).astype(BF16)
            lane = lax.broadcasted_iota(jnp.int32, (TB, LANE), 1)
            gate = jnp.sum(jnp.where(lane == e, comb_ref[...], 0.0), axis=-1, keepdims=True)
            for j, (p, sb) in enumerate(zip(onehots, subs)):
                o_ref[sb, :] += gate[sb] * _dot_tn(p, ys[j * CH:(j + 1) * CH])

        return carry

    lo = 0
    for ch in CH[:-1]:
        @pl.when((count > lo) & (count <= ch))
        def _(ch=ch):
            body(0, 0, ch)
        lo = ch

    @pl.when(count > lo)
    def _():
        lax.fori_loop(0, (count + CH[-1] - 1) // CH[-1], functools.partial(body, CH=CH[-1]), 0)


def _moe(x, h, comb, combt, w_up, wd, TB, SB, CH, nf):
    assert nf >= 2, "the first and last d_ff positions are separate code paths"
    T = x.shape[0]
    dff = wd.shape[1]
    tf = dff // nf
    slots = pl.cdiv(SB, CH[-1]) * CH[-1] * (TB // SB)
    tri = jnp.triu(jnp.ones((SB, SB), BF16), k=1)
    counts = jnp.sum((combt > 0.0).reshape(N_EXPERTS, T // TB, TB // SB, SB), axis=-1, dtype=jnp.int32)
    counts = jnp.max(counts, axis=-1).T
    once = pl.Buffered(1)
    return pl.pallas_call(
        functools.partial(_moe_kernel, TB=TB, SB=SB, CH=CH, nf=nf),
        grid_spec=pltpu.PrefetchScalarGridSpec(
            num_scalar_prefetch=1,
            grid=(T // TB, N_EXPERTS, nf),
            in_specs=[pl.BlockSpec((TB, D_MODEL), lambda i, e, f, c: (i, 0), pipeline_mode=once),
                      pl.BlockSpec((TB, D_MODEL), lambda i, e, f, c: (i, 0), pipeline_mode=once),
                      pl.BlockSpec((TB, LANE), lambda i, e, f, c: (i, 0)),
                      pl.BlockSpec((N_EXPERTS, TB), lambda i, e, f, c: (0, i)),
                      pl.BlockSpec((SB, SB), lambda i, e, f, c: (0, 0), pipeline_mode=once),
                      pl.BlockSpec((1, D_MODEL, tf), lambda i, e, f, c: (e, 0, f)),
                      pl.BlockSpec((1, D_MODEL, tf), lambda i, e, f, c: (e, 0, nf + f)),
                      pl.BlockSpec((1, tf, D_MODEL), lambda i, e, f, c: (e, f, 0))],
            out_specs=pl.BlockSpec((TB, D_MODEL), lambda i, e, f, c: (i, 0)),
            scratch_shapes=[pltpu.VMEM((N_EXPERTS, TB), F32), pltpu.VMEM((slots, D_MODEL), BF16),
                            pltpu.VMEM((slots, D_MODEL), F32)]),
        out_shape=jax.ShapeDtypeStruct((T, D_MODEL), F32),
        compiler_params=_cparams(("parallel", "arbitrary", "arbitrary")),
        name="moe_top2",
    )(counts, h, x, comb, combt, tri, w_up, w_up, wd)


def _prep_w_in(w):
    wt = w.T
    hq, hf, hi, hg = (wt[j * HG_WIDTH:(j + 1) * HG_WIDTH] for j in range(4))
    z = wt[_OFF_SSM:_OFF_SSM + SSM_WIDTH]
    xbc = wt[_OFF_SSM + SSM_WIDTH:_OFF_SSM + SSM_WIDTH + SSM_CONV_DIM]
    dt = wt[_OFF_ATT - SSM_HEADS:_OFF_ATT]
    att = wt[_OFF_ATT:_OFF_GATE]
    gates = wt[_OFF_GATE:]
    w16 = jnp.concatenate([gates, xbc, hq, hi, hg, z, att], axis=0).astype(BF16)
    w32 = jnp.concatenate([hf, dt, jnp.zeros((LANE - SSM_HEADS, D_MODEL), w.dtype)], axis=0).astype(BF16)
    return w16, w32


def _pad_lanes(v, n, value=0.0):
    v = v.reshape(1, -1).astype(F32)
    return jnp.pad(v, ((0, 0), (0, n - v.shape[1])), constant_values=value)


def _run_group(x, B, L, C_h, C_s, valid, NB, tm, layer_params, states, kv_bufs):
    depth = len(layer_params)
    st_out, kv_out = None, (None if kv_bufs is None else [])
    for l, P in enumerate(layer_params):
        st = states[l]
        a16, a32 = _norm_proj(x, P["mix_norm"], P["w16"], P["w32"], tm)
        a16b, a32b = a16.reshape(B, L, W16), a32.reshape(B, L, W32)
        y_hg, s_hg, y_ssm, s_ssm, s_conv = _recurrent_mixers(
            a16b, a32b, P["lb"], P["hg_norm"], st["hgrn_t"], st["conv"], P["conv_w"], P["conv_b"], P["dt_bias"],
            P["a_log"], P["d_skip"], P["ssm_norm"], st["ssm"], B, L, C_h, HG_SUB, valid, NB, l, depth, st_out)
        st_out = (s_hg, s_ssm, s_conv)
        y_hg, y_ssm = y_hg.reshape(B * L, HG_WIDTH), y_ssm.reshape(B * L, SSM_WIDTH)
        if kv_bufs is None:
            y_att, *kv_out = _attn_prompt(a16, P["gq"], P["gk"], P["bd"], B, L, l, depth, kv_out)
        else:
            y_att, *kv = _attn_sample(a16, kv_bufs, l, P["gq"], P["gk"], P["bd"], B)
            kv_out.append(kv)
        x = _merge(y_hg, y_ssm, y_att, a16, x, P["w_out_hg"], P["w_out_ssm"], P["w_out_att"], P["w_o"], tm)
        if l % 2 == 0:
            F = P["ffn"]
            nxt = layer_params[l + 1]["moe"] if l + 1 < len(layer_params) else {}
            pending = [k for k in ("w_up", "wd") if k + "_f32" in nxt and k not in nxt]
            cast = [nxt[k + "_f32"].reshape(-1, nxt[k + "_f32"].shape[-1]) for k in pending]
            x, done = _ffn(x, P["ffn_norm"], F["w_up"], F["wd"], min(tm, 256), cast)
            for k, w in zip(pending, done):
                nxt[k] = w.reshape(nxt[k + "_f32"].shape)
        else:
            F = P["moe"]
            h, comb, combt = _router(x, P["ffn_norm"], F["wr"], tm)
            x = _moe(x, h, comb, combt, F["w_up"], F["wd"], min(MOE_BLOCK, x.shape[0]), min(MOE_SUB, x.shape[0]), MOE_SLOTS, 2)
    return x, st_out, kv_out


def kernel(x_prompt, x_sample, state_hgrn, state_ssm, state_conv, cache_kv_w128, cache_kv_w512, cache_kv_w2048, w_in, mix_norm, hgrn_lb_logits, hgrn_norm, ssm_conv_w, ssm_conv_b, ssm_dt_bias, ssm_a_log, ssm_d, ssm_norm, att_q_norm, att_k_norm, w_out_hgrn, w_out_ssm, w_out_att, w_o, ffn_norm, w_ffn_up, w_ffn_down, w_router, w_moe_up, w_moe_down):
    depth = w_in.shape[0]
    Bp, Lp, _ = x_prompt.shape
    Bs, Ls, _ = x_sample.shape
    R = SAMPLE_ROWS
    for buf, w in zip((cache_kv_w128, cache_kv_w512, cache_kv_w2048), ATT_WINDOWS):
        assert buf.shape[2] == w, "sample attention assumes full window buffers"

    lb_soft = jax.nn.softmax(hgrn_lb_logits.astype(F32), axis=0)
    lb_table = jnp.cumsum(lb_soft, axis=0) - lb_soft[0]
    bd = jnp.kron(jnp.eye(ATT_HEADS, dtype=F32), jnp.full((ATT_HEAD_DIM, ATT_HEAD_DIM), 1.0 / ATT_HEAD_DIM, F32)).astype(BF16)

    layer_params = []
    for l in range(depth):
        w16, w32 = _prep_w_in(w_in[l])
        P = dict(
            w16=w16, w32=w32,
            mix_norm=mix_norm[l].reshape(1, -1), ffn_norm=ffn_norm[l].reshape(1, -1),
            lb=lb_table[l].reshape(1, -1), hg_norm=hgrn_norm[l].reshape(1, -1),
            conv_w=jnp.pad(ssm_conv_w[l], ((0, 8 - SSM_CONV), (0, 0))), conv_b=ssm_conv_b[l].reshape(1, -1),
            dt_bias=_pad_lanes(ssm_dt_bias[l], LANE), a_log=_pad_lanes(ssm_a_log[l], LANE),
            d_skip=jnp.repeat(ssm_d[l].astype(F32), SSM_HEAD_DIM).reshape(1, -1),
            ssm_norm=ssm_norm[l].reshape(1, -1),
            gq=jnp.tile(att_q_norm[l], (1, ATT_HEADS)), gk=jnp.tile(att_k_norm[l], (1, ATT_HEADS)), bd=bd,
            w_out_hg=w_out_hgrn[l].astype(BF16), w_out_ssm=w_out_ssm[l].astype(BF16),
            w_out_att=w_out_att[l].astype(BF16), w_o=w_o[l].astype(BF16),
        )
        if l % 2 == 0:
            P["ffn"] = dict(w_up=w_ffn_up[l // 2].astype(BF16), wd=w_ffn_down[l // 2].astype(BF16))
        else:
            P["moe"] = dict(wr=jnp.pad(w_router[l // 2].astype(F32), ((0, 0), (0, LANE - N_EXPERTS))),
                            w_up_f32=w_moe_up[l // 2], wd_f32=w_moe_down[l // 2])
        layer_params.append(P)

    zero_p = dict(hgrn_t=jnp.zeros((Bp, HG_WIDTH, HG_DIM), F32), ssm=jnp.zeros((Bp, SSM_WIDTH, SSM_STATE), F32),
                  conv=jnp.zeros((Bp, 8, SSM_CONV_DIM), F32))
    xp, st_p, kv_p = _run_group(x_prompt.reshape(Bp * Lp, D_MODEL), Bp, Lp, 64, 64, 64, 4, 512,
                                layer_params, [zero_p] * depth, None)

    xs = jnp.pad(x_sample, ((0, 0), (0, R - Ls), (0, 0))).reshape(Bs * R, D_MODEL)
    st_s = []
    for l in range(depth):
        st_s.append(dict(
            hgrn_t=jnp.swapaxes(state_hgrn[l], -1, -2).reshape(Bs, HG_WIDTH, HG_DIM),
            ssm=state_ssm[l].reshape(Bs, SSM_WIDTH, SSM_STATE),
            conv=jnp.pad(state_conv[l], ((0, 0), (8 - (SSM_CONV - 1), 0), (0, 0)))))
    kv_s = [jnp.transpose(c, (0, 1, 3, 4, 5, 2)).reshape(depth, Bs, 2 * ATT_WIDTH, c.shape[2])
            for c in (cache_kv_w128, cache_kv_w512, cache_kv_w2048)]
    xs, st_s, kv_s = _run_group(xs, Bs, R, R, R, Ls, 4, Bs * R, layer_params, st_s, kv_s)

    def states(st, B):
        s_hg, s_ssm, s_conv = st
        return (jnp.swapaxes(s_hg.reshape(depth, B, HG_HEADS, HG_DIM, HG_DIM), -1, -2),
                s_ssm.reshape(depth, B, SSM_HEADS, SSM_HEAD_DIM, SSM_STATE),
                s_conv[:, :, 8 - (SSM_CONV - 1):])

    kv_prompt = [jnp.transpose(kv.reshape(depth, Bp, 2, ATT_HEADS, ATT_HEAD_DIM, kv.shape[3]), (0, 1, 5, 2, 3, 4))
                 for kv in kv_p]
    kv_sample = [jnp.stack([kv_s[l][j][:, :Ls].reshape(Bs, Ls, 2, ATT_HEADS, ATT_HEAD_DIM) for l in range(depth)])
                 for j in range(3)]
    y_prompt = xp.reshape(Bp, Lp, D_MODEL)
    y_sample = xs.reshape(Bs, R, D_MODEL)[:, :Ls]
    return (y_prompt, y_sample, *states(st_p, Bp), *kv_prompt, *states(st_s, Bs), *kv_sample)
```
